```python
import jax, jax.numpy as jnp
from jax import lax
import numpy as np

D_MODEL = 1024
BATCH = 2
SEQ = 8192
DEPTH = 2

CHUNK = 64
POOL_WIDTH = 256
POOL_GROUPS = 4
POOL_WINDOWS = (2, 4, 8, 16)
MLA_HEADS = 8
Q_LORA = 256
KV_LORA = 128
QK_NOPE = 64
QK_ROPE = 32
V_HEAD = 64
ROPE_THETA = 10000.0
Q_BLOCK = 128
CONV_WIDTH = 256
CONV_K = 31
SGU_WIDTH = 256
SGU_GROUPS = 4
SGU_LEN = 128
N_BRANCH = 4
D_FF = 2816
N_EXPERTS = 8
TOP_K = 2
D_FF_EXPERT = 3584
N_DENSE = (DEPTH + 1) // 2
N_MOE = DEPTH // 2
EPS = 1e-6

IN_SIZES = (POOL_WIDTH, Q_LORA, KV_LORA, QK_ROPE, 2 * CONV_WIDTH, 2 * SGU_WIDTH, N_BRANCH * D_MODEL)
D_IN = sum(IN_SIZES)
IN_SPLITS = tuple(sum(IN_SIZES[:i + 1]) for i in range(len(IN_SIZES) - 1))

kernel_name = "hybrid_gated_mixers_moe_encoder"


def rmsnorm(x, g):
    xf = x.astype(jnp.float32)
    y = xf * lax.rsqrt(jnp.mean(xf * xf, axis=-1, keepdims=True) + EPS)
    return (y * g.astype(jnp.float32)).astype(x.dtype)


def layernorm(x, g, b):
    xf = x.astype(jnp.float32)
    mu = jnp.mean(xf, axis=-1, keepdims=True)
    var = jnp.mean(jnp.square(xf - mu), axis=-1, keepdims=True)
    y = (xf - mu) * lax.rsqrt(var + EPS)
    return (y * g.astype(jnp.float32) + b.astype(jnp.float32)).astype(x.dtype)


def rope_tables(seq):
    pos = jnp.arange(seq, dtype=jnp.float32)
    inv = ROPE_THETA ** (-jnp.arange(0, QK_ROPE, 2, dtype=jnp.float32) / QK_ROPE)
    ang = pos[:, None] * inv[None, :]
    return jnp.cos(ang), jnp.sin(ang)


def apply_rope(x, cos, sin):
    half = x.shape[-1] // 2
    cos = cos.astype(x.dtype)
    sin = sin.astype(x.dtype)
    x1, x2 = x[..., :half], x[..., half:]
    return jnp.concatenate([x1 * cos - x2 * sin, x2 * cos + x1 * sin], axis=-1)


def pool_mixer(z, w_pool, pool_scale):
    b, s, c = z.shape
    cg = c // POOL_GROUPS
    zf = z.astype(jnp.float32)
    cs = jnp.concatenate([jnp.zeros((b, 1, c), jnp.float32), jnp.cumsum(zf, axis=1)], axis=1)
    outs = []
    for g, w in enumerate(POOL_WINDOWS):
        sl = slice(g * cg, (g + 1) * cg)
        upper = cs[:, 1:, sl]
        lower = jnp.concatenate([jnp.zeros((b, w - 1, cg), jnp.float32), cs[:, :s - w + 1, sl]], axis=1)
        cnt = jnp.minimum(jnp.arange(s) + 1, w).astype(jnp.float32)[None, :, None]
        outs.append((upper - lower) / cnt)
    pooled = (jnp.concatenate(outs, axis=-1) - zf).astype(z.dtype).reshape(b, s, POOL_GROUPS, cg)
    y = jnp.einsum('bsgc,gcd->bsgd', pooled, w_pool).reshape(b, s, c)
    return y * pool_scale


def mla_mixer(c_q_raw, c_kv_raw, k_rope_raw, q_norm_g, w_uq, kv_norm_g, w_ukv, cos, sin):
    b, s, _ = c_q_raw.shape
    h = MLA_HEADS
    q = (rmsnorm(c_q_raw, q_norm_g) @ w_uq).reshape(b, s, h, QK_NOPE + QK_ROPE)
    q_nope = q[..., :QK_NOPE]
    q_rope = apply_rope(q[..., QK_NOPE:], cos[None, :, None, :], sin[None, :, None, :])
    kv = (rmsnorm(c_kv_raw, kv_norm_g) @ w_ukv).reshape(b, s, h, QK_NOPE + V_HEAD)
    k_nope, v = kv[..., :QK_NOPE], kv[..., QK_NOPE:]
    k_rope = apply_rope(k_rope_raw, cos[None], sin[None])
    scale = (QK_NOPE + QK_ROPE) ** -0.5
    nb = s // Q_BLOCK
    qn_b = q_nope.reshape(b, nb, Q_BLOCK, h, QK_NOPE).transpose(1, 0, 2, 3, 4)
    qr_b = q_rope.reshape(b, nb, Q_BLOCK, h, QK_ROPE).transpose(1, 0, 2, 3, 4)
    k_chunk = jnp.arange(s) // CHUNK

    def attend(args):
        qn, qr, i = args
        sc = (jnp.einsum('bqhd,bkhd->bhqk', qn, k_nope, preferred_element_type=jnp.float32)
              + jnp.einsum('bqhr,bkr->bhqk', qr, k_rope, preferred_element_type=jnp.float32))
        q_chunk = (i * Q_BLOCK + jnp.arange(Q_BLOCK)) // CHUNK
        mask = k_chunk[None, :] <= q_chunk[:, None]
        sc = jnp.where(mask, sc * scale, -1e30)
        p = jax.nn.softmax(sc, axis=-1).astype(v.dtype)
        return jnp.einsum('bhqk,bkhd->bqhd', p, v)

    o = lax.map(attend, (qn_b, qr_b, jnp.arange(nb)))
    return o.transpose(1, 0, 2, 3, 4).reshape(b, s, h * V_HEAD)


def conv_mixer(z2, conv_w, conv_b, ln_g, ln_b):
    a, gt = jnp.split(z2, 2, axis=-1)
    u = a * jax.nn.sigmoid(gt)
    y = lax.conv_general_dilated(u, conv_w[:, None, :], window_strides=(1,),
                                 padding=((CONV_K - 1, 0),),
                                 dimension_numbers=('NWC', 'WIO', 'NWC'),
                                 feature_group_count=CONV_WIDTH) + conv_b
    return jax.nn.silu(layernorm(y, ln_g, ln_b))


def sgu_mixer(z2, ln_g, ln_b, w_s, b_s):
    z2 = jax.nn.gelu(z2, approximate=False)
    u, v = jnp.split(z2, 2, axis=-1)
    v = layernorm(v, ln_g, ln_b)
    b, s, c = v.shape
    n = s // SGU_LEN
    cg = c // SGU_GROUPS
    vb = v.reshape(b, n, SGU_LEN, SGU_GROUPS, cg)
    causal = jnp.tril(jnp.ones((SGU_LEN, SGU_LEN), dtype=bool))
    ws = jnp.where(causal[None], w_s, jnp.zeros_like(w_s))
    mixed = jnp.einsum('gts,bnsgc->bntgc', ws, vb) + b_s.T[None, None, :, :, None]
    return u * mixed.reshape(b, s, c)


def swiglu(h, wg, wu, wd):
    return (jax.nn.silu(h @ wg) * (h @ wu)) @ wd


def moe_ffn(h, w_router, b_router, wg, wu, wd):
    b, s, d = h.shape
    t = h.reshape(-1, d)
    logits = (t @ w_router).astype(jnp.float32) + b_router.astype(jnp.float32)
    vals, idx = lax.top_k(logits, TOP_K)
    wts = jax.nn.softmax(vals, axis=-1)
    combine = jnp.sum(jax.nn.one_hot(idx, N_EXPERTS, dtype=jnp.float32) * wts[..., None], axis=1).astype(t.dtype)
    acc = jnp.zeros_like(t)
    for e in range(N_EXPERTS):
        acc = acc + combine[:, e:e + 1] * swiglu(t, wg[e], wu[e], wd[e])
    return acc.reshape(b, s, d)


def setup_inputs(seed: int = 0) -> dict:
    key = jax.random.key(seed)
    ks = iter(jax.random.split(key, 64))
    L = DEPTH

    def nrm(shape, scale):
        return jax.random.normal(next(ks), shape, jnp.float32) * scale

    def gain(shape):
        return 1.0 + 0.02 * jax.random.normal(next(ks), shape, jnp.float32)

    cgp = POOL_WIDTH // POOL_GROUPS
    inp = {}
    inp['x'] = nrm((BATCH, SEQ, D_MODEL), 1.0)
    inp['norm_mix_g'] = gain((L, D_MODEL))
    inp['w_in'] = nrm((L, D_MODEL, D_IN), D_MODEL ** -0.5)
    inp['b_gate'] = nrm((L, N_BRANCH * D_MODEL), 0.02)
    inp['w_pool'] = nrm((L, POOL_GROUPS, cgp, cgp), cgp ** -0.5)
    inp['pool_scale'] = gain((L, POOL_WIDTH))
    inp['w_pool_out'] = nrm((L, POOL_WIDTH, D_MODEL), POOL_WIDTH ** -0.5)
    inp['q_norm_g'] = gain((L, Q_LORA))
    inp['w_uq'] = nrm((L, Q_LORA, MLA_HEADS * (QK_NOPE + QK_ROPE)), Q_LORA ** -0.5)
    inp['kv_norm_g'] = gain((L, KV_LORA))
    inp['w_ukv'] = nrm((L, KV_LORA, MLA_HEADS * (QK_NOPE + V_HEAD)), KV_LORA ** -0.5)
    inp['w_mla_out'] = nrm((L, MLA_HEADS * V_HEAD, D_MODEL), (MLA_HEADS * V_HEAD) ** -0.5)
    inp['conv_w'] = nrm((L, CONV_K, CONV_WIDTH), CONV_K ** -0.5)
    inp['conv_b'] = nrm((L, CONV_WIDTH), 0.02)
    inp['conv_ln_g'] = gain((L, CONV_WIDTH))
    inp['conv_ln_b'] = nrm((L, CONV_WIDTH), 0.02)
    inp['w_conv_out'] = nrm((L, CONV_WIDTH, D_MODEL), CONV_WIDTH ** -0.5)
    inp['sgu_ln_g'] = gain((L, SGU_WIDTH // 1 // 1 if False else SGU_WIDTH))
    inp['sgu_ln_b'] = nrm((L, SGU_WIDTH), 0.02)
    inp['sgu_w'] = nrm((L, SGU_GROUPS, SGU_LEN, SGU_LEN), SGU_LEN ** -0.5)
    inp['sgu_b'] = gain((L, SGU_GROUPS, SGU_LEN))
    inp['w_sgu_out'] = nrm((L, SGU_WIDTH, D_MODEL), SGU_WIDTH ** -0.5)
    inp['w_o'] = nrm((L, D_MODEL, D_MODEL), 0.5 * D_MODEL ** -0.5)
    inp['norm_ffn_g'] = gain((L, D_MODEL))
    inp['w_ffn_gate'] = nrm((N_DENSE, D_MODEL, D_FF), D_MODEL ** -0.5)
    inp['w_ffn_up'] = nrm((N_DENSE, D_MODEL, D_FF), D_MODEL ** -0.5)
    inp['w_ffn_down'] = nrm((N_DENSE, D_FF, D_MODEL), D_FF ** -0.5)
    inp['w_router'] = nrm((N_MOE, D_MODEL, N_EXPERTS), D_MODEL ** -0.5)
    inp['b_router'] = nrm((N_MOE, N_EXPERTS), 0.01)
    inp['w_moe_gate'] = nrm((N_MOE, N_EXPERTS, D_MODEL, D_FF_EXPERT), D_MODEL ** -0.5)
    inp['w_moe_up'] = nrm((N_MOE, N_EXPERTS, D_MODEL, D_FF_EXPERT), D_MODEL ** -0.5)
    inp['w_moe_down'] = nrm((N_MOE, N_EXPERTS, D_FF_EXPERT, D_MODEL), D_FF_EXPERT ** -0.5)
    inp['final_norm_g'] = gain((D_MODEL,))
    return inp


def reference(x, norm_mix_g, w_in, b_gate, w_pool, pool_scale, w_pool_out, q_norm_g, w_uq,
              kv_norm_g, w_ukv, w_mla_out, conv_w, conv_b, conv_ln_g, conv_ln_b, w_conv_out,
              sgu_ln_g, sgu_ln_b, sgu_w, sgu_b, w_sgu_out, w_o, norm_ffn_g, w_ffn_gate,
              w_ffn_up, w_ffn_down, w_router, b_router, w_moe_gate, w_moe_up, w_moe_down,
              final_norm_g):
    b, s, d = x.shape
    cos, sin = rope_tables(s)
    for l in range(DEPTH):
        h = rmsnorm(x, norm_mix_g[l])
        z = h @ w_in[l]
        zp, cq, ckv, kr, zc, zs, zg = jnp.split(z, IN_SPLITS, axis=-1)
        y_pool = pool_mixer(zp, w_pool[l], pool_scale[l]) @ w_pool_out[l]
        y_mla = mla_mixer(cq, ckv, kr, q_norm_g[l], w_uq[l], kv_norm_g[l], w_ukv[l], cos, sin) @ w_mla_out[l]
        y_conv = conv_mixer(zc, conv_w[l], conv_b[l], conv_ln_g[l], conv_ln_b[l]) @ w_conv_out[l]
        y_sgu = sgu_mixer(zs, sgu_ln_g[l], sgu_ln_b[l], sgu_w[l], sgu_b[l]) @ w_sgu_out[l]
        gates = jax.nn.sigmoid(zg + b_gate[l]).reshape(b, s, N_BRANCH, d)
        merged = (gates[:, :, 0] * y_pool + gates[:, :, 1] * y_mla
                  + gates[:, :, 2] * y_conv + gates[:, :, 3] * y_sgu)
        x = x + merged @ w_o[l]
        h = rmsnorm(x, norm_ffn_g[l])
        if l % 2 == 0:
            j = l // 2
            x = x + swiglu(h, w_ffn_gate[j], w_ffn_up[j], w_ffn_down[j])
        else:
            j = l // 2
            x = x + moe_ffn(h, w_router[j], b_router[j], w_moe_gate[j], w_moe_up[j], w_moe_down[j])
    return rmsnorm(x, final_norm_g)
```

```python
import functools

import jax
import jax.numpy as jnp
from jax import lax
from jax.experimental import pallas as pl
from jax.experimental.pallas import tpu as pltpu

F32 = jnp.float32
BF16 = jnp.bfloat16

D_MODEL = 1024
SEQ = 8192
CHUNK = 64
POOL_WIDTH = 256
POOL_WINDOWS = (2, 4, 8, 16)
MLA_HEADS = 8
Q_LORA = 256
KV_LORA = 128
QK_NOPE = 64
QK_ROPE = 32
V_HEAD = 64
ROPE_THETA = 10000.0
CONV_WIDTH = 256
CONV_K = 31
SGU_WIDTH = 256
SGU_GROUPS = 4
SGU_LEN = 128
N_BRANCH = 4
D_FF = 2816
N_EXPERTS = 8
D_FF_EXPERT = 3584
EPS = 1e-6
NEG = -1e30

LANES = 128
HEAD_PAD = 128
ZC_COL, ZS_COL, ZP_COL, CQ_COL, CKV_COL, KR_COL = 0, 512, 1024, 1280, 1536, 1664
Z_SMALL = 1792
POOL_HALO = 16
CONV_HALO = 32

TM = 512
TQ = 256
TM_MOE = 1024
TF_MOE = 512
VMEM_LIMIT = 56 * 1024 * 1024


def _params(sem):
    return pltpu.CompilerParams(dimension_semantics=sem, vmem_limit_bytes=VMEM_LIMIT)


def _rms(x, g):
    return x * lax.rsqrt(jnp.mean(x * x, axis=-1, keepdims=True) + EPS) * g


def _layernorm(x, g, b):
    mu = jnp.mean(x, axis=-1, keepdims=True)
    xc = x - mu
    var = jnp.mean(xc * xc, axis=-1, keepdims=True)
    return xc * lax.rsqrt(var + EPS) * g + b


def _sigmoid(x):
    return 1.0 / (1.0 + jnp.exp(-x))


def _dot(a, b):
    return jnp.dot(a, b, preferred_element_type=F32)


def _in_proj_kernel(x_ref, g_ref, ws_ref, wg_ref, zs_ref, zg_ref):
    h = _rms(x_ref[...], g_ref[...]).astype(BF16)
    for n0 in range(0, Z_SMALL, 896):
        zs_ref[:, n0:n0 + 896] = _dot(h, ws_ref[:, n0:n0 + 896])
    for n0 in range(0, N_BRANCH * D_MODEL, 1024):
        zg_ref[:, n0:n0 + 1024] = _dot(h, wg_ref[:, n0:n0 + 1024]).astype(BF16)


def _in_proj(x, g, w_small, w_gate):
    t = x.shape[0]
    return pl.pallas_call(
        _in_proj_kernel,
        grid=(t // TM,),
        in_specs=[
            pl.BlockSpec((TM, D_MODEL), lambda i: (i, 0)),
            pl.BlockSpec((1, D_MODEL), lambda i: (0, 0)),
            pl.BlockSpec((D_MODEL, Z_SMALL), lambda i: (0, 0)),
            pl.BlockSpec((D_MODEL, N_BRANCH * D_MODEL), lambda i: (0, 0)),
        ],
        out_specs=[
            pl.BlockSpec((TM, Z_SMALL), lambda i: (i, 0)),
            pl.BlockSpec((TM, N_BRANCH * D_MODEL), lambda i: (i, 0)),
        ],
        out_shape=[
            jax.ShapeDtypeStruct((t, Z_SMALL), F32),
            jax.ShapeDtypeStruct((t, N_BRANCH * D_MODEL), BF16),
        ],
        compiler_params=_params(("parallel",)),
        name="in_proj",
    )(x, g, w_small, w_gate)


def _pool_kernel(z_ref, halo_ref, w_ref, scale_ref, o_ref, buf_ref):
    i = pl.program_id(0)
    pos0 = (i * TM) % SEQ
    z = z_ref[...]
    buf_ref[0:POOL_HALO, :] = jnp.where(pos0 == 0, 0.0, halo_ref[...])
    buf_ref[POOL_HALO:, :] = z
    lane = lax.broadcasted_iota(jnp.int32, (TM, POOL_WIDTH), 1)
    pos = lax.broadcasted_iota(jnp.int32, (TM, POOL_WIDTH), 0) + pos0
    group = lane // (POOL_WIDTH // len(POOL_WINDOWS))
    acc = z
    total = jnp.zeros_like(z)
    win = jnp.zeros_like(lane)
    prev_w = 1
    for gi, w in enumerate(POOL_WINDOWS):
        for j in range(prev_w, w):
            acc = acc + buf_ref[POOL_HALO - j:POOL_HALO - j + TM, :]
        prev_w = w
        total = jnp.where(group == gi, acc, total)
        win = jnp.where(group == gi, w, win)
    cnt = jnp.minimum(pos + 1, win).astype(F32)
    pooled = total / cnt - z
    y = _dot(pooled.astype(BF16), w_ref[...]) * scale_ref[...]
    o_ref[...] = y.astype(BF16)


def _pool(zs, w_bd, scale):
    t = zs.shape[0]
    cb = ZP_COL // POOL_WIDTH
    r = TM // POOL_HALO
    return pl.pallas_call(
        _pool_kernel,
        grid=(t // TM,),
        in_specs=[
            pl.BlockSpec((TM, POOL_WIDTH), lambda i: (i, cb)),
            pl.BlockSpec((POOL_HALO, POOL_WIDTH), lambda i: (jnp.maximum(i * r - 1, 0), cb)),
            pl.BlockSpec((POOL_WIDTH, POOL_WIDTH), lambda i: (0, 0)),
            pl.BlockSpec((1, POOL_WIDTH), lambda i: (0, 0)),
        ],
        out_specs=pl.BlockSpec((TM, POOL_WIDTH), lambda i: (i, 0)),
        out_shape=jax.ShapeDtypeStruct((t, POOL_WIDTH), BF16),
        scratch_shapes=[pltpu.VMEM((TM + POOL_HALO, POOL_WIDTH), F32)],
        compiler_params=_params(("parallel",)),
        name="pool_mixer",
    )(zs, zs, w_bd, scale)


def _glu(z2):
    return z2[:, :CONV_WIDTH] * _sigmoid(z2[:, CONV_WIDTH:])


def _conv_kernel(z_ref, halo_ref, w_ref, b_ref, lg_ref, lb_ref, o_ref, buf_ref):
    i = pl.program_id(0)
    pos0 = (i * TM) % SEQ
    buf_ref[0:CONV_HALO, :] = jnp.where(pos0 == 0, 0.0, _glu(halo_ref[...]))
    buf_ref[CONV_HALO:, :] = _glu(z_ref[...])
    off = CONV_HALO - (CONV_K - 1)
    y = jnp.zeros((TM, CONV_WIDTH), F32) + b_ref[...]
    for k in range(CONV_K):
        y = y + w_ref[k:k + 1, :] * buf_ref[off + k:off + k + TM, :]
    yn = _layernorm(y, lg_ref[...], lb_ref[...])
    o_ref[...] = (yn * _sigmoid(yn)).astype(BF16)


def _conv(zs, conv_w, conv_b, ln_g, ln_b):
    t = zs.shape[0]
    cb = ZC_COL // (2 * CONV_WIDTH)
    r = TM // CONV_HALO
    vec = pl.BlockSpec((1, CONV_WIDTH), lambda i: (0, 0))
    return pl.pallas_call(
        _conv_kernel,
        grid=(t // TM,),
        in_specs=[
            pl.BlockSpec((TM, 2 * CONV_WIDTH), lambda i: (i, cb)),
            pl.BlockSpec((CONV_HALO, 2 * CONV_WIDTH), lambda i: (jnp.maximum(i * r - 1, 0), cb)),
            pl.BlockSpec((CONV_K, CONV_WIDTH), lambda i: (0, 0)),
            vec, vec, vec,
        ],
        out_specs=pl.BlockSpec((TM, CONV_WIDTH), lambda i: (i, 0)),
        out_shape=jax.ShapeDtypeStruct((t, CONV_WIDTH), BF16),
        scratch_shapes=[pltpu.VMEM((TM + CONV_HALO, CONV_WIDTH), F32)],
        compiler_params=_params(("parallel",)),
        name="conv_mixer",
    )(zs, zs, conv_w, conv_b, ln_g, ln_b)


def _sgu_kernel(z_ref, lg_ref, lb_ref, ws_ref, bias_ref, o_ref):
    z = z_ref[...]
    z = 0.5 * z * (1.0 + lax.erf(z * (2.0 ** -0.5)))
    u = z[:, :SGU_WIDTH]
    v = _layernorm(z[:, SGU_WIDTH:], lg_ref[...], lb_ref[...]).astype(BF16)
    rows = SGU_GROUPS * SGU_LEN
    row = lax.broadcasted_iota(jnp.int32, (rows, SGU_LEN), 0) % SGU_LEN
    col = lax.broadcasted_iota(jnp.int32, (rows, SGU_LEN), 1)
    ws = jnp.where(col <= row, ws_ref[...], 0.0).astype(BF16)
    lane_group = lax.broadcasted_iota(jnp.int32, (SGU_LEN, SGU_WIDTH), 1) // (SGU_WIDTH // SGU_GROUPS)
    for blk in range(TM // SGU_LEN):
        r0 = blk * SGU_LEN
        full = _dot(ws, v[r0:r0 + SGU_LEN, :])
        mixed = full[0:SGU_LEN, :]
        for g in range(1, SGU_GROUPS):
            mixed = jnp.where(lane_group == g, full[g * SGU_LEN:(g + 1) * SGU_LEN, :], mixed)
        mixed = mixed + bias_ref[...]
        o_ref[r0:r0 + SGU_LEN, :] = (u[r0:r0 + SGU_LEN, :] * mixed).astype(BF16)


def _sgu(zs, ln_g, ln_b, ws_stack, bias_full):
    t = zs.shape[0]
    cb = ZS_COL // (2 * SGU_WIDTH)
    vec = pl.BlockSpec((1, SGU_WIDTH), lambda i: (0, 0))
    return pl.pallas_call(
        _sgu_kernel,
        grid=(t // TM,),
        in_specs=[
            pl.BlockSpec((TM, 2 * SGU_WIDTH), lambda i: (i, cb)),
            vec, vec,
            pl.BlockSpec((SGU_GROUPS * SGU_LEN, SGU_LEN), lambda i: (0, 0)),
            pl.BlockSpec((SGU_LEN, SGU_WIDTH), lambda i: (0, 0)),
        ],
        out_specs=pl.BlockSpec((TM, SGU_WIDTH), lambda i: (i, 0)),
        out_shape=jax.ShapeDtypeStruct((t, SGU_WIDTH), BF16),
        compiler_params=_params(("parallel",)),
        name="sgu_mixer",
    )(zs, ln_g, ln_b, ws_stack, bias_full)


def _tile_heads(tab):
    return jnp.concatenate([tab] * MLA_HEADS, axis=1)


def _mla_prep_kernel(cq_ref, ckv_ref, kr_ref, qg_ref, kvg_ref, wqa_ref, wqb_ref, wk_ref, wv_ref,
                     place_ref, cq_tab_ref, sq_tab_ref, ck_tab_ref, sk_tab_ref,
                     q_ref, k_ref, v_ref):
    cqn = _rms(cq_ref[...], qg_ref[...]).astype(BF16)
    qa = _dot(cqn, wqa_ref[...])
    qb = _dot(cqn, wqb_ref[...])
    scale = (QK_NOPE + QK_ROPE) ** -0.5
    q = (qa * _tile_heads(cq_tab_ref[...]) + qb * _tile_heads(sq_tab_ref[...])) * scale
    q_ref[...] = q.astype(BF16)
    kvn = _rms(ckv_ref[...], kvg_ref[...]).astype(BF16)
    v_ref[...] = _dot(kvn, wv_ref[...]).astype(BF16)
    kr = kr_ref[...]
    kro = kr * ck_tab_ref[...] + pltpu.roll(kr, LANES - QK_ROPE, 1) * sk_tab_ref[...]
    k = _dot(kvn, wk_ref[...]) + _dot(kro.astype(BF16), place_ref[...])
    k_ref[...] = k.astype(BF16)


def _mla_prep(zs, q_g, kv_g, wqa, wqb, wk, wv, place, tabs):
    t = zs.shape[0]
    hw = MLA_HEADS * HEAD_PAD
    ns = SEQ // TM
    tab = pl.BlockSpec((TM, LANES), lambda i: (i % ns, 0))
    full = lambda a: pl.BlockSpec(a.shape, lambda i: (0, 0))
    return pl.pallas_call(
        _mla_prep_kernel,
        grid=(t // TM,),
        in_specs=[
            pl.BlockSpec((TM, Q_LORA), lambda i: (i, CQ_COL // Q_LORA)),
            pl.BlockSpec((TM, KV_LORA), lambda i: (i, CKV_COL // KV_LORA)),
            pl.BlockSpec((TM, LANES), lambda i: (i, KR_COL // LANES)),
            full(q_g), full(kv_g), full(wqa), full(wqb), full(wk), full(wv), full(place),
            tab, tab, tab, tab,
        ],
        out_specs=[
            pl.BlockSpec((TM, hw), lambda i: (i, 0)),
            pl.BlockSpec((TM, hw), lambda i: (i, 0)),
            pl.BlockSpec((TM, MLA_HEADS * V_HEAD), lambda i: (i, 0)),
        ],
        out_shape=[
            jax.ShapeDtypeStruct((t, hw), BF16),
            jax.ShapeDtypeStruct((t, hw), BF16),
            jax.ShapeDtypeStruct((t, MLA_HEADS * V_HEAD), BF16),
        ],
        compiler_params=_params(("parallel",)),
        name="mla_prep",
    )(zs, zs, zs, q_g, kv_g, wqa, wqb, wk, wv, place, *tabs)


def _attn_kernel(q_ref, k_ref, v_ref, o_ref, m_ref, l_ref, acc_ref):
    qi = pl.program_id(2)
    m_ref[...] = jnp.full(m_ref.shape, NEG, F32)
    l_ref[...] = jnp.zeros(l_ref.shape, F32)
    acc_ref[...] = jnp.zeros(acc_ref.shape, F32)
    row_chunk = lax.broadcasted_iota(jnp.int32, (TQ, TQ), 0) // CHUNK
    col_chunk = lax.broadcasted_iota(jnp.int32, (TQ, TQ), 1) // CHUNK
    visible = col_chunk <= row_chunk

    def step(j, masked):
        ks = pl.multiple_of(j * TQ, TQ)
        v = v_ref[pl.ds(ks, TQ), :]
        for h in range(2):
            q = q_ref[:, h * HEAD_PAD:(h + 1) * HEAD_PAD]
            k = k_ref[pl.ds(ks, TQ), h * HEAD_PAD:(h + 1) * HEAD_PAD]
            s = lax.dot_general(q, k, (((1,), (1,)), ((), ())), preferred_element_type=F32)
            if masked:
                s = jnp.where(visible, s, NEG)
            m_prev = m_ref[h]
            m_new = jnp.maximum(m_prev, jnp.max(s, axis=1, keepdims=True))
            alpha = jnp.exp(m_prev - m_new)
            p = jnp.exp(s - m_new)
            l_ref[h] = alpha * l_ref[h] + jnp.sum(p, axis=1, keepdims=True)
            acc_ref[h] = alpha * acc_ref[h] + _dot(p.astype(BF16), v)
            m_ref[h] = m_new

    def body(j, carry):
        step(j, False)
        return carry

    lax.fori_loop(0, qi, body, 0)
    step(qi, True)
    o0 = acc_ref[0] / l_ref[0]
    o1 = acc_ref[1] / l_ref[1]
    lane = lax.broadcasted_iota(jnp.int32, (TQ, 2 * V_HEAD), 1)
    o_ref[...] = jnp.where(lane < V_HEAD, o0, o1).astype(BF16)


def _attention(q, k, v, batch):
    t = q.shape[0]
    nq = SEQ // TQ
    pairs = MLA_HEADS // 2
    return pl.pallas_call(
        _attn_kernel,
        grid=(batch, pairs, nq),
        in_specs=[
            pl.BlockSpec((TQ, 2 * HEAD_PAD), lambda b, p, i: (b * nq + i, p)),
            pl.BlockSpec((SEQ, 2 * HEAD_PAD), lambda b, p, i: (b, p)),
            pl.BlockSpec((SEQ, 2 * V_HEAD), lambda b, p, i: (b, p)),
        ],
        out_specs=pl.BlockSpec((TQ, 2 * V_HEAD), lambda b, p, i: (b * nq + i, p)),
        out_shape=jax.ShapeDtypeStruct((t, MLA_HEADS * V_HEAD), BF16),
        scratch_shapes=[
            pltpu.VMEM((2, TQ, 1), F32),
            pltpu.VMEM((2, TQ, 1), F32),
            pltpu.VMEM((2, TQ, 2 * V_HEAD), F32),
        ],
        compiler_params=_params(("parallel", "parallel", "arbitrary")),
        name="mla_attention",
    )(q, k, v)


def _merge_kernel(x_ref, mp_ref, ma_ref, mc_ref, ms_ref, zg_ref, bg_ref,
                  wp_ref, wa_ref, wc_ref, ws_ref, wo_ref, o_ref, mg_ref):
    branches = ((mp_ref, wp_ref), (ma_ref, wa_ref), (mc_ref, wc_ref), (ms_ref, ws_ref))
    half = D_MODEL // 2
    for n0 in range(0, D_MODEL, half):
        merged = jnp.zeros((TM, half), F32)
        for b, (m_ref, w_ref) in enumerate(branches):
            c0 = b * D_MODEL + n0
            gate = _sigmoid(zg_ref[:, c0:c0 + half].astype(F32) + bg_ref[:, c0:c0 + half])
            merged = merged + gate * _dot(m_ref[...], w_ref[:, n0:n0 + half])
        mg_ref[:, n0:n0 + half] = merged.astype(BF16)
    o_ref[...] = x_ref[...] + _dot(mg_ref[...], wo_ref[...])


def _merge(x, mp, ma, mc, ms, zg, b_gate, wp, wa, wc, ws, wo):
    t = x.shape[0]
    row = lambda w: pl.BlockSpec((TM, w), lambda i: (i, 0))
    full = lambda a: pl.BlockSpec(a.shape, lambda i: (0, 0))
    return pl.pallas_call(
        _merge_kernel,
        grid=(t // TM,),
        in_specs=[row(D_MODEL), row(POOL_WIDTH), row(MLA_HEADS * V_HEAD), row(CONV_WIDTH), row(SGU_WIDTH),
                  row(N_BRANCH * D_MODEL), full(b_gate), full(wp), full(wa), full(wc), full(ws), full(wo)],
        out_specs=row(D_MODEL),
        out_shape=jax.ShapeDtypeStruct((t, D_MODEL), F32),
        scratch_shapes=[pltpu.VMEM((TM, D_MODEL), BF16)],
        compiler_params=_params(("parallel",)),
        name="merge_out",
    )(x, mp, ma, mc, ms, zg, b_gate, wp, wa, wc, ws, wo)


def _ffn_kernel(x_ref, g_ref, wg_ref, wu_ref, wd_ref, o_ref):
    x = x_ref[...]
    h = _rms(x, g_ref[...]).astype(BF16)
    half = D_FF // 2
    acc = x
    for f0 in range(0, D_FF, half):
        gate = _dot(h, wg_ref[:, f0:f0 + half])
        up = _dot(h, wu_ref[:, f0:f0 + half])
        act = (gate * _sigmoid(gate) * up).astype(BF16)
        acc = acc + _dot(act, wd_ref[f0:f0 + half, :])
    o_ref[...] = acc


def _ffn(x, g, wg, wu, wd):
    t = x.shape[0]
    full = lambda a: pl.BlockSpec(a.shape, lambda i: (0, 0))
    return pl.pallas_call(
        _ffn_kernel,
        grid=(t // TM,),
        in_specs=[pl.BlockSpec((TM, D_MODEL), lambda i: (i, 0)), full(g), full(wg), full(wu), full(wd)],
        out_specs=pl.BlockSpec((TM, D_MODEL), lambda i: (i, 0)),
        out_shape=jax.ShapeDtypeStruct((t, D_MODEL), F32),
        compiler_params=_params(("parallel",)),
        name="ffn_dense",
    )(x, g, wg, wu, wd)


def _router_kernel(x_ref, g_ref, wr_ref, br_ref, c_ref):
    h = _rms(x_ref[...], g_ref[...])
    logits = jnp.dot(h, wr_ref[...], preferred_element_type=F32,
                     precision=lax.Precision.HIGHEST) + br_ref[...]
    lane = lax.broadcasted_iota(jnp.int32, logits.shape, 1).astype(F32)
    logits = jnp.where(lane < N_EXPERTS, logits, NEG)
    v1 = jnp.max(logits, axis=1, keepdims=True)
    i1 = jnp.min(jnp.where(logits == v1, lane, float(LANES)), axis=1, keepdims=True)
    rest = jnp.where(lane == i1, NEG, logits)
    v2 = jnp.max(rest, axis=1, keepdims=True)
    i2 = jnp.min(jnp.where(rest == v2, lane, float(LANES)), axis=1, keepdims=True)
    e = jnp.exp(v2 - v1)
    w1 = 1.0 / (1.0 + e)
    w2 = e / (1.0 + e)
    c_ref[...] = jnp.where(lane == i1, w1, 0.0) + jnp.where(lane == i2, w2, 0.0)


def _router(x, g, wr, br):
    t = x.shape[0]
    full = lambda a: pl.BlockSpec(a.shape, lambda i: (0, 0))
    return pl.pallas_call(
        _router_kernel,
        grid=(t // TM,),
        in_specs=[pl.BlockSpec((TM, D_MODEL), lambda i: (i, 0)), full(g), full(wr), full(br)],
        out_specs=pl.BlockSpec((TM, LANES), lambda i: (i, 0)),
        out_shape=jax.ShapeDtypeStruct((t, LANES), F32),
        compiler_params=_params(("parallel",)),
        name="moe_router",
    )(x, g, wr, br)


def _moe_kernel(x_ref, g_ref, c_ref, wg_ref, wu_ref, wd_ref, fg_ref, o_ref, h_ref, acc_ref):
    e = pl.program_id(1)
    f = pl.program_id(2)

    @pl.when((e == 0) & (f == 0))
    def _():
        h_ref[...] = _rms(x_ref[...], g_ref[...]).astype(BF16)
        acc_ref[...] = jnp.zeros(acc_ref.shape, F32)

    h = h_ref[...]
    gate = _dot(h, wg_ref[...])
    up = _dot(h, wu_ref[...])
    lane = lax.broadcasted_iota(jnp.int32, (TM_MOE, LANES), 1)
    ce = jnp.sum(jnp.where(lane == e, c_ref[...], 0.0), axis=1, keepdims=True)
    act = (gate * _sigmoid(gate) * up * ce).astype(BF16)
    acc_ref[...] += _dot(act, wd_ref[...])

    @pl.when((e == N_EXPERTS - 1) & (f == pl.num_programs(2) - 1))
    def _():
        o_ref[...] = _rms(x_ref[...] + acc_ref[...], fg_ref[...])


def _moe(x, g, combine, wg, wu, wd, final_g):
    t = x.shape[0]
    nf = D_FF_EXPERT // TF_MOE
    vec = pl.BlockSpec((1, D_MODEL), lambda i, e, f: (0, 0))
    return pl.pallas_call(
        _moe_kernel,
        grid=(t // TM_MOE, N_EXPERTS, nf),
        in_specs=[
            pl.BlockSpec((TM_MOE, D_MODEL), lambda i, e, f: (i, 0)),
            vec,
            pl.BlockSpec((TM_MOE, LANES), lambda i, e, f: (i, 0)),
            pl.BlockSpec((None, D_MODEL, TF_MOE), lambda i, e, f: (e, 0, f)),
            pl.BlockSpec((None, D_MODEL, TF_MOE), lambda i, e, f: (e, 0, f)),
            pl.BlockSpec((None, TF_MOE, D_MODEL), lambda i, e, f: (e, f, 0)),
            vec,
        ],
        out_specs=pl.BlockSpec((TM_MOE, D_MODEL), lambda i, e, f: (i, 0)),
        out_shape=jax.ShapeDtypeStruct((t, D_MODEL), F32),
        scratch_shapes=[pltpu.VMEM((TM_MOE, D_MODEL), BF16), pltpu.VMEM((TM_MOE, D_MODEL), F32)],
        compiler_params=_params(("parallel", "arbitrary", "arbitrary")),
        name="moe_experts",
    )(x, g, combine, wg, wu, wd, final_g)


def _rotate_half_cols(w):
    half = w.shape[-1] // 2
    return jnp.concatenate([-w[..., half:], w[..., :half]], axis=-1)


def _rope_tables():
    pos = jnp.arange(SEQ, dtype=F32)
    inv = ROPE_THETA ** (-jnp.arange(0, QK_ROPE, 2, dtype=F32) / QK_ROPE)
    ang = pos[:, None] * inv[None, :]
    cos, sin = jnp.cos(ang), jnp.sin(ang)
    one = jnp.ones((SEQ, QK_NOPE), F32)
    z_nope = jnp.zeros((SEQ, QK_NOPE), F32)
    z_pad = jnp.zeros((SEQ, HEAD_PAD - QK_NOPE - QK_ROPE), F32)
    z_rest = jnp.zeros((SEQ, LANES - QK_ROPE), F32)
    cq_tab = jnp.concatenate([one, cos, cos, z_pad], axis=1)
    sq_tab = jnp.concatenate([z_nope, sin, sin, z_pad], axis=1)
    ck_tab = jnp.concatenate([cos, cos, z_rest], axis=1)
    sk_tab = jnp.concatenate([sin, sin, z_rest], axis=1)
    return cq_tab, sq_tab, ck_tab, sk_tab


def _rope_placement():
    r = jnp.arange(QK_ROPE)
    place = jnp.zeros((LANES, MLA_HEADS * HEAD_PAD), F32)
    for h in range(MLA_HEADS):
        place = place.at[r, h * HEAD_PAD + QK_NOPE + r].set(1.0)
    return place.astype(BF16)


def _split_in_proj(w):
    zp, cq, ckv, kr, zc, zs, zg = jnp.split(
        w, (256, 512, 640, 672, 1184, 1696), axis=1)
    pad = jnp.zeros((D_MODEL, LANES - 2 * QK_ROPE), w.dtype)
    small = jnp.concatenate([zc, zs, zp, cq, ckv, kr, _rotate_half_cols(kr), pad], axis=1)
    return small.astype(BF16), zg.astype(BF16)


def _mla_weights(w_uq, w_ukv):
    wq = w_uq.reshape(Q_LORA, MLA_HEADS, QK_NOPE + QK_ROPE)
    nope, rope = wq[..., :QK_NOPE], wq[..., QK_NOPE:]
    zq = jnp.zeros((Q_LORA, MLA_HEADS, HEAD_PAD - QK_NOPE - QK_ROPE), w_uq.dtype)
    wqa = jnp.concatenate([nope, rope, zq], axis=-1).reshape(Q_LORA, MLA_HEADS * HEAD_PAD)
    wqb = jnp.concatenate([jnp.zeros_like(nope), _rotate_half_cols(rope), zq], axis=-1)
    wqb = wqb.reshape(Q_LORA, MLA_HEADS * HEAD_PAD)
    wkv = w_ukv.reshape(KV_LORA, MLA_HEADS, QK_NOPE + V_HEAD)
    k_nope, v = wkv[..., :QK_NOPE], wkv[..., QK_NOPE:]
    zk = jnp.zeros((KV_LORA, MLA_HEADS, HEAD_PAD - QK_NOPE), w_ukv.dtype)
    wk = jnp.concatenate([k_nope, zk], axis=-1).reshape(KV_LORA, MLA_HEADS * HEAD_PAD)
    wv = v.reshape(KV_LORA, MLA_HEADS * V_HEAD)
    return wqa.astype(BF16), wqb.astype(BF16), wk.astype(BF16), wv.astype(BF16)


def kernel(x, norm_mix_g, w_in, b_gate, w_pool, pool_scale, w_pool_out, q_norm_g, w_uq, kv_norm_g, w_ukv, w_mla_out, conv_w, conv_b, conv_ln_g, conv_ln_b, w_conv_out, sgu_ln_g, sgu_ln_b, sgu_w, sgu_b, w_sgu_out, w_o, norm_ffn_g, w_ffn_gate, w_ffn_up, w_ffn_down, w_router, b_router, w_moe_gate, w_moe_up, w_moe_down, final_norm_g):
    batch, seq, d = x.shape
    assert (seq, d) == (SEQ, D_MODEL)
    depth = w_in.shape[0]
    xt = x.reshape(batch * seq, d)
    tabs = _rope_tables()
    place = _rope_placement()
    row = lambda a: a.reshape(1, -1)

    for l in range(depth):
        w_small, w_gate = _split_in_proj(w_in[l])
        zs, zg = _in_proj(xt, row(norm_mix_g[l]), w_small, w_gate)

        w_bd = jax.scipy.linalg.block_diag(*[w_pool[l, g] for g in range(w_pool.shape[1])])
        m_pool = _pool(zs, w_bd.astype(BF16), row(pool_scale[l]))
        m_conv = _conv(zs, conv_w[l], row(conv_b[l]), row(conv_ln_g[l]), row(conv_ln_b[l]))
        bias_full = jnp.repeat(sgu_b[l].T, SGU_WIDTH // SGU_GROUPS, axis=1)
        m_sgu = _sgu(zs, row(sgu_ln_g[l]), row(sgu_ln_b[l]),
                     sgu_w[l].reshape(SGU_GROUPS * SGU_LEN, SGU_LEN), bias_full)
        wqa, wqb, wk, wv = _mla_weights(w_uq[l], w_ukv[l])
        q, k, v = _mla_prep(zs, row(q_norm_g[l]), row(kv_norm_g[l]), wqa, wqb, wk, wv, place, tabs)
        m_mla = _attention(q, k, v, batch)

        xt = _merge(xt, m_pool, m_mla, m_conv, m_sgu, zg, row(b_gate[l]),
                    w_pool_out[l].astype(BF16), w_mla_out[l].astype(BF16),
                    w_conv_out[l].astype(BF16), w_sgu_out[l].astype(BF16), w_o[l].astype(BF16))

        j = l // 2
        if l % 2 == 0:
            xt = _ffn(xt, row(norm_ffn_g[l]), w_ffn_gate[j].astype(BF16),
                      w_ffn_up[j].astype(BF16), w_ffn_down[j].astype(BF16))
        else:
            wr = jnp.pad(w_router[j], ((0, 0), (0, LANES - N_EXPERTS)))
            br = jnp.pad(b_router[j], (0, LANES - N_EXPERTS)).reshape(1, LANES)
            combine = _router(xt, row(norm_ffn_g[l]), wr, br)
            assert l == depth - 1
            xt = _moe(xt, row(norm_ffn_g[l]), combine, w_moe_gate[j].astype(BF16),
                      w_moe_up[j].astype(BF16), w_moe_down[j].astype(BF16), row(final_norm_g))
    return xt.reshape(batch, seq, d)
```

```python
import functools

import jax
import jax.numpy as jnp
from jax import lax
from jax.experimental import pallas as pl
from jax.experimental.pallas import tpu as pltpu

F32 = jnp.float32
BF16 = jnp.bfloat16

D_MODEL = 1024
SEQ = 8192
CHUNK = 64
POOL_WIDTH = 256
POOL_WINDOWS = (2, 4, 8, 16)
MLA_HEADS = 8
Q_LORA = 256
KV_LORA = 128
QK_NOPE = 64
QK_ROPE = 32
V_HEAD = 64
ROPE_THETA = 10000.0
CONV_WIDTH = 256
CONV_K = 31
SGU_WIDTH = 256
SGU_GROUPS = 4
SGU_LEN = 128
N_BRANCH = 4
D_FF = 2816
N_EXPERTS = 8
D_FF_EXPERT = 3584
EPS = 1e-6
NEG = -1e30
LOG2E = 1.4426950408889634

LANES = 128
HEAD_PAD = 128
ZC_COL, ZS_COL, ZP_COL, CQ_COL, CKV_COL, KR_COL = 0, 512, 1024, 1280, 1536, 1664
Z_SMALL = 1792
POOL_HALO = 16
CONV_HALO = 32

TM = 512
TQ = 256
TK = 256
TM_MOE = 1024
TF_MOE = 512
VMEM_LIMIT = 56 * 1024 * 1024


def _params(sem):
    return pltpu.CompilerParams(dimension_semantics=sem, vmem_limit_bytes=VMEM_LIMIT)


def _rms(x, g):
    return x * lax.rsqrt(jnp.mean(x * x, axis=-1, keepdims=True) + EPS) * g


def _layernorm(x, g, b):
    mu = jnp.mean(x, axis=-1, keepdims=True)
    xc = x - mu
    var = jnp.mean(xc * xc, axis=-1, keepdims=True)
    return xc * lax.rsqrt(var + EPS) * g + b


def _sigmoid(x):
    return 1.0 / (1.0 + jnp.exp(-x))


def _dot(a, b):
    return jnp.dot(a, b, preferred_element_type=F32)


def _in_proj_kernel(x_ref, g_ref, ws_ref, wg_ref, zs_ref, zg_ref):
    h = _rms(x_ref[...], g_ref[...]).astype(BF16)
    for n0 in range(0, Z_SMALL, 896):
        zs_ref[:, n0:n0 + 896] = _dot(h, ws_ref[:, n0:n0 + 896])
    for n0 in range(0, N_BRANCH * D_MODEL, 1024):
        zg_ref[:, n0:n0 + 1024] = _dot(h, wg_ref[:, n0:n0 + 1024]).astype(BF16)


def _in_proj(x, g, w_small, w_gate):
    t = x.shape[0]
    return pl.pallas_call(
        _in_proj_kernel,
        grid=(t // TM,),
        in_specs=[
            pl.BlockSpec((TM, D_MODEL), lambda i: (i, 0)),
            pl.BlockSpec((1, D_MODEL), lambda i: (0, 0)),
            pl.BlockSpec((D_MODEL, Z_SMALL), lambda i: (0, 0)),
            pl.BlockSpec((D_MODEL, N_BRANCH * D_MODEL), lambda i: (0, 0)),
        ],
        out_specs=[
            pl.BlockSpec((TM, Z_SMALL), lambda i: (i, 0)),
            pl.BlockSpec((TM, N_BRANCH * D_MODEL), lambda i: (i, 0)),
        ],
        out_shape=[
            jax.ShapeDtypeStruct((t, Z_SMALL), F32),
            jax.ShapeDtypeStruct((t, N_BRANCH * D_MODEL), BF16),
        ],
        compiler_params=_params(("parallel",)),
        name="in_proj",
    )(x, g, w_small, w_gate)


def _pool_kernel(z_ref, halo_ref, w_ref, scale_ref, o_ref, buf_ref):
    i = pl.program_id(0)
    pos0 = (i * TM) % SEQ
    z = z_ref[...]
    buf_ref[0:POOL_HALO, :] = jnp.where(pos0 == 0, 0.0, halo_ref[...])
    buf_ref[POOL_HALO:, :] = z
    lane = lax.broadcasted_iota(jnp.int32, (TM, POOL_WIDTH), 1)
    pos = lax.broadcasted_iota(jnp.int32, (TM, POOL_WIDTH), 0) + pos0
    group = lane // (POOL_WIDTH // len(POOL_WINDOWS))
    acc = z
    total = jnp.zeros_like(z)
    win = jnp.zeros_like(lane)
    prev_w = 1
    for gi, w in enumerate(POOL_WINDOWS):
        for j in range(prev_w, w):
            acc = acc + buf_ref[POOL_HALO - j:POOL_HALO - j + TM, :]
        prev_w = w
        total = jnp.where(group == gi, acc, total)
        win = jnp.where(group == gi, w, win)
    cnt = jnp.minimum(pos + 1, win).astype(F32)
    pooled = total / cnt - z
    y = _dot(pooled.astype(BF16), w_ref[...]) * scale_ref[...]
    o_ref[...] = y.astype(BF16)


def _pool(zs, w_bd, scale):
    t = zs.shape[0]
    cb = ZP_COL // POOL_WIDTH
    r = TM // POOL_HALO
    return pl.pallas_call(
        _pool_kernel,
        grid=(t // TM,),
        in_specs=[
            pl.BlockSpec((TM, POOL_WIDTH), lambda i: (i, cb)),
            pl.BlockSpec((POOL_HALO, POOL_WIDTH), lambda i: (jnp.maximum(i * r - 1, 0), cb)),
            pl.BlockSpec((POOL_WIDTH, POOL_WIDTH), lambda i: (0, 0)),
            pl.BlockSpec((1, POOL_WIDTH), lambda i: (0, 0)),
        ],
        out_specs=pl.BlockSpec((TM, POOL_WIDTH), lambda i: (i, 0)),
        out_shape=jax.ShapeDtypeStruct((t, POOL_WIDTH), BF16),
        scratch_shapes=[pltpu.VMEM((TM + POOL_HALO, POOL_WIDTH), F32)],
        compiler_params=_params(("parallel",)),
        name="pool_mixer",
    )(zs, zs, w_bd, scale)


def _glu(z2):
    return z2[:, :CONV_WIDTH] * _sigmoid(z2[:, CONV_WIDTH:])


def _conv_kernel(z_ref, halo_ref, w_ref, b_ref, lg_ref, lb_ref, o_ref, buf_ref):
    i = pl.program_id(0)
    pos0 = (i * TM) % SEQ
    buf_ref[0:CONV_HALO, :] = jnp.where(pos0 == 0, 0.0, _glu(halo_ref[...]))
    buf_ref[CONV_HALO:, :] = _glu(z_ref[...])
    off = CONV_HALO - (CONV_K - 1)
    y = jnp.zeros((TM, CONV_WIDTH), F32) + b_ref[...]
    for k in range(CONV_K):
        y = y + w_ref[k:k + 1, :] * buf_ref[off + k:off + k + TM, :]
    yn = _layernorm(y, lg_ref[...], lb_ref[...])
    o_ref[...] = (yn * _sigmoid(yn)).astype(BF16)


def _conv(zs, conv_w, conv_b, ln_g, ln_b):
    t = zs.shape[0]
    cb = ZC_COL // (2 * CONV_WIDTH)
    r = TM // CONV_HALO
    vec = pl.BlockSpec((1, CONV_WIDTH), lambda i: (0, 0))
    return pl.pallas_call(
        _conv_kernel,
        grid=(t // TM,),
        in_specs=[
            pl.BlockSpec((TM, 2 * CONV_WIDTH), lambda i: (i, cb)),
            pl.BlockSpec((CONV_HALO, 2 * CONV_WIDTH), lambda i: (jnp.maximum(i * r - 1, 0), cb)),
            pl.BlockSpec((CONV_K, CONV_WIDTH), lambda i: (0, 0)),
            vec, vec, vec,
        ],
        out_specs=pl.BlockSpec((TM, CONV_WIDTH), lambda i: (i, 0)),
        out_shape=jax.ShapeDtypeStruct((t, CONV_WIDTH), BF16),
        scratch_shapes=[pltpu.VMEM((TM + CONV_HALO, CONV_WIDTH), F32)],
        compiler_params=_params(("parallel",)),
        name="conv_mixer",
    )(zs, zs, conv_w, conv_b, ln_g, ln_b)


def _sgu_kernel(z_ref, lg_ref, lb_ref, ws_ref, bias_ref, o_ref):
    z = z_ref[...]
    z = 0.5 * z * (1.0 + lax.erf(z * (2.0 ** -0.5)))
    u = z[:, :SGU_WIDTH]
    v = _layernorm(z[:, SGU_WIDTH:], lg_ref[...], lb_ref[...]).astype(BF16)
    rows = SGU_GROUPS * SGU_LEN
    row = lax.broadcasted_iota(jnp.int32, (rows, SGU_LEN), 0) % SGU_LEN
    col = lax.broadcasted_iota(jnp.int32, (rows, SGU_LEN), 1)
    ws = jnp.where(col <= row, ws_ref[...], 0.0).astype(BF16)
    lane_group = lax.broadcasted_iota(jnp.int32, (SGU_LEN, SGU_WIDTH), 1) // (SGU_WIDTH // SGU_GROUPS)
    for blk in range(TM // SGU_LEN):
        r0 = blk * SGU_LEN
        full = _dot(ws, v[r0:r0 + SGU_LEN, :])
        mixed = full[0:SGU_LEN, :]
        for g in range(1, SGU_GROUPS):
            mixed = jnp.where(lane_group == g, full[g * SGU_LEN:(g + 1) * SGU_LEN, :], mixed)
        mixed = mixed + bias_ref[...]
        o_ref[r0:r0 + SGU_LEN, :] = (u[r0:r0 + SGU_LEN, :] * mixed).astype(BF16)


def _sgu(zs, ln_g, ln_b, ws_stack, bias_full):
    t = zs.shape[0]
    cb = ZS_COL // (2 * SGU_WIDTH)
    vec = pl.BlockSpec((1, SGU_WIDTH), lambda i: (0, 0))
    return pl.pallas_call(
        _sgu_kernel,
        grid=(t // TM,),
        in_specs=[
            pl.BlockSpec((TM, 2 * SGU_WIDTH), lambda i: (i, cb)),
            vec, vec,
            pl.BlockSpec((SGU_GROUPS * SGU_LEN, SGU_LEN), lambda i: (0, 0)),
            pl.BlockSpec((SGU_LEN, SGU_WIDTH), lambda i: (0, 0)),
        ],
        out_specs=pl.BlockSpec((TM, SGU_WIDTH), lambda i: (i, 0)),
        out_shape=jax.ShapeDtypeStruct((t, SGU_WIDTH), BF16),
        compiler_params=_params(("parallel",)),
        name="sgu_mixer",
    )(zs, ln_g, ln_b, ws_stack, bias_full)


def _tile_heads(tab):
    return jnp.concatenate([tab] * MLA_HEADS, axis=1)


def _mla_prep_kernel(cq_ref, ckv_ref, kr_ref, qg_ref, kvg_ref, wqa_ref, wqb_ref, wk_ref, wv_ref,
                     place_ref, cq_tab_ref, sq_tab_ref, ck_tab_ref, sk_tab_ref,
                     qt_ref, k_ref, vt_ref):
    cqn = _rms(cq_ref[...], qg_ref[...]).astype(BF16)
    qa = _dot(cqn, wqa_ref[...])
    qb = _dot(cqn, wqb_ref[...])
    scale = (QK_NOPE + QK_ROPE) ** -0.5 * LOG2E
    q = (qa * _tile_heads(cq_tab_ref[...]) + qb * _tile_heads(sq_tab_ref[...])) * scale
    qt_ref[...] = q.T.astype(BF16)
    kvn = _rms(ckv_ref[...], kvg_ref[...]).astype(BF16)
    vt = _dot(kvn, wv_ref[...]).T
    for c in range(TM // TK):
        vt_ref[c] = vt[:, c * TK:(c + 1) * TK].astype(BF16)
    kr = kr_ref[...]
    kro = kr * ck_tab_ref[...] + pltpu.roll(kr, LANES - QK_ROPE, 1) * sk_tab_ref[...]
    k = _dot(kvn, wk_ref[...]) + _dot(kro.astype(BF16), place_ref[...])
    k_ref[...] = k.astype(BF16)


def _mla_prep(zs, q_g, kv_g, wqa, wqb, wk, wv, place, tabs):
    t = zs.shape[0]
    hw = MLA_HEADS * HEAD_PAD
    hv = MLA_HEADS * V_HEAD
    ns = SEQ // TM
    tab = pl.BlockSpec((TM, LANES), lambda i: (i % ns, 0))
    full = lambda a: pl.BlockSpec(a.shape, lambda i: (0, 0))
    return pl.pallas_call(
        _mla_prep_kernel,
        grid=(t // TM,),
        in_specs=[
            pl.BlockSpec((TM, Q_LORA), lambda i: (i, CQ_COL // Q_LORA)),
            pl.BlockSpec((TM, KV_LORA), lambda i: (i, CKV_COL // KV_LORA)),
            pl.BlockSpec((TM, LANES), lambda i: (i, KR_COL // LANES)),
            full(q_g), full(kv_g), full(wqa), full(wqb), full(wk), full(wv), full(place),
            tab, tab, tab, tab,
        ],
        out_specs=[
            pl.BlockSpec((hw, TM), lambda i: (0, i)),
            pl.BlockSpec((TM, hw), lambda i: (i, 0)),
            pl.BlockSpec((TM // TK, hv, TK), lambda i: (i, 0, 0)),
        ],
        out_shape=[
            jax.ShapeDtypeStruct((hw, t), BF16),
            jax.ShapeDtypeStruct((t, hw), BF16),
            jax.ShapeDtypeStruct((t // TK, hv, TK), BF16),
        ],
        compiler_params=_params(("parallel",)),
        name="mla_prep",
    )(zs, zs, zs, q_g, kv_g, wqa, wqb, wk, wv, place, *tabs)


def _attn_kernel(qt_ref, k_ref, vt_ref, o_ref, m_ref, l_ref, acc_ref, st_ref, p_ref):
    qi = pl.program_id(1)
    m_ref[...] = jnp.full(m_ref.shape, NEG, F32)
    l_ref[...] = jnp.zeros(l_ref.shape, F32)
    acc_ref[...] = jnp.zeros(acc_ref.shape, F32)
    key_chunk = lax.broadcasted_iota(jnp.int32, (TK, TQ), 0) // CHUNK
    qry_chunk = lax.broadcasted_iota(jnp.int32, (TK, TQ), 1) // CHUNK
    visible = key_chunk <= qry_chunk

    def step(j, masked):
        ks = pl.multiple_of(j * TK, TK)
        for h in range(MLA_HEADS):
            hs = slice(h * HEAD_PAD, (h + 1) * HEAD_PAD)
            st_ref[h] = _dot(k_ref[pl.ds(ks, TK), hs], qt_ref[hs, :])
        alphas = []
        for h in range(MLA_HEADS):
            st = st_ref[h]
            if masked:
                st = jnp.where(visible, st, NEG)
            m_prev = m_ref[h]
            m_new = jnp.maximum(m_prev, jnp.max(st, axis=0, keepdims=True))
            alpha = jnp.exp2(m_prev - m_new)
            p = jnp.exp2(st - m_new)
            l_ref[h] = alpha * l_ref[h] + jnp.sum(p, axis=0, keepdims=True)
            m_ref[h] = m_new
            p_ref[h] = p.astype(BF16)
            alphas.append(alpha)
        for h in range(MLA_HEADS):
            rows = slice(h * V_HEAD, (h + 1) * V_HEAD)
            acc_ref[rows, :] = alphas[h] * acc_ref[rows, :] + _dot(vt_ref[j, rows, :], p_ref[h])

    def body(j, carry):
        step(j, False)
        return carry

    lax.fori_loop(0, qi, body, 0)
    step(qi, True)
    outs = [acc_ref[h * V_HEAD:(h + 1) * V_HEAD, :] / l_ref[h] for h in range(MLA_HEADS)]
    o_ref[...] = jnp.concatenate(outs, axis=0).T.astype(BF16)


def _attention(qt, k, vt, batch):
    t = k.shape[0]
    nq = SEQ // TQ
    nk = SEQ // TK
    hw = MLA_HEADS * HEAD_PAD
    hv = MLA_HEADS * V_HEAD
    once = pl.Buffered(1)
    return pl.pallas_call(
        _attn_kernel,
        grid=(batch, nq),
        in_specs=[
            pl.BlockSpec((hw, TQ), lambda b, i: (0, b * nq + i)),
            pl.BlockSpec((SEQ, hw), lambda b, i: (b, 0), pipeline_mode=once),
            pl.BlockSpec((nk, hv, TK), lambda b, i: (b, 0, 0), pipeline_mode=once),
        ],
        out_specs=pl.BlockSpec((TQ, hv), lambda b, i: (b * nq + i, 0)),
        out_shape=jax.ShapeDtypeStruct((t, hv), BF16),
        scratch_shapes=[
            pltpu.VMEM((MLA_HEADS, 1, TQ), F32),
            pltpu.VMEM((MLA_HEADS, 1, TQ), F32),
            pltpu.VMEM((hv, TQ), F32),
            pltpu.VMEM((MLA_HEADS, TK, TQ), F32),
            pltpu.VMEM((MLA_HEADS, TK, TQ), BF16),
        ],
        compiler_params=_params(("parallel", "arbitrary")),
        name="mla_attention",
    )(qt, k, vt)


def _merge_kernel(x_ref, mp_ref, ma_ref, mc_ref, ms_ref, zg_ref, bg_ref,
                  wp_ref, wa_ref, wc_ref, ws_ref, wo_ref, o_ref, mg_ref):
    branches = ((mp_ref, wp_ref), (ma_ref, wa_ref), (mc_ref, wc_ref), (ms_ref, ws_ref))
    half = D_MODEL // 2
    for n0 in range(0, D_MODEL, half):
        merged = jnp.zeros((TM, half), F32)
        for b, (m_ref, w_ref) in enumerate(branches):
            c0 = b * D_MODEL + n0
            gate = _sigmoid(zg_ref[:, c0:c0 + half].astype(F32) + bg_ref[:, c0:c0 + half])
            merged = merged + gate * _dot(m_ref[...], w_ref[:, n0:n0 + half])
        mg_ref[:, n0:n0 + half] = merged.astype(BF16)
    o_ref[...] = x_ref[...] + _dot(mg_ref[...], wo_ref[...])


def _merge(x, mp, ma, mc, ms, zg, b_gate, wp, wa, wc, ws, wo):
    t = x.shape[0]
    row = lambda w: pl.BlockSpec((TM, w), lambda i: (i, 0))
    full = lambda a: pl.BlockSpec(a.shape, lambda i: (0, 0))
    return pl.pallas_call(
        _merge_kernel,
        grid=(t // TM,),
        in_specs=[row(D_MODEL), row(POOL_WIDTH), row(MLA_HEADS * V_HEAD), row(CONV_WIDTH), row(SGU_WIDTH),
                  row(N_BRANCH * D_MODEL), full(b_gate), full(wp), full(wa), full(wc), full(ws), full(wo)],
        out_specs=row(D_MODEL),
        out_shape=jax.ShapeDtypeStruct((t, D_MODEL), F32),
        scratch_shapes=[pltpu.VMEM((TM, D_MODEL), BF16)],
        compiler_params=_params(("parallel",)),
        name="merge_out",
    )(x, mp, ma, mc, ms, zg, b_gate, wp, wa, wc, ws, wo)


def _ffn_kernel(x_ref, g_ref, wg_ref, wu_ref, wd_ref, o_ref):
    x = x_ref[...]
    h = _rms(x, g_ref[...]).astype(BF16)
    half = D_FF // 2
    acc = x
    for f0 in range(0, D_FF, half):
        gate = _dot(h, wg_ref[:, f0:f0 + half])
        up = _dot(h, wu_ref[:, f0:f0 + half])
        act = (gate * _sigmoid(gate) * up).astype(BF16)
        acc = acc + _dot(act, wd_ref[f0:f0 + half, :])
    o_ref[...] = acc


def _ffn(x, g, wg, wu, wd):
    t = x.shape[0]
    full = lambda a: pl.BlockSpec(a.shape, lambda i: (0, 0))
    return pl.pallas_call(
        _ffn_kernel,
        grid=(t // TM,),
        in_specs=[pl.BlockSpec((TM, D_MODEL), lambda i: (i, 0)), full(g), full(wg), full(wu), full(wd)],
        out_specs=pl.BlockSpec((TM, D_MODEL), lambda i: (i, 0)),
        out_shape=jax.ShapeDtypeStruct((t, D_MODEL), F32),
        compiler_params=_params(("parallel",)),
        name="ffn_dense",
    )(x, g, wg, wu, wd)


def _router_kernel(x_ref, g_ref, wr_ref, br_ref, c_ref):
    h = _rms(x_ref[...], g_ref[...])
    logits = jnp.dot(h, wr_ref[...], preferred_element_type=F32,
                     precision=lax.Precision.HIGHEST) + br_ref[...]
    lane = lax.broadcasted_iota(jnp.int32, logits.shape, 1).astype(F32)
    logits = jnp.where(lane < N_EXPERTS, logits, NEG)
    v1 = jnp.max(logits, axis=1, keepdims=True)
    i1 = jnp.min(jnp.where(logits == v1, lane, float(LANES)), axis=1, keepdims=True)
    rest = jnp.where(lane == i1, NEG, logits)
    v2 = jnp.max(rest, axis=1, keepdims=True)
    i2 = jnp.min(jnp.where(rest == v2, lane, float(LANES)), axis=1, keepdims=True)
    e = jnp.exp(v2 - v1)
    w1 = 1.0 / (1.0 + e)
    w2 = e / (1.0 + e)
    c_ref[...] = jnp.where(lane == i1, w1, 0.0) + jnp.where(lane == i2, w2, 0.0)


def _router(x, g, wr, br):
    t = x.shape[0]
    full = lambda a: pl.BlockSpec(a.shape, lambda i: (0, 0))
    return pl.pallas_call(
        _router_kernel,
        grid=(t // TM,),
        in_specs=[pl.BlockSpec((TM, D_MODEL), lambda i: (i, 0)), full(g), full(wr), full(br)],
        out_specs=pl.BlockSpec((TM, LANES), lambda i: (i, 0)),
        out_shape=jax.ShapeDtypeStruct((t, LANES), F32),
        compiler_params=_params(("parallel",)),
        name="moe_router",
    )(x, g, wr, br)


def _moe_kernel(x_ref, g_ref, c_ref, wg_ref, wu_ref, wd_ref, fg_ref, o_ref, h_ref, acc_ref):
    e = pl.program_id(1)
    f = pl.program_id(2)

    @pl.when((e == 0) & (f == 0))
    def _():
        h_ref[...] = _rms(x_ref[...], g_ref[...]).astype(BF16)
        acc_ref[...] = jnp.zeros(acc_ref.shape, F32)

    h = h_ref[...]
    gate = _dot(h, wg_ref[...])
    up = _dot(h, wu_ref[...])
    lane = lax.broadcasted_iota(jnp.int32, (TM_MOE, LANES), 1)
    ce = jnp.sum(jnp.where(lane == e, c_ref[...], 0.0), axis=1, keepdims=True)
    act = (gate * _sigmoid(gate) * up * ce).astype(BF16)
    acc_ref[...] += _dot(act, wd_ref[...])

    @pl.when((e == N_EXPERTS - 1) & (f == pl.num_programs(2) - 1))
    def _():
        o_ref[...] = _rms(x_ref[...] + acc_ref[...], fg_ref[...])


def _moe(x, g, combine, wg, wu, wd, final_g):
    t = x.shape[0]
    nf = D_FF_EXPERT // TF_MOE
    vec = pl.BlockSpec((1, D_MODEL), lambda i, e, f: (0, 0))
    return pl.pallas_call(
        _moe_kernel,
        grid=(t // TM_MOE, N_EXPERTS, nf),
        in_specs=[
            pl.BlockSpec((TM_MOE, D_MODEL), lambda i, e, f: (i, 0)),
            vec,
            pl.BlockSpec((TM_MOE, LANES), lambda i, e, f: (i, 0)),
            pl.BlockSpec((None, D_MODEL, TF_MOE), lambda i, e, f: (e, 0, f)),
            pl.BlockSpec((None, D_MODEL, TF_MOE), lambda i, e, f: (e, 0, f)),
            pl.BlockSpec((None, TF_MOE, D_MODEL), lambda i, e, f: (e, f, 0)),
            vec,
        ],
        out_specs=pl.BlockSpec((TM_MOE, D_MODEL), lambda i, e, f: (i, 0)),
        out_shape=jax.ShapeDtypeStruct((t, D_MODEL), F32),
        scratch_shapes=[pltpu.VMEM((TM_MOE, D_MODEL), BF16), pltpu.VMEM((TM_MOE, D_MODEL), F32)],
        compiler_params=_params(("parallel", "arbitrary", "arbitrary")),
        name="moe_experts",
    )(x, g, combine, wg, wu, wd, final_g)


def _rotate_half_cols(w):
    half = w.shape[-1] // 2
    return jnp.concatenate([-w[..., half:], w[..., :half]], axis=-1)


def _rope_tables():
    pos = jnp.arange(SEQ, dtype=F32)
    inv = ROPE_THETA ** (-jnp.arange(0, QK_ROPE, 2, dtype=F32) / QK_ROPE)
    ang = pos[:, None] * inv[None, :]
    cos, sin = jnp.cos(ang), jnp.sin(ang)
    one = jnp.ones((SEQ, QK_NOPE), F32)
    z_nope = jnp.zeros((SEQ, QK_NOPE), F32)
    z_pad = jnp.zeros((SEQ, HEAD_PAD - QK_NOPE - QK_ROPE), F32)
    z_rest = jnp.zeros((SEQ, LANES - QK_ROPE), F32)
    cq_tab = jnp.concatenate([one, cos, cos, z_pad], axis=1)
    sq_tab = jnp.concatenate([z_nope, sin, sin, z_pad], axis=1)
    ck_tab = jnp.concatenate([cos, cos, z_rest], axis=1)
    sk_tab = jnp.concatenate([sin, sin, z_rest], axis=1)
    return cq_tab, sq_tab, ck_tab, sk_tab


def _rope_placement():
    r = jnp.arange(QK_ROPE)
    place = jnp.zeros((LANES, MLA_HEADS * HEAD_PAD), F32)
    for h in range(MLA_HEADS):
        place = place.at[r, h * HEAD_PAD + QK_NOPE + r].set(1.0)
    return place.astype(BF16)


def _split_in_proj(w):
    zp, cq, ckv, kr, zc, zs, zg = jnp.split(
        w, (256, 512, 640, 672, 1184, 1696), axis=1)
    pad = jnp.zeros((D_MODEL, LANES - 2 * QK_ROPE), w.dtype)
    small = jnp.concatenate([zc, zs, zp, cq, ckv, kr, _rotate_half_cols(kr), pad], axis=1)
    return small.astype(BF16), zg.astype(BF16)


def _mla_weights(w_uq, w_ukv):
    wq = w_uq.reshape(Q_LORA, MLA_HEADS, QK_NOPE + QK_ROPE)
    nope, rope = wq[..., :QK_NOPE], wq[..., QK_NOPE:]
    zq = jnp.zeros((Q_LORA, MLA_HEADS, HEAD_PAD - QK_NOPE - QK_ROPE), w_uq.dtype)
    wqa = jnp.concatenate([nope, rope, zq], axis=-1).reshape(Q_LORA, MLA_HEADS * HEAD_PAD)
    wqb = jnp.concatenate([jnp.zeros_like(nope), _rotate_half_cols(rope), zq], axis=-1)
    wqb = wqb.reshape(Q_LORA, MLA_HEADS * HEAD_PAD)
    wkv = w_ukv.reshape(KV_LORA, MLA_HEADS, QK_NOPE + V_HEAD)
    k_nope, v = wkv[..., :QK_NOPE], wkv[..., QK_NOPE:]
    zk = jnp.zeros((KV_LORA, MLA_HEADS, HEAD_PAD - QK_NOPE), w_ukv.dtype)
    wk = jnp.concatenate([k_nope, zk], axis=-1).reshape(KV_LORA, MLA_HEADS * HEAD_PAD)
    wv = v.reshape(KV_LORA, MLA_HEADS * V_HEAD)
    return wqa.astype(BF16), wqb.astype(BF16), wk.astype(BF16), wv.astype(BF16)


def kernel(x, norm_mix_g, w_in, b_gate, w_pool, pool_scale, w_pool_out, q_norm_g, w_uq, kv_norm_g, w_ukv, w_mla_out, conv_w, conv_b, conv_ln_g, conv_ln_b, w_conv_out, sgu_ln_g, sgu_ln_b, sgu_w, sgu_b, w_sgu_out, w_o, norm_ffn_g, w_ffn_gate, w_ffn_up, w_ffn_down, w_router, b_router, w_moe_gate, w_moe_up, w_moe_down, final_norm_g):
    batch, seq, d = x.shape
    assert (seq, d) == (SEQ, D_MODEL)
    depth = w_in.shape[0]
    xt = x.reshape(batch * seq, d)
    tabs = _rope_tables()
    place = _rope_placement()
    row = lambda a: a.reshape(1, -1)

    for l in range(depth):
        w_small, w_gate = _split_in_proj(w_in[l])
        zs, zg = _in_proj(xt, row(norm_mix_g[l]), w_small, w_gate)

        w_bd = jax.scipy.linalg.block_diag(*[w_pool[l, g] for g in range(w_pool.shape[1])])
        m_pool = _pool(zs, w_bd.astype(BF16), row(pool_scale[l]))
        m_conv = _conv(zs, conv_w[l], row(conv_b[l]), row(conv_ln_g[l]), row(conv_ln_b[l]))
        bias_full = jnp.repeat(sgu_b[l].T, SGU_WIDTH // SGU_GROUPS, axis=1)
        m_sgu = _sgu(zs, row(sgu_ln_g[l]), row(sgu_ln_b[l]),
                     sgu_w[l].reshape(SGU_GROUPS * SGU_LEN, SGU_LEN), bias_full)
        wqa, wqb, wk, wv = _mla_weights(w_uq[l], w_ukv[l])
        q, k, v = _mla_prep(zs, row(q_norm_g[l]), row(kv_norm_g[l]), wqa, wqb, wk, wv, place, tabs)
        m_mla = _attention(q, k, v, batch)

        xt = _merge(xt, m_pool, m_mla, m_conv, m_sgu, zg, row(b_gate[l]),
                    w_pool_out[l].astype(BF16), w_mla_out[l].astype(BF16),
                    w_conv_out[l].astype(BF16), w_sgu_out[l].astype(BF16), w_o[l].astype(BF16))

        j = l // 2
        if l % 2 == 0:
            xt = _ffn(xt, row(norm_ffn_g[l]), w_ffn_gate[j].astype(BF16),
                      w_ffn_up[j].astype(BF16), w_ffn_down[j].astype(BF16))
        else:
            wr = jnp.pad(w_router[j], ((0, 0), (0, LANES - N_EXPERTS)))
            br = jnp.pad(b_router[j], (0, LANES - N_EXPERTS)).reshape(1, LANES)
            combine = _router(xt, row(norm_ffn_g[l]), wr, br)
            assert l == depth - 1
            xt = _moe(xt, row(norm_ffn_g[l]), combine, w_moe_gate[j].astype(BF16),
                      w_moe_up[j].astype(BF16), w_moe_down[j].astype(BF16), row(final_norm_g))
    return xt.reshape(batch, seq, d)
```

```python
import functools

import jax
import jax.numpy as jnp
from jax import lax
from jax.experimental import pallas as pl
from jax.experimental.pallas import tpu as pltpu

F32 = jnp.float32
BF16 = jnp.bfloat16

D_MODEL = 1024
SEQ = 8192
CHUNK = 64
POOL_WIDTH = 256
POOL_WINDOWS = (2, 4, 8, 16)
MLA_HEADS = 8
Q_LORA = 256
KV_LORA = 128
QK_NOPE = 64
QK_ROPE = 32
V_HEAD = 64
ROPE_THETA = 10000.0
CONV_WIDTH = 256
CONV_K = 31
SGU_WIDTH = 256
SGU_GROUPS = 4
SGU_LEN = 128
N_BRANCH = 4
D_FF = 2816
N_EXPERTS = 8
D_FF_EXPERT = 3584
EPS = 1e-6
NEG = -1e30
LOG2E = 1.4426950408889634

LANES = 128
HEAD_PAD = 128
ZC_COL, ZS_COL, ZP_COL, CQ_COL, CKV_COL, KR_COL = 0, 512, 1024, 1280, 1536, 1664
Z_SMALL = 1792
POOL_HALO = 16
CONV_HALO = 32

TM = 512
TQ = 256
TK = 256
TR = 512
BLK = 256
TMG = 512
TF_G = 1792
INFO_W = 8
VMEM_LIMIT = 56 * 1024 * 1024


def _params(sem):
    return pltpu.CompilerParams(dimension_semantics=sem, vmem_limit_bytes=VMEM_LIMIT)


def _rms(x, g):
    return x * lax.rsqrt(jnp.mean(x * x, axis=-1, keepdims=True) + EPS) * g


def _layernorm(x, g, b):
    mu = jnp.mean(x, axis=-1, keepdims=True)
    xc = x - mu
    var = jnp.mean(xc * xc, axis=-1, keepdims=True)
    return xc * lax.rsqrt(var + EPS) * g + b


def _sigmoid(x):
    return 1.0 / (1.0 + jnp.exp(-x))


def _dot(a, b):
    return jnp.dot(a, b, preferred_element_type=F32)


def _in_proj_kernel(x_ref, g_ref, ws_ref, wg_ref, zs_ref, zg_ref):
    h = _rms(x_ref[...], g_ref[...]).astype(BF16)
    for n0 in range(0, Z_SMALL, 896):
        zs_ref[:, n0:n0 + 896] = _dot(h, ws_ref[:, n0:n0 + 896])
    for n0 in range(0, N_BRANCH * D_MODEL, 1024):
        zg_ref[:, n0:n0 + 1024] = _dot(h, wg_ref[:, n0:n0 + 1024]).astype(BF16)


def _in_proj(x, g, w_small, w_gate):
    t = x.shape[0]
    return pl.pallas_call(
        _in_proj_kernel,
        grid=(t // TM,),
        in_specs=[
            pl.BlockSpec((TM, D_MODEL), lambda i: (i, 0)),
            pl.BlockSpec((1, D_MODEL), lambda i: (0, 0)),
            pl.BlockSpec((D_MODEL, Z_SMALL), lambda i: (0, 0)),
            pl.BlockSpec((D_MODEL, N_BRANCH * D_MODEL), lambda i: (0, 0)),
        ],
        out_specs=[
            pl.BlockSpec((TM, Z_SMALL), lambda i: (i, 0)),
            pl.BlockSpec((TM, N_BRANCH * D_MODEL), lambda i: (i, 0)),
        ],
        out_shape=[
            jax.ShapeDtypeStruct((t, Z_SMALL), F32),
            jax.ShapeDtypeStruct((t, N_BRANCH * D_MODEL), BF16),
        ],
        compiler_params=_params(("parallel",)),
        name="in_proj",
    )(x, g, w_small, w_gate)


def _pool_kernel(z_ref, halo_ref, w_ref, scale_ref, o_ref, buf_ref):
    i = pl.program_id(0)
    pos0 = (i * TM) % SEQ
    z = z_ref[...]
    buf_ref[0:POOL_HALO, :] = jnp.where(pos0 == 0, 0.0, halo_ref[...])
    buf_ref[POOL_HALO:, :] = z
    lane = lax.broadcasted_iota(jnp.int32, (TM, POOL_WIDTH), 1)
    pos = lax.broadcasted_iota(jnp.int32, (TM, POOL_WIDTH), 0) + pos0
    group = lane // (POOL_WIDTH // len(POOL_WINDOWS))
    acc = z
    total = jnp.zeros_like(z)
    win = jnp.zeros_like(lane)
    prev_w = 1
    for gi, w in enumerate(POOL_WINDOWS):
        for j in range(prev_w, w):
            acc = acc + buf_ref[POOL_HALO - j:POOL_HALO - j + TM, :]
        prev_w = w
        total = jnp.where(group == gi, acc, total)
        win = jnp.where(group == gi, w, win)
    cnt = jnp.minimum(pos + 1, win).astype(F32)
    pooled = total / cnt - z
    y = _dot(pooled.astype(BF16), w_ref[...]) * scale_ref[...]
    o_ref[...] = y.astype(BF16)


def _pool(zs, w_bd, scale):
    t = zs.shape[0]
    cb = ZP_COL // POOL_WIDTH
    r = TM // POOL_HALO
    return pl.pallas_call(
        _pool_kernel,
        grid=(t // TM,),
        in_specs=[
            pl.BlockSpec((TM, POOL_WIDTH), lambda i: (i, cb)),
            pl.BlockSpec((POOL_HALO, POOL_WIDTH), lambda i: (jnp.maximum(i * r - 1, 0), cb)),
            pl.BlockSpec((POOL_WIDTH, POOL_WIDTH), lambda i: (0, 0)),
            pl.BlockSpec((1, POOL_WIDTH), lambda i: (0, 0)),
        ],
        out_specs=pl.BlockSpec((TM, POOL_WIDTH), lambda i: (i, 0)),
        out_shape=jax.ShapeDtypeStruct((t, POOL_WIDTH), BF16),
        scratch_shapes=[pltpu.VMEM((TM + POOL_HALO, POOL_WIDTH), F32)],
        compiler_params=_params(("parallel",)),
        name="pool_mixer",
    )(zs, zs, w_bd, scale)


def _glu(z2):
    return z2[:, :CONV_WIDTH] * _sigmoid(z2[:, CONV_WIDTH:])


def _conv_kernel(z_ref, halo_ref, w_ref, b_ref, lg_ref, lb_ref, o_ref, buf_ref):
    i = pl.program_id(0)
    pos0 = (i * TM) % SEQ
    buf_ref[0:CONV_HALO, :] = jnp.where(pos0 == 0, 0.0, _glu(halo_ref[...]))
    buf_ref[CONV_HALO:, :] = _glu(z_ref[...])
    off = CONV_HALO - (CONV_K - 1)
    y = jnp.zeros((TM, CONV_WIDTH), F32) + b_ref[...]
    for k in range(CONV_K):
        y = y + w_ref[k:k + 1, :] * buf_ref[off + k:off + k + TM, :]
    yn = _layernorm(y, lg_ref[...], lb_ref[...])
    o_ref[...] = (yn * _sigmoid(yn)).astype(BF16)


def _conv(zs, conv_w, conv_b, ln_g, ln_b):
    t = zs.shape[0]
    cb = ZC_COL // (2 * CONV_WIDTH)
    r = TM // CONV_HALO
    vec = pl.BlockSpec((1, CONV_WIDTH), lambda i: (0, 0))
    return pl.pallas_call(
        _conv_kernel,
        grid=(t // TM,),
        in_specs=[
            pl.BlockSpec((TM, 2 * CONV_WIDTH), lambda i: (i, cb)),
            pl.BlockSpec((CONV_HALO, 2 * CONV_WIDTH), lambda i: (jnp.maximum(i * r - 1, 0), cb)),
            pl.BlockSpec((CONV_K, CONV_WIDTH), lambda i: (0, 0)),
            vec, vec, vec,
        ],
        out_specs=pl.BlockSpec((TM, CONV_WIDTH), lambda i: (i, 0)),
        out_shape=jax.ShapeDtypeStruct((t, CONV_WIDTH), BF16),
        scratch_shapes=[pltpu.VMEM((TM + CONV_HALO, CONV_WIDTH), F32)],
        compiler_params=_params(("parallel",)),
        name="conv_mixer",
    )(zs, zs, conv_w, conv_b, ln_g, ln_b)


def _sgu_kernel(z_ref, lg_ref, lb_ref, ws_ref, bias_ref, o_ref):
    z = z_ref[...]
    z = 0.5 * z * (1.0 + lax.erf(z * (2.0 ** -0.5)))
    u = z[:, :SGU_WIDTH]
    v = _layernorm(z[:, SGU_WIDTH:], lg_ref[...], lb_ref[...]).astype(BF16)
    rows = SGU_GROUPS * SGU_LEN
    row = lax.broadcasted_iota(jnp.int32, (rows, SGU_LEN), 0) % SGU_LEN
    col = lax.broadcasted_iota(jnp.int32, (rows, SGU_LEN), 1)
    ws = jnp.where(col <= row, ws_ref[...], 0.0).astype(BF16)
    lane_group = lax.broadcasted_iota(jnp.int32, (SGU_LEN, SGU_WIDTH), 1) // (SGU_WIDTH // SGU_GROUPS)
    for blk in range(TM // SGU_LEN):
        r0 = blk * SGU_LEN
        full = _dot(ws, v[r0:r0 + SGU_LEN, :])
        mixed = full[0:SGU_LEN, :]
        for g in range(1, SGU_GROUPS):
            mixed = jnp.where(lane_group == g, full[g * SGU_LEN:(g + 1) * SGU_LEN, :], mixed)
        mixed = mixed + bias_ref[...]
        o_ref[r0:r0 + SGU_LEN, :] = (u[r0:r0 + SGU_LEN, :] * mixed).astype(BF16)


def _sgu(zs, ln_g, ln_b, ws_stack, bias_full):
    t = zs.shape[0]
    cb = ZS_COL // (2 * SGU_WIDTH)
    vec = pl.BlockSpec((1, SGU_WIDTH), lambda i: (0, 0))
    return pl.pallas_call(
        _sgu_kernel,
        grid=(t // TM,),
        in_specs=[
            pl.BlockSpec((TM, 2 * SGU_WIDTH), lambda i: (i, cb)),
            vec, vec,
            pl.BlockSpec((SGU_GROUPS * SGU_LEN, SGU_LEN), lambda i: (0, 0)),
            pl.BlockSpec((SGU_LEN, SGU_WIDTH), lambda i: (0, 0)),
        ],
        out_specs=pl.BlockSpec((TM, SGU_WIDTH), lambda i: (i, 0)),
        out_shape=jax.ShapeDtypeStruct((t, SGU_WIDTH), BF16),
        compiler_params=_params(("parallel",)),
        name="sgu_mixer",
    )(zs, ln_g, ln_b, ws_stack, bias_full)


def _tile_heads(tab):
    return jnp.concatenate([tab] * MLA_HEADS, axis=1)


def _mla_prep_kernel(cq_ref, ckv_ref, kr_ref, qg_ref, kvg_ref, wqa_ref, wqb_ref, wk_ref, wv_ref,
                     place_ref, cq_tab_ref, sq_tab_ref, ck_tab_ref, sk_tab_ref,
                     qt_ref, k_ref, vt_ref):
    cqn = _rms(cq_ref[...], qg_ref[...]).astype(BF16)
    qa = _dot(cqn, wqa_ref[...])
    qb = _dot(cqn, wqb_ref[...])
    scale = (QK_NOPE + QK_ROPE) ** -0.5 * LOG2E
    q = (qa * _tile_heads(cq_tab_ref[...]) + qb * _tile_heads(sq_tab_ref[...])) * scale
    qt_ref[...] = q.T.astype(BF16)
    kvn = _rms(ckv_ref[...], kvg_ref[...]).astype(BF16)
    vt = _dot(kvn, wv_ref[...]).T
    for c in range(TM // TK):
        vt_ref[c] = vt[:, c * TK:(c + 1) * TK].astype(BF16)
    kr = kr_ref[...]
    kro = kr * ck_tab_ref[...] + pltpu.roll(kr, LANES - QK_ROPE, 1) * sk_tab_ref[...]
    k = _dot(kvn, wk_ref[...]) + _dot(kro.astype(BF16), place_ref[...])
    k_ref[...] = k.astype(BF16)


def _mla_prep(zs, q_g, kv_g, wqa, wqb, wk, wv, place, tabs):
    t = zs.shape[0]
    hw = MLA_HEADS * HEAD_PAD
    hv = MLA_HEADS * V_HEAD
    ns = SEQ // TM
    tab = pl.BlockSpec((TM, LANES), lambda i: (i % ns, 0))
    full = lambda a: pl.BlockSpec(a.shape, lambda i: (0, 0))
    return pl.pallas_call(
        _mla_prep_kernel,
        grid=(t // TM,),
        in_specs=[
            pl.BlockSpec((TM, Q_LORA), lambda i: (i, CQ_COL // Q_LORA)),
            pl.BlockSpec((TM, KV_LORA), lambda i: (i, CKV_COL // KV_LORA)),
            pl.BlockSpec((TM, LANES), lambda i: (i, KR_COL // LANES)),
            full(q_g), full(kv_g), full(wqa), full(wqb), full(wk), full(wv), full(place),
            tab, tab, tab, tab,
        ],
        out_specs=[
            pl.BlockSpec((hw, TM), lambda i: (0, i)),
            pl.BlockSpec((TM, hw), lambda i: (i, 0)),
            pl.BlockSpec((TM // TK, hv, TK), lambda i: (i, 0, 0)),
        ],
        out_shape=[
            jax.ShapeDtypeStruct((hw, t), BF16),
            jax.ShapeDtypeStruct((t, hw), BF16),
            jax.ShapeDtypeStruct((t // TK, hv, TK), BF16),
        ],
        compiler_params=_params(("parallel",)),
        name="mla_prep",
    )(zs, zs, zs, q_g, kv_g, wqa, wqb, wk, wv, place, *tabs)


def _attn_kernel(qt_ref, k_ref, vt_ref, o_ref, m_ref, l_ref, acc_ref, st_ref, p_ref):
    qi = pl.program_id(1)
    m_ref[...] = jnp.full(m_ref.shape, NEG, F32)
    l_ref[...] = jnp.zeros(l_ref.shape, F32)
    acc_ref[...] = jnp.zeros(acc_ref.shape, F32)
    key_chunk = lax.broadcasted_iota(jnp.int32, (TK, TQ), 0) // CHUNK
    qry_chunk = lax.broadcasted_iota(jnp.int32, (TK, TQ), 1) // CHUNK
    visible = key_chunk <= qry_chunk

    def step(j, masked):
        ks = pl.multiple_of(j * TK, TK)
        for h in range(MLA_HEADS):
            hs = slice(h * HEAD_PAD, (h + 1) * HEAD_PAD)
            st_ref[h] = _dot(k_ref[pl.ds(ks, TK), hs], qt_ref[hs, :])
        alphas = []
        for h in range(MLA_HEADS):
            st = st_ref[h]
            if masked:
                st = jnp.where(visible, st, NEG)
            m_prev = m_ref[h]
            m_new = jnp.maximum(m_prev, jnp.max(st, axis=0, keepdims=True))
            alpha = jnp.exp2(m_prev - m_new)
            p = jnp.exp2(st - m_new)
            l_ref[h] = alpha * l_ref[h] + jnp.sum(p, axis=0, keepdims=True)
            m_ref[h] = m_new
            p_ref[h] = p.astype(BF16)
            alphas.append(alpha)
        for h in range(MLA_HEADS):
            rows = slice(h * V_HEAD, (h + 1) * V_HEAD)
            acc_ref[rows, :] = alphas[h] * acc_ref[rows, :] + _dot(vt_ref[j, rows, :], p_ref[h])

    def body(j, carry):
        step(j, False)
        return carry

    lax.fori_loop(0, qi, body, 0)
    step(qi, True)
    outs = [acc_ref[h * V_HEAD:(h + 1) * V_HEAD, :] / l_ref[h] for h in range(MLA_HEADS)]
    o_ref[...] = jnp.concatenate(outs, axis=0).T.astype(BF16)


def _attention(qt, k, vt, batch):
    t = k.shape[0]
    nq = SEQ // TQ
    nk = SEQ // TK
    hw = MLA_HEADS * HEAD_PAD
    hv = MLA_HEADS * V_HEAD
    once = pl.Buffered(1)
    return pl.pallas_call(
        _attn_kernel,
        grid=(batch, nq),
        in_specs=[
            pl.BlockSpec((hw, TQ), lambda b, i: (0, b * nq + i)),
            pl.BlockSpec((SEQ, hw), lambda b, i: (b, 0), pipeline_mode=once),
            pl.BlockSpec((nk, hv, TK), lambda b, i: (b, 0, 0), pipeline_mode=once),
        ],
        out_specs=pl.BlockSpec((TQ, hv), lambda b, i: (b * nq + i, 0)),
        out_shape=jax.ShapeDtypeStruct((t, hv), BF16),
        scratch_shapes=[
            pltpu.VMEM((MLA_HEADS, 1, TQ), F32),
            pltpu.VMEM((MLA_HEADS, 1, TQ), F32),
            pltpu.VMEM((hv, TQ), F32),
            pltpu.VMEM((MLA_HEADS, TK, TQ), F32),
            pltpu.VMEM((MLA_HEADS, TK, TQ), BF16),
        ],
        compiler_params=_params(("parallel", "arbitrary")),
        name="mla_attention",
    )(qt, k, vt)


def _merge_kernel(x_ref, mp_ref, ma_ref, mc_ref, ms_ref, zg_ref, bg_ref,
                  wp_ref, wa_ref, wc_ref, ws_ref, wo_ref, o_ref, mg_ref):
    branches = ((mp_ref, wp_ref), (ma_ref, wa_ref), (mc_ref, wc_ref), (ms_ref, ws_ref))
    half = D_MODEL // 2
    for n0 in range(0, D_MODEL, half):
        merged = jnp.zeros((TM, half), F32)
        for b, (m_ref, w_ref) in enumerate(branches):
            c0 = b * D_MODEL + n0
            gate = _sigmoid(zg_ref[:, c0:c0 + half].astype(F32) + bg_ref[:, c0:c0 + half])
            merged = merged + gate * _dot(m_ref[...], w_ref[:, n0:n0 + half])
        mg_ref[:, n0:n0 + half] = merged.astype(BF16)
    o_ref[...] = x_ref[...] + _dot(mg_ref[...], wo_ref[...])


def _merge(x, mp, ma, mc, ms, zg, b_gate, wp, wa, wc, ws, wo):
    t = x.shape[0]
    row = lambda w: pl.BlockSpec((TM, w), lambda i: (i, 0))
    full = lambda a: pl.BlockSpec(a.shape, lambda i: (0, 0))
    return pl.pallas_call(
        _merge_kernel,
        grid=(t // TM,),
        in_specs=[row(D_MODEL), row(POOL_WIDTH), row(MLA_HEADS * V_HEAD), row(CONV_WIDTH), row(SGU_WIDTH),
                  row(N_BRANCH * D_MODEL), full(b_gate), full(wp), full(wa), full(wc), full(ws), full(wo)],
        out_specs=row(D_MODEL),
        out_shape=jax.ShapeDtypeStruct((t, D_MODEL), F32),
        scratch_shapes=[pltpu.VMEM((TM, D_MODEL), BF16)],
        compiler_params=_params(("parallel",)),
        name="merge_out",
    )(x, mp, ma, mc, ms, zg, b_gate, wp, wa, wc, ws, wo)


def _ffn_kernel(x_ref, g_ref, wg_ref, wu_ref, wd_ref, o_ref):
    x = x_ref[...]
    h = _rms(x, g_ref[...]).astype(BF16)
    half = D_FF // 2
    acc = x
    for f0 in range(0, D_FF, half):
        gate = _dot(h, wg_ref[:, f0:f0 + half])
        up = _dot(h, wu_ref[:, f0:f0 + half])
        act = (gate * _sigmoid(gate) * up).astype(BF16)
        acc = acc + _dot(act, wd_ref[f0:f0 + half, :])
    o_ref[...] = acc


def _ffn(x, g, wg, wu, wd):
    t = x.shape[0]
    full = lambda a: pl.BlockSpec(a.shape, lambda i: (0, 0))
    return pl.pallas_call(
        _ffn_kernel,
        grid=(t // TM,),
        in_specs=[pl.BlockSpec((TM, D_MODEL), lambda i: (i, 0)), full(g), full(wg), full(wu), full(wd)],
        out_specs=pl.BlockSpec((TM, D_MODEL), lambda i: (i, 0)),
        out_shape=jax.ShapeDtypeStruct((t, D_MODEL), F32),
        compiler_params=_params(("parallel",)),
        name="ffn_dense",
    )(x, g, wg, wu, wd)


def _router_kernel(x_ref, g_ref, wr_ref, br_ref, h_ref, info_ref, cnt_ref, carry_ref):
    @pl.when(pl.program_id(0) == 0)
    def _():
        carry_ref[...] = jnp.zeros(carry_ref.shape, F32)

    h = _rms(x_ref[...], g_ref[...])
    h_ref[...] = h
    logits = jnp.dot(h, wr_ref[...], preferred_element_type=F32,
                     precision=lax.Precision.HIGHEST) + br_ref[...]
    lane = lax.broadcasted_iota(jnp.int32, logits.shape, 1).astype(F32)
    logits = jnp.where(lane < N_EXPERTS, logits, NEG)
    v1 = jnp.max(logits, axis=1, keepdims=True)
    i1 = jnp.min(jnp.where(logits == v1, lane, float(LANES)), axis=1, keepdims=True)
    rest = jnp.where(lane == i1, NEG, logits)
    v2 = jnp.max(rest, axis=1, keepdims=True)
    i2 = jnp.min(jnp.where(rest == v2, lane, float(LANES)), axis=1, keepdims=True)
    e = jnp.exp(v2 - v1)
    w1 = 1.0 / (1.0 + e)
    w2 = e / (1.0 + e)
    chosen = jnp.where((lane == i1) | (lane == i2), 1.0, 0.0)
    row = lax.broadcasted_iota(jnp.int32, (TR, TR), 0)
    col = lax.broadcasted_iota(jnp.int32, (TR, TR), 1)
    before = jnp.where(col < row, 1.0, 0.0)
    pos = _dot(before, chosen) + carry_ref[0:1, :]
    r1 = jnp.sum(jnp.where(lane == i1, pos, 0.0), axis=1, keepdims=True)
    r2 = jnp.sum(jnp.where(lane == i2, pos, 0.0), axis=1, keepdims=True)
    carry_ref[0:1, :] = carry_ref[0:1, :] + jnp.sum(chosen, axis=0, keepdims=True)
    cnt_ref[...] = carry_ref[...]
    fields = (i1, i2, r1, r2, w1, w2)
    info = jnp.zeros(logits.shape, F32)
    for n, val in enumerate(fields):
        info = jnp.where(lane == n, val, info)
    info_ref[...] = info[:, :INFO_W]


def _router(x, g, wr, br):
    t = x.shape[0]
    full = lambda a: pl.BlockSpec(a.shape, lambda i: (0, 0))
    return pl.pallas_call(
        _router_kernel,
        grid=(t // TR,),
        in_specs=[pl.BlockSpec((TR, D_MODEL), lambda i: (i, 0)), full(g), full(wr), full(br)],
        out_specs=[
            pl.BlockSpec((TR, D_MODEL), lambda i: (i, 0)),
            pl.BlockSpec((TR, INFO_W), lambda i: (i, 0)),
            pl.BlockSpec((8, LANES), lambda i: (0, 0)),
        ],
        out_shape=[
            jax.ShapeDtypeStruct((t, D_MODEL), F32),
            jax.ShapeDtypeStruct((t, INFO_W), F32),
            jax.ShapeDtypeStruct((8, LANES), F32),
        ],
        scratch_shapes=[pltpu.VMEM((8, LANES), F32)],
        compiler_params=_params(("arbitrary",)),
        name="moe_router",
    )(x, g, wr, br)


def _num_row_tiles(t):
    return (2 * t) // TMG + N_EXPERTS


def _route_plan(info, cnt, t):
    counts = cnt[0, :N_EXPERTS].astype(jnp.int32)
    padded = ((counts + TMG - 1) // TMG) * TMG
    ends = jnp.cumsum(padded)
    off = ends - padded
    slot = off[info[:, 0:2].astype(jnp.int32)] + info[:, 2:4].astype(jnp.int32)
    slots = slot.reshape(t // BLK, BLK, 2).transpose(0, 2, 1).reshape(-1)
    starts = jnp.arange(_num_row_tiles(t), dtype=jnp.int32) * TMG
    tile_expert = jnp.searchsorted(ends, jnp.minimum(starts, ends[-1] - 1), side="right")
    n_tiles = (ends[-1] // TMG).reshape(1)
    return slots.astype(jnp.int32), tile_expert.astype(jnp.int32), n_tiles.astype(jnp.int32)


def _row_copy(src_ref, src_row, dst_ref, dst_row, sem):
    return pltpu.make_async_copy(src_ref.at[pl.ds(src_row, 1), :], dst_ref.at[pl.ds(dst_row, 1), :], sem)


def _dispatch_kernel(slots_ref, h_ref, xs_in_ref, xs_ref, sem):
    del xs_in_ref
    base = pl.program_id(0) * (2 * BLK)

    def body(r, carry):
        for j in range(2):
            _row_copy(h_ref, r, xs_ref, slots_ref[base + j * BLK + r], sem).start()
        return carry

    lax.fori_loop(0, BLK, body, 0)
    for j in range(2):
        pltpu.make_async_copy(h_ref, xs_ref.at[pl.ds(0, BLK), :], sem).wait()


def _dispatch(slots, h, n_rows):
    t = h.shape[0]
    return pl.pallas_call(
        _dispatch_kernel,
        grid_spec=pltpu.PrefetchScalarGridSpec(
            num_scalar_prefetch=1,
            grid=(t // BLK,),
            in_specs=[pl.BlockSpec((BLK, D_MODEL), lambda i, s: (i, 0)),
                      pl.BlockSpec(memory_space=pl.ANY)],
            out_specs=pl.BlockSpec(memory_space=pl.ANY),
            scratch_shapes=[pltpu.SemaphoreType.DMA(())],
        ),
        out_shape=jax.ShapeDtypeStruct((n_rows, D_MODEL), F32),
        input_output_aliases={2: 0},
        compiler_params=_params(("arbitrary",)),
        name="moe_dispatch",
    )(slots, h, jnp.zeros((n_rows, D_MODEL), F32))


def _experts_kernel(te_ref, nt_ref, xs_ref, wg_ref, wu_ref, wd_ref, y_ref, xb_ref, acc_ref):
    del te_ref, nt_ref
    f = pl.program_id(1)

    @pl.when(f == 0)
    def _():
        xb_ref[...] = xs_ref[...].astype(BF16)
        acc_ref[...] = jnp.zeros(acc_ref.shape, F32)

    h = xb_ref[...]
    part = jnp.zeros((TMG, D_MODEL), F32)
    for c0 in range(0, TF_G, TF_G // 2):
        cs = slice(c0, c0 + TF_G // 2)
        gate = _dot(h, wg_ref[:, cs])
        up = _dot(h, wu_ref[:, cs])
        part = part + _dot((gate * _sigmoid(gate) * up).astype(BF16), wd_ref[cs, :])
    acc_ref[...] += part

    @pl.when(f == pl.num_programs(1) - 1)
    def _():
        y_ref[...] = acc_ref[...]


def _experts(tile_expert, n_tiles, xs, wg, wu, wd):
    n_rows = xs.shape[0]
    nf = D_FF_EXPERT // TF_G

    def f_eff(g, f, nt):
        return jnp.where(g < nt[0], f, nf - 1)

    return pl.pallas_call(
        _experts_kernel,
        grid_spec=pltpu.PrefetchScalarGridSpec(
            num_scalar_prefetch=2,
            grid=(n_rows // TMG, nf),
            in_specs=[
                pl.BlockSpec((TMG, D_MODEL), lambda g, f, te, nt: (g, 0)),
                pl.BlockSpec((None, D_MODEL, TF_G), lambda g, f, te, nt: (te[g], 0, f_eff(g, f, nt))),
                pl.BlockSpec((None, D_MODEL, TF_G), lambda g, f, te, nt: (te[g], 0, f_eff(g, f, nt))),
                pl.BlockSpec((None, TF_G, D_MODEL), lambda g, f, te, nt: (te[g], f_eff(g, f, nt), 0)),
            ],
            out_specs=pl.BlockSpec((TMG, D_MODEL), lambda g, f, te, nt: (g, 0)),
            scratch_shapes=[pltpu.VMEM((TMG, D_MODEL), BF16), pltpu.VMEM((TMG, D_MODEL), F32)],
        ),
        out_shape=jax.ShapeDtypeStruct((n_rows, D_MODEL), F32),
        compiler_params=_params(("arbitrary", "arbitrary")),
        name="moe_experts",
    )(tile_expert, n_tiles, xs, wg, wu, wd)


def _combine_kernel(slots_ref, x_ref, info_ref, fg_ref, y_ref, o_ref, rows_ref, sem):
    base = pl.program_id(0) * (2 * BLK)

    def body(r, carry):
        for j in range(2):
            _row_copy(y_ref, slots_ref[base + j * BLK + r], rows_ref.at[j], r, sem).start()
        return carry

    lax.fori_loop(0, BLK, body, 0)
    for j in range(2):
        pltpu.make_async_copy(y_ref.at[pl.ds(0, BLK), :], rows_ref.at[j], sem).wait()
    w1 = info_ref[:, 4:5]
    w2 = info_ref[:, 5:6]
    o_ref[...] = _rms(x_ref[...] + w1 * rows_ref[0] + w2 * rows_ref[1], fg_ref[...])


def _combine(slots, x, info, final_g, y):
    t = x.shape[0]
    return pl.pallas_call(
        _combine_kernel,
        grid_spec=pltpu.PrefetchScalarGridSpec(
            num_scalar_prefetch=1,
            grid=(t // BLK,),
            in_specs=[pl.BlockSpec((BLK, D_MODEL), lambda i, s: (i, 0)),
                      pl.BlockSpec((BLK, INFO_W), lambda i, s: (i, 0)),
                      pl.BlockSpec((1, D_MODEL), lambda i, s: (0, 0)),
                      pl.BlockSpec(memory_space=pl.ANY)],
            out_specs=pl.BlockSpec((BLK, D_MODEL), lambda i, s: (i, 0)),
            scratch_shapes=[pltpu.VMEM((2, BLK, D_MODEL), F32), pltpu.SemaphoreType.DMA(())],
        ),
        out_shape=jax.ShapeDtypeStruct((t, D_MODEL), F32),
        compiler_params=_params(("arbitrary",)),
        name="moe_combine",
    )(slots, x, info, final_g, y)


def _rotate_half_cols(w):
    half = w.shape[-1] // 2
    return jnp.concatenate([-w[..., half:], w[..., :half]], axis=-1)


def _rope_tables():
    pos = jnp.arange(SEQ, dtype=F32)
    inv = ROPE_THETA ** (-jnp.arange(0, QK_ROPE, 2, dtype=F32) / QK_ROPE)
    ang = pos[:, None] * inv[None, :]
    cos, sin = jnp.cos(ang), jnp.sin(ang)
    one = jnp.ones((SEQ, QK_NOPE), F32)
    z_nope = jnp.zeros((SEQ, QK_NOPE), F32)
    z_pad = jnp.zeros((SEQ, HEAD_PAD - QK_NOPE - QK_ROPE), F32)
    z_rest = jnp.zeros((SEQ, LANES - QK_ROPE), F32)
    cq_tab = jnp.concatenate([one, cos, cos, z_pad], axis=1)
    sq_tab = jnp.concatenate([z_nope, sin, sin, z_pad], axis=1)
    ck_tab = jnp.concatenate([cos, cos, z_rest], axis=1)
    sk_tab = jnp.concatenate([sin, sin, z_rest], axis=1)
    return cq_tab, sq_tab, ck_tab, sk_tab


def _rope_placement():
    r = jnp.arange(QK_ROPE)
    place = jnp.zeros((LANES, MLA_HEADS * HEAD_PAD), F32)
    for h in range(MLA_HEADS):
        place = place.at[r, h * HEAD_PAD + QK_NOPE + r].set(1.0)
    return place.astype(BF16)


def _split_in_proj(w):
    zp, cq, ckv, kr, zc, zs, zg = jnp.split(
        w, (256, 512, 640, 672, 1184, 1696), axis=1)
    pad = jnp.zeros((D_MODEL, LANES - 2 * QK_ROPE), w.dtype)
    small = jnp.concatenate([zc, zs, zp, cq, ckv, kr, _rotate_half_cols(kr), pad], axis=1)
    return small.astype(BF16), zg.astype(BF16)


def _mla_weights(w_uq, w_ukv):
    wq = w_uq.reshape(Q_LORA, MLA_HEADS, QK_NOPE + QK_ROPE)
    nope, rope = wq[..., :QK_NOPE], wq[..., QK_NOPE:]
    zq = jnp.zeros((Q_LORA, MLA_HEADS, HEAD_PAD - QK_NOPE - QK_ROPE), w_uq.dtype)
    wqa = jnp.concatenate([nope, rope, zq], axis=-1).reshape(Q_LORA, MLA_HEADS * HEAD_PAD)
    wqb = jnp.concatenate([jnp.zeros_like(nope), _rotate_half_cols(rope), zq], axis=-1)
    wqb = wqb.reshape(Q_LORA, MLA_HEADS * HEAD_PAD)
    wkv = w_ukv.reshape(KV_LORA, MLA_HEADS, QK_NOPE + V_HEAD)
    k_nope, v = wkv[..., :QK_NOPE], wkv[..., QK_NOPE:]
    zk = jnp.zeros((KV_LORA, MLA_HEADS, HEAD_PAD - QK_NOPE), w_ukv.dtype)
    wk = jnp.concatenate([k_nope, zk], axis=-1).reshape(KV_LORA, MLA_HEADS * HEAD_PAD)
    wv = v.reshape(KV_LORA, MLA_HEADS * V_HEAD)
    return wqa.astype(BF16), wqb.astype(BF16), wk.astype(BF16), wv.astype(BF16)


def kernel(x, norm_mix_g, w_in, b_gate, w_pool, pool_scale, w_pool_out, q_norm_g, w_uq, kv_norm_g, w_ukv, w_mla_out, conv_w, conv_b, conv_ln_g, conv_ln_b, w_conv_out, sgu_ln_g, sgu_ln_b, sgu_w, sgu_b, w_sgu_out, w_o, norm_ffn_g, w_ffn_gate, w_ffn_up, w_ffn_down, w_router, b_router, w_moe_gate, w_moe_up, w_moe_down, final_norm_g):
    batch, seq, d = x.shape
    assert (seq, d) == (SEQ, D_MODEL)
    depth = w_in.shape[0]
    xt = x.reshape(batch * seq, d)
    tabs = _rope_tables()
    place = _rope_placement()
    row = lambda a: a.reshape(1, -1)

    for l in range(depth):
        w_small, w_gate = _split_in_proj(w_in[l])
        zs, zg = _in_proj(xt, row(norm_mix_g[l]), w_small, w_gate)

        w_bd = jax.scipy.linalg.block_diag(*[w_pool[l, g] for g in range(w_pool.shape[1])])
        m_pool = _pool(zs, w_bd.astype(BF16), row(pool_scale[l]))
        m_conv = _conv(zs, conv_w[l], row(conv_b[l]), row(conv_ln_g[l]), row(conv_ln_b[l]))
        bias_full = jnp.repeat(sgu_b[l].T, SGU_WIDTH // SGU_GROUPS, axis=1)
        m_sgu = _sgu(zs, row(sgu_ln_g[l]), row(sgu_ln_b[l]),
                     sgu_w[l].reshape(SGU_GROUPS * SGU_LEN, SGU_LEN), bias_full)
        wqa, wqb, wk, wv = _mla_weights(w_uq[l], w_ukv[l])
        q, k, v = _mla_prep(zs, row(q_norm_g[l]), row(kv_norm_g[l]), wqa, wqb, wk, wv, place, tabs)
        m_mla = _attention(q, k, v, batch)

        xt = _merge(xt, m_pool, m_mla, m_conv, m_sgu, zg, row(b_gate[l]),
                    w_pool_out[l].astype(BF16), w_mla_out[l].astype(BF16),
                    w_conv_out[l].astype(BF16), w_sgu_out[l].astype(BF16), w_o[l].astype(BF16))

        j = l // 2
        if l % 2 == 0:
            xt = _ffn(xt, row(norm_ffn_g[l]), w_ffn_gate[j].astype(BF16),
                      w_ffn_up[j].astype(BF16), w_ffn_down[j].astype(BF16))
        else:
            wr = jnp.pad(w_router[j], ((0, 0), (0, LANES - N_EXPERTS)))
            br = jnp.pad(b_router[j], (0, LANES - N_EXPERTS)).reshape(1, LANES)
            t = xt.shape[0]
            h, info, cnt = _router(xt, row(norm_ffn_g[l]), wr, br)
            slots, tile_expert, n_tiles = _route_plan(info, cnt, t)
            xs = _dispatch(slots, h, _num_row_tiles(t) * TMG)
            y = _experts(tile_expert, n_tiles, xs, w_moe_gate[j].astype(BF16),
                         w_moe_up[j].astype(BF16), w_moe_down[j].astype(BF16))
            assert l == depth - 1
            xt = _combine(slots, xt, info, row(final_norm_g), y)
    return xt.reshape(batch, seq, d)
```

```python
import functools

import jax
import jax.numpy as jnp
from jax import lax
from jax.experimental import pallas as pl
from jax.experimental.pallas import tpu as pltpu

F32 = jnp.float32
BF16 = jnp.bfloat16

D_MODEL = 1024
SEQ = 8192
CHUNK = 64
POOL_WIDTH = 256
POOL_WINDOWS = (2, 4, 8, 16)
MLA_HEADS = 8
Q_LORA = 256
KV_LORA = 128
QK_NOPE = 64
QK_ROPE = 32
V_HEAD = 64
ROPE_THETA = 10000.0
CONV_WIDTH = 256
CONV_K = 31
SGU_WIDTH = 256
SGU_GROUPS = 4
SGU_LEN = 128
N_BRANCH = 4
D_FF = 2816
N_EXPERTS = 8
D_FF_EXPERT = 3584
EPS = 1e-6
NEG = -1e30
LOG2E = 1.4426950408889634

LANES = 128
HEAD_PAD = 128
ZC_COL, ZS_COL, ZP_COL, CQ_COL, CKV_COL, KR_COL = 0, 512, 1024, 1280, 1536, 1664
Z_SMALL = 1792
POOL_HALO = 16
CONV_HALO = 32

TM = 512
TQ = 512
TK = 512
TR = 512
BLK = 256
ROW_UNROLL = 8
TMG = 512
TF_G = 1792
INFO_W = 8
VMEM_LIMIT = 56 * 1024 * 1024


def _params(sem):
    return pltpu.CompilerParams(dimension_semantics=sem, vmem_limit_bytes=VMEM_LIMIT)


def _rms(x, g):
    return x * lax.rsqrt(jnp.mean(x * x, axis=-1, keepdims=True) + EPS) * g


def _layernorm(x, g, b):
    mu = jnp.mean(x, axis=-1, keepdims=True)
    xc = x - mu
    var = jnp.mean(xc * xc, axis=-1, keepdims=True)
    return xc * lax.rsqrt(var + EPS) * g + b


def _sigmoid(x):
    return 0.5 * jnp.tanh(0.5 * x) + 0.5


def _dot(a, b):
    return jnp.dot(a, b, preferred_element_type=F32)


def _in_proj_kernel(x_ref, g_ref, ws_ref, wg_ref, zs_ref, zg_ref):
    h = _rms(x_ref[...], g_ref[...]).astype(BF16)
    for n0 in range(0, Z_SMALL, 896):
        zs_ref[:, n0:n0 + 896] = _dot(h, ws_ref[:, n0:n0 + 896])
    for n0 in range(0, N_BRANCH * D_MODEL, 1024):
        zg_ref[:, n0:n0 + 1024] = _dot(h, wg_ref[:, n0:n0 + 1024]).astype(BF16)


def _in_proj(x, g, w_small, w_gate):
    t = x.shape[0]
    return pl.pallas_call(
        _in_proj_kernel,
        grid=(t // TM,),
        in_specs=[
            pl.BlockSpec((TM, D_MODEL), lambda i: (i, 0)),
            pl.BlockSpec((1, D_MODEL), lambda i: (0, 0)),
            pl.BlockSpec((D_MODEL, Z_SMALL), lambda i: (0, 0)),
            pl.BlockSpec((D_MODEL, N_BRANCH * D_MODEL), lambda i: (0, 0)),
        ],
        out_specs=[
            pl.BlockSpec((TM, Z_SMALL), lambda i: (i, 0)),
            pl.BlockSpec((TM, N_BRANCH * D_MODEL), lambda i: (i, 0)),
        ],
        out_shape=[
            jax.ShapeDtypeStruct((t, Z_SMALL), F32),
            jax.ShapeDtypeStruct((t, N_BRANCH * D_MODEL), BF16),
        ],
        compiler_params=_params(("parallel",)),
        name="in_proj",
    )(x, g, w_small, w_gate)


def _pool_kernel(z_ref, halo_ref, w_ref, scale_ref, o_ref, buf_ref):
    i = pl.program_id(0)
    pos0 = (i * TM) % SEQ
    z = z_ref[...]
    buf_ref[0:POOL_HALO, :] = jnp.where(pos0 == 0, 0.0, halo_ref[...])
    buf_ref[POOL_HALO:, :] = z
    lane = lax.broadcasted_iota(jnp.int32, (TM, POOL_WIDTH), 1)
    pos = lax.broadcasted_iota(jnp.int32, (TM, POOL_WIDTH), 0) + pos0
    group = lane // (POOL_WIDTH // len(POOL_WINDOWS))
    acc = z
    total = jnp.zeros_like(z)
    win = jnp.zeros_like(lane)
    prev_w = 1
    for gi, w in enumerate(POOL_WINDOWS):
        for j in range(prev_w, w):
            acc = acc + buf_ref[POOL_HALO - j:POOL_HALO - j + TM, :]
        prev_w = w
        total = jnp.where(group == gi, acc, total)
        win = jnp.where(group == gi, w, win)
    cnt = jnp.minimum(pos + 1, win).astype(F32)
    pooled = total / cnt - z
    y = _dot(pooled.astype(BF16), w_ref[...]) * scale_ref[...]
    o_ref[...] = y.astype(BF16)


def _pool(zs, w_bd, scale):
    t = zs.shape[0]
    cb = ZP_COL // POOL_WIDTH
    r = TM // POOL_HALO
    return pl.pallas_call(
        _pool_kernel,
        grid=(t // TM,),
        in_specs=[
            pl.BlockSpec((TM, POOL_WIDTH), lambda i: (i, cb)),
            pl.BlockSpec((POOL_HALO, POOL_WIDTH), lambda i: (jnp.maximum(i * r - 1, 0), cb)),
            pl.BlockSpec((POOL_WIDTH, POOL_WIDTH), lambda i: (0, 0)),
            pl.BlockSpec((1, POOL_WIDTH), lambda i: (0, 0)),
        ],
        out_specs=pl.BlockSpec((TM, POOL_WIDTH), lambda i: (i, 0)),
        out_shape=jax.ShapeDtypeStruct((t, POOL_WIDTH), BF16),
        scratch_shapes=[pltpu.VMEM((TM + POOL_HALO, POOL_WIDTH), F32)],
        compiler_params=_params(("parallel",)),
        name="pool_mixer",
    )(zs, zs, w_bd, scale)


def _glu(z2):
    return z2[:, :CONV_WIDTH] * _sigmoid(z2[:, CONV_WIDTH:])


def _conv_kernel(z_ref, halo_ref, w_ref, b_ref, lg_ref, lb_ref, o_ref, buf_ref):
    i = pl.program_id(0)
    pos0 = (i * TM) % SEQ
    buf_ref[0:CONV_HALO, :] = jnp.where(pos0 == 0, 0.0, _glu(halo_ref[...]))
    buf_ref[CONV_HALO:, :] = _glu(z_ref[...])
    off = CONV_HALO - (CONV_K - 1)
    y = jnp.zeros((TM, CONV_WIDTH), F32) + b_ref[...]
    for k in range(CONV_K):
        y = y + w_ref[k:k + 1, :] * buf_ref[off + k:off + k + TM, :]
    yn = _layernorm(y, lg_ref[...], lb_ref[...])
    o_ref[...] = (yn * _sigmoid(yn)).astype(BF16)


def _conv(zs, conv_w, conv_b, ln_g, ln_b):
    t = zs.shape[0]
    cb = ZC_COL // (2 * CONV_WIDTH)
    r = TM // CONV_HALO
    vec = pl.BlockSpec((1, CONV_WIDTH), lambda i: (0, 0))
    return pl.pallas_call(
        _conv_kernel,
        grid=(t // TM,),
        in_specs=[
            pl.BlockSpec((TM, 2 * CONV_WIDTH), lambda i: (i, cb)),
            pl.BlockSpec((CONV_HALO, 2 * CONV_WIDTH), lambda i: (jnp.maximum(i * r - 1, 0), cb)),
            pl.BlockSpec((CONV_K, CONV_WIDTH), lambda i: (0, 0)),
            vec, vec, vec,
        ],
        out_specs=pl.BlockSpec((TM, CONV_WIDTH), lambda i: (i, 0)),
        out_shape=jax.ShapeDtypeStruct((t, CONV_WIDTH), BF16),
        scratch_shapes=[pltpu.VMEM((TM + CONV_HALO, CONV_WIDTH), F32)],
        compiler_params=_params(("parallel",)),
        name="conv_mixer",
    )(zs, zs, conv_w, conv_b, ln_g, ln_b)


def _sgu_kernel(z_ref, lg_ref, lb_ref, ws_ref, bias_ref, o_ref):
    z = z_ref[...]
    z = 0.5 * z * (1.0 + lax.erf(z * (2.0 ** -0.5)))
    u = z[:, :SGU_WIDTH]
    v = _layernorm(z[:, SGU_WIDTH:], lg_ref[...], lb_ref[...]).astype(BF16)
    rows = SGU_GROUPS * SGU_LEN
    row = lax.broadcasted_iota(jnp.int32, (rows, SGU_LEN), 0) % SGU_LEN
    col = lax.broadcasted_iota(jnp.int32, (rows, SGU_LEN), 1)
    ws = jnp.where(col <= row, ws_ref[...], 0.0).astype(BF16)
    lane_group = lax.broadcasted_iota(jnp.int32, (SGU_LEN, SGU_WIDTH), 1) // (SGU_WIDTH // SGU_GROUPS)
    for blk in range(TM // SGU_LEN):
        r0 = blk * SGU_LEN
        full = _dot(ws, v[r0:r0 + SGU_LEN, :])
        mixed = full[0:SGU_LEN, :]
        for g in range(1, SGU_GROUPS):
            mixed = jnp.where(lane_group == g, full[g * SGU_LEN:(g + 1) * SGU_LEN, :], mixed)
        mixed = mixed + bias_ref[...]
        o_ref[r0:r0 + SGU_LEN, :] = (u[r0:r0 + SGU_LEN, :] * mixed).astype(BF16)


def _sgu(zs, ln_g, ln_b, ws_stack, bias_full):
    t = zs.shape[0]
    cb = ZS_COL // (2 * SGU_WIDTH)
    vec = pl.BlockSpec((1, SGU_WIDTH), lambda i: (0, 0))
    return pl.pallas_call(
        _sgu_kernel,
        grid=(t // TM,),
        in_specs=[
            pl.BlockSpec((TM, 2 * SGU_WIDTH), lambda i: (i, cb)),
            vec, vec,
            pl.BlockSpec((SGU_GROUPS * SGU_LEN, SGU_LEN), lambda i: (0, 0)),
            pl.BlockSpec((SGU_LEN, SGU_WIDTH), lambda i: (0, 0)),
        ],
        out_specs=pl.BlockSpec((TM, SGU_WIDTH), lambda i: (i, 0)),
        out_shape=jax.ShapeDtypeStruct((t, SGU_WIDTH), BF16),
        compiler_params=_params(("parallel",)),
        name="sgu_mixer",
    )(zs, ln_g, ln_b, ws_stack, bias_full)


def _tile_heads(tab):
    return jnp.concatenate([tab] * MLA_HEADS, axis=1)


def _mla_prep_kernel(cq_ref, ckv_ref, kr_ref, qg_ref, kvg_ref, wqa_ref, wqb_ref, wk_ref, wv_ref,
                     place_ref, cq_tab_ref, sq_tab_ref, ck_tab_ref, sk_tab_ref,
                     qt_ref, k_ref, vt_ref):
    cqn = _rms(cq_ref[...], qg_ref[...]).astype(BF16)
    qa = _dot(cqn, wqa_ref[...])
    qb = _dot(cqn, wqb_ref[...])
    scale = (QK_NOPE + QK_ROPE) ** -0.5 * LOG2E
    q = (qa * _tile_heads(cq_tab_ref[...]) + qb * _tile_heads(sq_tab_ref[...])) * scale
    qt_ref[...] = q.T.astype(BF16)
    kvn = _rms(ckv_ref[...], kvg_ref[...]).astype(BF16)
    vt = _dot(kvn, wv_ref[...]).T
    for c in range(TM // TK):
        vt_ref[c] = vt[:, c * TK:(c + 1) * TK].astype(BF16)
    kr = kr_ref[...]
    kro = kr * ck_tab_ref[...] + pltpu.roll(kr, LANES - QK_ROPE, 1) * sk_tab_ref[...]
    k = _dot(kvn, wk_ref[...]) + _dot(kro.astype(BF16), place_ref[...])
    k_ref[...] = k.astype(BF16)


def _mla_prep(zs, q_g, kv_g, wqa, wqb, wk, wv, place, tabs):
    t = zs.shape[0]
    hw = MLA_HEADS * HEAD_PAD
    hv = MLA_HEADS * V_HEAD
    ns = SEQ // TM
    tab = pl.BlockSpec((TM, LANES), lambda i: (i % ns, 0))
    full = lambda a: pl.BlockSpec(a.shape, lambda i: (0, 0))
    return pl.pallas_call(
        _mla_prep_kernel,
        grid=(t // TM,),
        in_specs=[
            pl.BlockSpec((TM, Q_LORA), lambda i: (i, CQ_COL // Q_LORA)),
            pl.BlockSpec((TM, KV_LORA), lambda i: (i, CKV_COL // KV_LORA)),
            pl.BlockSpec((TM, LANES), lambda i: (i, KR_COL // LANES)),
            full(q_g), full(kv_g), full(wqa), full(wqb), full(wk), full(wv), full(place),
            tab, tab, tab, tab,
        ],
        out_specs=[
            pl.BlockSpec((hw, TM), lambda i: (0, i)),
            pl.BlockSpec((TM, hw), lambda i: (i, 0)),
            pl.BlockSpec((TM // TK, hv, TK), lambda i: (i, 0, 0)),
        ],
        out_shape=[
            jax.ShapeDtypeStruct((hw, t), BF16),
            jax.ShapeDtypeStruct((t, hw), BF16),
            jax.ShapeDtypeStruct((t // TK, hv, TK), BF16),
        ],
        compiler_params=_params(("parallel",)),
        name="mla_prep",
    )(zs, zs, zs, q_g, kv_g, wqa, wqb, wk, wv, place, *tabs)


def _attn_kernel(qt_ref, k_ref, vt_ref, o_ref, m_ref, l_ref, acc_ref, st_ref):
    qi = pl.program_id(1)
    m_ref[...] = jnp.full(m_ref.shape, NEG, F32)
    l_ref[...] = jnp.zeros(l_ref.shape, F32)
    acc_ref[...] = jnp.zeros(acc_ref.shape, F32)
    key_chunk = lax.broadcasted_iota(jnp.int32, (TK, TQ), 0) // CHUNK
    qry_chunk = lax.broadcasted_iota(jnp.int32, (TK, TQ), 1) // CHUNK
    visible = key_chunk <= qry_chunk

    def step(j, masked):
        ks = pl.multiple_of(j * TK, TK)
        tile_max = []
        for h in range(MLA_HEADS):
            hs = slice(h * HEAD_PAD, (h + 1) * HEAD_PAD)
            st = _dot(k_ref[pl.ds(ks, TK), hs], qt_ref[hs, :])
            if masked:
                st = jnp.where(visible, st, NEG)
            st_ref[h] = st
            tile_max.append(jnp.max(st, axis=0, keepdims=True))
        for h in range(MLA_HEADS):
            m_prev = m_ref[h]
            m_new = jnp.maximum(m_prev, tile_max[h])
            alpha = jnp.exp2(m_prev - m_new)
            p = jnp.exp2(st_ref[h] - m_new)
            l_ref[h] = alpha * l_ref[h] + jnp.sum(p, axis=0, keepdims=True)
            m_ref[h] = m_new
            rows = slice(h * V_HEAD, (h + 1) * V_HEAD)
            acc_ref[rows, :] = alpha * acc_ref[rows, :] + _dot(vt_ref[j, rows, :], p.astype(BF16))

    def body(j, carry):
        step(j, False)
        return carry

    lax.fori_loop(0, qi, body, 0)
    step(qi, True)
    outs = [acc_ref[h * V_HEAD:(h + 1) * V_HEAD, :] / l_ref[h] for h in range(MLA_HEADS)]
    o_ref[...] = jnp.concatenate(outs, axis=0).T.astype(BF16)


def _attention(qt, k, vt, batch):
    t = k.shape[0]
    nq = SEQ // TQ
    nk = SEQ // TK
    hw = MLA_HEADS * HEAD_PAD
    hv = MLA_HEADS * V_HEAD
    once = pl.Buffered(1)
    return pl.pallas_call(
        _attn_kernel,
        grid=(batch, nq),
        in_specs=[
            pl.BlockSpec((hw, TQ), lambda b, i: (0, b * nq + i)),
            pl.BlockSpec((SEQ, hw), lambda b, i: (b, 0), pipeline_mode=once),
            pl.BlockSpec((nk, hv, TK), lambda b, i: (b, 0, 0), pipeline_mode=once),
        ],
        out_specs=pl.BlockSpec((TQ, hv), lambda b, i: (b * nq + i, 0)),
        out_shape=jax.ShapeDtypeStruct((t, hv), BF16),
        scratch_shapes=[
            pltpu.VMEM((MLA_HEADS, 1, TQ), F32),
            pltpu.VMEM((MLA_HEADS, 1, TQ), F32),
            pltpu.VMEM((hv, TQ), F32),
            pltpu.VMEM((MLA_HEADS, TK, TQ), F32),
        ],
        compiler_params=_params(("parallel", "arbitrary")),
        name="mla_attention",
    )(qt, k, vt)


def _merge_kernel(x_ref, mp_ref, ma_ref, mc_ref, ms_ref, zg_ref, bg_ref,
                  wp_ref, wa_ref, wc_ref, ws_ref, wo_ref, o_ref, mg_ref):
    branches = ((mp_ref, wp_ref), (ma_ref, wa_ref), (mc_ref, wc_ref), (ms_ref, ws_ref))
    half = D_MODEL // 2
    for n0 in range(0, D_MODEL, half):
        merged = jnp.zeros((TM, half), F32)
        for b, (m_ref, w_ref) in enumerate(branches):
            c0 = b * D_MODEL + n0
            gate = _sigmoid(zg_ref[:, c0:c0 + half].astype(F32) + bg_ref[:, c0:c0 + half])
            merged = merged + gate * _dot(m_ref[...], w_ref[:, n0:n0 + half])
        mg_ref[:, n0:n0 + half] = merged.astype(BF16)
    o_ref[...] = x_ref[...] + _dot(mg_ref[...], wo_ref[...])


def _merge(x, mp, ma, mc, ms, zg, b_gate, wp, wa, wc, ws, wo):
    t = x.shape[0]
    row = lambda w: pl.BlockSpec((TM, w), lambda i: (i, 0))
    full = lambda a: pl.BlockSpec(a.shape, lambda i: (0, 0))
    return pl.pallas_call(
        _merge_kernel,
        grid=(t // TM,),
        in_specs=[row(D_MODEL), row(POOL_WIDTH), row(MLA_HEADS * V_HEAD), row(CONV_WIDTH), row(SGU_WIDTH),
                  row(N_BRANCH * D_MODEL), full(b_gate), full(wp), full(wa), full(wc), full(ws), full(wo)],
        out_specs=row(D_MODEL),
        out_shape=jax.ShapeDtypeStruct((t, D_MODEL), F32),
        scratch_shapes=[pltpu.VMEM((TM, D_MODEL), BF16)],
        compiler_params=_params(("parallel",)),
        name="merge_out",
    )(x, mp, ma, mc, ms, zg, b_gate, wp, wa, wc, ws, wo)


def _ffn_kernel(x_ref, g_ref, wg_ref, wu_ref, wd_ref, o_ref):
    x = x_ref[...]
    h = _rms(x, g_ref[...]).astype(BF16)
    half = D_FF // 2
    acc = x
    for f0 in range(0, D_FF, half):
        gate = _dot(h, wg_ref[:, f0:f0 + half])
        up = _dot(h, wu_ref[:, f0:f0 + half])
        act = (gate * _sigmoid(gate) * up).astype(BF16)
        acc = acc + _dot(act, wd_ref[f0:f0 + half, :])
    o_ref[...] = acc


def _ffn(x, g, wg, wu, wd):
    t = x.shape[0]
    full = lambda a: pl.BlockSpec(a.shape, lambda i: (0, 0))
    return pl.pallas_call(
        _ffn_kernel,
        grid=(t // TM,),
        in_specs=[pl.BlockSpec((TM, D_MODEL), lambda i: (i, 0)), full(g), full(wg), full(wu), full(wd)],
        out_specs=pl.BlockSpec((TM, D_MODEL), lambda i: (i, 0)),
        out_shape=jax.ShapeDtypeStruct((t, D_MODEL), F32),
        compiler_params=_params(("parallel",)),
        name="ffn_dense",
    )(x, g, wg, wu, wd)


def _router_kernel(x_ref, g_ref, wr_ref, br_ref, h_ref, info_ref, cnt_ref, carry_ref):
    @pl.when(pl.program_id(0) == 0)
    def _():
        carry_ref[...] = jnp.zeros(carry_ref.shape, F32)

    h = _rms(x_ref[...], g_ref[...])
    h_ref[...] = h
    logits = jnp.dot(h, wr_ref[...], preferred_element_type=F32,
                     precision=lax.Precision.HIGHEST) + br_ref[...]
    lane = lax.broadcasted_iota(jnp.int32, logits.shape, 1).astype(F32)
    logits = jnp.where(lane < N_EXPERTS, logits, NEG)
    v1 = jnp.max(logits, axis=1, keepdims=True)
    i1 = jnp.min(jnp.where(logits == v1, lane, float(LANES)), axis=1, keepdims=True)
    rest = jnp.where(lane == i1, NEG, logits)
    v2 = jnp.max(rest, axis=1, keepdims=True)
    i2 = jnp.min(jnp.where(rest == v2, lane, float(LANES)), axis=1, keepdims=True)
    e = jnp.exp(v2 - v1)
    w1 = 1.0 / (1.0 + e)
    w2 = e / (1.0 + e)
    chosen = jnp.where((lane == i1) | (lane == i2), 1.0, 0.0)
    row = lax.broadcasted_iota(jnp.int32, (TR, TR), 0)
    col = lax.broadcasted_iota(jnp.int32, (TR, TR), 1)
    before = jnp.where(col < row, 1.0, 0.0)
    pos = _dot(before, chosen) + carry_ref[0:1, :]
    r1 = jnp.sum(jnp.where(lane == i1, pos, 0.0), axis=1, keepdims=True)
    r2 = jnp.sum(jnp.where(lane == i2, pos, 0.0), axis=1, keepdims=True)
    carry_ref[0:1, :] = carry_ref[0:1, :] + jnp.sum(chosen, axis=0, keepdims=True)
    cnt_ref[...] = carry_ref[...]
    fields = (i1, i2, r1, r2, w1, w2)
    info = jnp.zeros(logits.shape, F32)
    for n, val in enumerate(fields):
        info = jnp.where(lane == n, val, info)
    info_ref[...] = info[:, :INFO_W]


def _router(x, g, wr, br):
    t = x.shape[0]
    full = lambda a: pl.BlockSpec(a.shape, lambda i: (0, 0))
    return pl.pallas_call(
        _router_kernel,
        grid=(t // TR,),
        in_specs=[pl.BlockSpec((TR, D_MODEL), lambda i: (i, 0)), full(g), full(wr), full(br)],
        out_specs=[
            pl.BlockSpec((TR, D_MODEL), lambda i: (i, 0)),
            pl.BlockSpec((TR, INFO_W), lambda i: (i, 0)),
            pl.BlockSpec((8, LANES), lambda i: (0, 0)),
        ],
        out_shape=[
            jax.ShapeDtypeStruct((t, D_MODEL), F32),
            jax.ShapeDtypeStruct((t, INFO_W), F32),
            jax.ShapeDtypeStruct((8, LANES), F32),
        ],
        scratch_shapes=[pltpu.VMEM((8, LANES), F32)],
        compiler_params=_params(("arbitrary",)),
        name="moe_router",
    )(x, g, wr, br)


def _num_row_tiles(t):
    return (2 * t) // TMG + N_EXPERTS


def _route_plan(info, cnt, t):
    counts = cnt[0, :N_EXPERTS].astype(jnp.int32)
    padded = ((counts + TMG - 1) // TMG) * TMG
    ends = jnp.cumsum(padded)
    off = ends - padded
    slot = off[info[:, 0:2].astype(jnp.int32)] + info[:, 2:4].astype(jnp.int32)
    slots = slot.reshape(t // BLK, BLK, 2).transpose(0, 2, 1).reshape(-1)
    starts = jnp.arange(_num_row_tiles(t), dtype=jnp.int32) * TMG
    first_row = jnp.minimum(starts, ends[-1] - 1)
    tile_expert = jnp.sum(ends[None, :] <= first_row[:, None], axis=1)
    n_tiles = (ends[-1] // TMG).reshape(1)
    return slots.astype(jnp.int32), tile_expert.astype(jnp.int32), n_tiles.astype(jnp.int32)


def _row_copy(src_ref, src_row, dst_ref, dst_row, sem):
    return pltpu.make_async_copy(src_ref.at[pl.ds(src_row, 1), :], dst_ref.at[pl.ds(dst_row, 1), :], sem)


def _dispatch_kernel(slots_ref, h_ref, xs_in_ref, xs_ref, sem):
    del xs_in_ref
    base = pl.program_id(0) * (2 * BLK)

    def body(r, carry):
        for j in range(2):
            _row_copy(h_ref, r, xs_ref, slots_ref[base + j * BLK + r], sem).start()
        return carry

    lax.fori_loop(0, BLK, body, 0, unroll=ROW_UNROLL)
    for j in range(2):
        pltpu.make_async_copy(h_ref, xs_ref.at[pl.ds(0, BLK), :], sem).wait()


def _dispatch(slots, h, n_rows):
    t = h.shape[0]
    return pl.pallas_call(
        _dispatch_kernel,
        grid_spec=pltpu.PrefetchScalarGridSpec(
            num_scalar_prefetch=1,
            grid=(t // BLK,),
            in_specs=[pl.BlockSpec((BLK, D_MODEL), lambda i, s: (i, 0)),
                      pl.BlockSpec(memory_space=pl.ANY)],
            out_specs=pl.BlockSpec(memory_space=pl.ANY),
            scratch_shapes=[pltpu.SemaphoreType.DMA(())],
        ),
        out_shape=jax.ShapeDtypeStruct((n_rows, D_MODEL), F32),
        input_output_aliases={2: 0},
        compiler_params=_params(("arbitrary",)),
        name="moe_dispatch",
    )(slots, h, jnp.zeros((n_rows, D_MODEL), F32))


def _experts_kernel(te_ref, nt_ref, xs_ref, wg_ref, wu_ref, wd_ref, y_ref, xb_ref, acc_ref):
    del te_ref, nt_ref
    f = pl.program_id(1)

    @pl.when(f == 0)
    def _():
        xb_ref[...] = xs_ref[...].astype(BF16)
        acc_ref[...] = jnp.zeros(acc_ref.shape, F32)

    h = xb_ref[...]
    part = jnp.zeros((TMG, D_MODEL), F32)
    for c0 in range(0, TF_G, TF_G // 2):
        cs = slice(c0, c0 + TF_G // 2)
        gate = _dot(h, wg_ref[:, cs])
        up = _dot(h, wu_ref[:, cs])
        part = part + _dot((gate * _sigmoid(gate) * up).astype(BF16), wd_ref[cs, :])
    acc_ref[...] += part

    @pl.when(f == pl.num_programs(1) - 1)
    def _():
        y_ref[...] = acc_ref[...]


def _experts(tile_expert, n_tiles, xs, wg, wu, wd):
    n_rows = xs.shape[0]
    nf = D_FF_EXPERT // TF_G

    def f_eff(g, f, nt):
        return jnp.where(g < nt[0], f, nf - 1)

    return pl.pallas_call(
        _experts_kernel,
        grid_spec=pltpu.PrefetchScalarGridSpec(
            num_scalar_prefetch=2,
            grid=(n_rows // TMG, nf),
            in_specs=[
                pl.BlockSpec((TMG, D_MODEL), lambda g, f, te, nt: (g, 0)),
                pl.BlockSpec((None, D_MODEL, TF_G), lambda g, f, te, nt: (te[g], 0, f_eff(g, f, nt))),
                pl.BlockSpec((None, D_MODEL, TF_G), lambda g, f, te, nt: (te[g], 0, f_eff(g, f, nt))),
                pl.BlockSpec((None, TF_G, D_MODEL), lambda g, f, te, nt: (te[g], f_eff(g, f, nt), 0)),
            ],
            out_specs=pl.BlockSpec((TMG, D_MODEL), lambda g, f, te, nt: (g, 0)),
            scratch_shapes=[pltpu.VMEM((TMG, D_MODEL), BF16), pltpu.VMEM((TMG, D_MODEL), F32)],
        ),
        out_shape=jax.ShapeDtypeStruct((n_rows, D_MODEL), F32),
        compiler_params=_params(("arbitrary", "arbitrary")),
        name="moe_experts",
    )(tile_expert, n_tiles, xs, wg, wu, wd)


def _combine_kernel(slots_ref, x_ref, info_ref, fg_ref, y_ref, o_ref, rows_ref, sem):
    base = pl.program_id(0) * (2 * BLK)

    def body(r, carry):
        for j in range(2):
            _row_copy(y_ref, slots_ref[base + j * BLK + r], rows_ref.at[j], r, sem).start()
        return carry

    lax.fori_loop(0, BLK, body, 0, unroll=ROW_UNROLL)
    for j in range(2):
        pltpu.make_async_copy(y_ref.at[pl.ds(0, BLK), :], rows_ref.at[j], sem).wait()
    w1 = info_ref[:, 4:5]
    w2 = info_ref[:, 5:6]
    o_ref[...] = _rms(x_ref[...] + w1 * rows_ref[0] + w2 * rows_ref[1], fg_ref[...])


def _combine(slots, x, info, final_g, y):
    t = x.shape[0]
    return pl.pallas_call(
        _combine_kernel,
        grid_spec=pltpu.PrefetchScalarGridSpec(
            num_scalar_prefetch=1,
            grid=(t // BLK,),
            in_specs=[pl.BlockSpec((BLK, D_MODEL), lambda i, s: (i, 0)),
                      pl.BlockSpec((BLK, INFO_W), lambda i, s: (i, 0)),
                      pl.BlockSpec((1, D_MODEL), lambda i, s: (0, 0)),
                      pl.BlockSpec(memory_space=pl.ANY)],
            out_specs=pl.BlockSpec((BLK, D_MODEL), lambda i, s: (i, 0)),
            scratch_shapes=[pltpu.VMEM((2, BLK, D_MODEL), F32), pltpu.SemaphoreType.DMA(())],
        ),
        out_shape=jax.ShapeDtypeStruct((t, D_MODEL), F32),
        compiler_params=_params(("arbitrary",)),
        name="moe_combine",
    )(slots, x, info, final_g, y)


def _rotate_half_cols(w):
    half = w.shape[-1] // 2
    return jnp.concatenate([-w[..., half:], w[..., :half]], axis=-1)


def _rope_tables():
    pos = jnp.arange(SEQ, dtype=F32)
    inv = ROPE_THETA ** (-jnp.arange(0, QK_ROPE, 2, dtype=F32) / QK_ROPE)
    ang = pos[:, None] * inv[None, :]
    cos, sin = jnp.cos(ang), jnp.sin(ang)
    one = jnp.ones((SEQ, QK_NOPE), F32)
    z_nope = jnp.zeros((SEQ, QK_NOPE), F32)
    z_pad = jnp.zeros((SEQ, HEAD_PAD - QK_NOPE - QK_ROPE), F32)
    z_rest = jnp.zeros((SEQ, LANES - QK_ROPE), F32)
    cq_tab = jnp.concatenate([one, cos, cos, z_pad], axis=1)
    sq_tab = jnp.concatenate([z_nope, sin, sin, z_pad], axis=1)
    ck_tab = jnp.concatenate([cos, cos, z_rest], axis=1)
    sk_tab = jnp.concatenate([sin, sin, z_rest], axis=1)
    return cq_tab, sq_tab, ck_tab, sk_tab


def _rope_placement():
    r = jnp.arange(QK_ROPE)
    place = jnp.zeros((LANES, MLA_HEADS * HEAD_PAD), F32)
    for h in range(MLA_HEADS):
        place = place.at[r, h * HEAD_PAD + QK_NOPE + r].set(1.0)
    return place.astype(BF16)


def _split_in_proj(w):
    zp, cq, ckv, kr, zc, zs, zg = jnp.split(
        w, (256, 512, 640, 672, 1184, 1696), axis=1)
    pad = jnp.zeros((D_MODEL, LANES - 2 * QK_ROPE), w.dtype)
    small = jnp.concatenate([zc, zs, zp, cq, ckv, kr, _rotate_half_cols(kr), pad], axis=1)
    return small.astype(BF16), zg.astype(BF16)


def _mla_weights(w_uq, w_ukv):
    wq = w_uq.reshape(Q_LORA, MLA_HEADS, QK_NOPE + QK_ROPE)
    nope, rope = wq[..., :QK_NOPE], wq[..., QK_NOPE:]
    zq = jnp.zeros((Q_LORA, MLA_HEADS, HEAD_PAD - QK_NOPE - QK_ROPE), w_uq.dtype)
    wqa = jnp.concatenate([nope, rope, zq], axis=-1).reshape(Q_LORA, MLA_HEADS * HEAD_PAD)
    wqb = jnp.concatenate([jnp.zeros_like(nope), _rotate_half_cols(rope), zq], axis=-1)
    wqb = wqb.reshape(Q_LORA, MLA_HEADS * HEAD_PAD)
    wkv = w_ukv.reshape(KV_LORA, MLA_HEADS, QK_NOPE + V_HEAD)
    k_nope, v = wkv[..., :QK_NOPE], wkv[..., QK_NOPE:]
    zk = jnp.zeros((KV_LORA, MLA_HEADS, HEAD_PAD - QK_NOPE), w_ukv.dtype)
    wk = jnp.concatenate([k_nope, zk], axis=-1).reshape(KV_LORA, MLA_HEADS * HEAD_PAD)
    wv = v.reshape(KV_LORA, MLA_HEADS * V_HEAD)
    return wqa.astype(BF16), wqb.astype(BF16), wk.astype(BF16), wv.astype(BF16)


def kernel(x, norm_mix_g, w_in, b_gate, w_pool, pool_scale, w_pool_out, q_norm_g, w_uq, kv_norm_g, w_ukv, w_mla_out, conv_w, conv_b, conv_ln_g, conv_ln_b, w_conv_out, sgu_ln_g, sgu_ln_b, sgu_w, sgu_b, w_sgu_out, w_o, norm_ffn_g, w_ffn_gate, w_ffn_up, w_ffn_down, w_router, b_router, w_moe_gate, w_moe_up, w_moe_down, final_norm_g):
    batch, seq, d = x.shape
    assert (seq, d) == (SEQ, D_MODEL)
    depth = w_in.shape[0]
    xt = x.reshape(batch * seq, d)
    tabs = _rope_tables()
    place = _rope_placement()
    row = lambda a: a.reshape(1, -1)

    for l in range(depth):
        w_small, w_gate = _split_in_proj(w_in[l])
        zs, zg = _in_proj(xt, row(norm_mix_g[l]), w_small, w_gate)

        w_bd = jax.scipy.linalg.block_diag(*[w_pool[l, g] for g in range(w_pool.shape[1])])
        m_pool = _pool(zs, w_bd.astype(BF16), row(pool_scale[l]))
        m_conv = _conv(zs, conv_w[l], row(conv_b[l]), row(conv_ln_g[l]), row(conv_ln_b[l]))
        bias_full = jnp.repeat(sgu_b[l].T, SGU_WIDTH // SGU_GROUPS, axis=1)
        m_sgu = _sgu(zs, row(sgu_ln_g[l]), row(sgu_ln_b[l]),
                     sgu_w[l].reshape(SGU_GROUPS * SGU_LEN, SGU_LEN), bias_full)
        wqa, wqb, wk, wv = _mla_weights(w_uq[l], w_ukv[l])
        q, k, v = _mla_prep(zs, row(q_norm_g[l]), row(kv_norm_g[l]), wqa, wqb, wk, wv, place, tabs)
        m_mla = _attention(q, k, v, batch)

        xt = _merge(xt, m_pool, m_mla, m_conv, m_sgu, zg, row(b_gate[l]),
                    w_pool_out[l].astype(BF16), w_mla_out[l].astype(BF16),
                    w_conv_out[l].astype(BF16), w_sgu_out[l].astype(BF16), w_o[l].astype(BF16))

        j = l // 2
        if l % 2 == 0:
            xt = _ffn(xt, row(norm_ffn_g[l]), w_ffn_gate[j].astype(BF16),
                      w_ffn_up[j].astype(BF16), w_ffn_down[j].astype(BF16))
        else:
            wr = jnp.pad(w_router[j], ((0, 0), (0, LANES - N_EXPERTS)))
            br = jnp.pad(b_router[j], (0, LANES - N_EXPERTS)).reshape(1, LANES)
            t = xt.shape[0]
            h, info, cnt = _router(xt, row(norm_ffn_g[l]), wr, br)
            slots, tile_expert, n_tiles = _route_plan(info, cnt, t)
            xs = _dispatch(slots, h, _num_row_tiles(t) * TMG)
            y = _experts(tile_expert, n_tiles, xs, w_moe_gate[j].astype(BF16),
                         w_moe_up[j].astype(BF16), w_moe_down[j].astype(BF16))
            assert l == depth - 1
            xt = _combine(slots, xt, info, row(final_norm_g), y)
    return xt.reshape(batch, seq, d)
```

```python
import functools

import jax
import jax.numpy as jnp
from jax import lax
from jax.experimental import pallas as pl
from jax.experimental.pallas import tpu as pltpu

F32 = jnp.float32
BF16 = jnp.bfloat16

D_MODEL = 1024
SEQ = 8192
CHUNK = 64
POOL_WIDTH = 256
POOL_WINDOWS = (2, 4, 8, 16)
MLA_HEADS = 8
Q_LORA = 256
KV_LORA = 128
QK_NOPE = 64
QK_ROPE = 32
V_HEAD = 64
V_ROWS = 80
ROPE_THETA = 10000.0
CONV_WIDTH = 256
CONV_K = 31
SGU_WIDTH = 256
SGU_GROUPS = 4
SGU_LEN = 128
N_BRANCH = 4
D_FF = 2816
N_EXPERTS = 8
D_FF_EXPERT = 3584
EPS = 1e-6
NEG = -1e30
LOG2E = 1.4426950408889634

LANES = 128
SUBLANES = 8
HEAD_PAD = 128
ZC_COL, ZS_COL, ZP_COL, CQ_COL, CKV_COL, KR_COL = 0, 512, 1024, 1280, 1536, 1664
Z_SMALL = 1792
POOL_HALO = 16
CONV_HALO = 32

TM = 512
TQ = 512
TK = 512
TR = 512
BLK = 512
ROW_UNROLL = 8
TMG = 512
TF_G = 1792
INFO_W = 8
VMEM_LIMIT = 56 * 1024 * 1024


def _params(sem):
    return pltpu.CompilerParams(dimension_semantics=sem, vmem_limit_bytes=VMEM_LIMIT)


def _rms(x, g):
    return x * lax.rsqrt(jnp.mean(x * x, axis=-1, keepdims=True) + EPS) * g


def _layernorm(x, g, b):
    mu = jnp.mean(x, axis=-1, keepdims=True)
    xc = x - mu
    var = jnp.mean(xc * xc, axis=-1, keepdims=True)
    return xc * lax.rsqrt(var + EPS) * g + b


def _sigmoid(x):
    return 0.5 * jnp.tanh(0.5 * x) + 0.5


def _dot(a, b):
    return jnp.dot(a, b, preferred_element_type=F32)


def _in_proj_kernel(x_ref, g_ref, ws_ref, zs_ref):
    h = _rms(x_ref[...], g_ref[...]).astype(BF16)
    for n0 in range(0, Z_SMALL, 896):
        zs_ref[:, n0:n0 + 896] = _dot(h, ws_ref[:, n0:n0 + 896])


def _in_proj(x, g, w_small):
    t = x.shape[0]
    return pl.pallas_call(
        _in_proj_kernel,
        grid=(t // TM,),
        in_specs=[
            pl.BlockSpec((TM, D_MODEL), lambda i: (i, 0)),
            pl.BlockSpec((1, D_MODEL), lambda i: (0, 0)),
            pl.BlockSpec((D_MODEL, Z_SMALL), lambda i: (0, 0)),
        ],
        out_specs=pl.BlockSpec((TM, Z_SMALL), lambda i: (i, 0)),
        out_shape=jax.ShapeDtypeStruct((t, Z_SMALL), F32),
        compiler_params=_params(("parallel",)),
        name="in_proj",
    )(x, g, w_small)


def _pool_kernel(z_ref, halo_ref, w_ref, scale_ref, o_ref, buf_ref):
    i = pl.program_id(0)
    pos0 = (i * TM) % SEQ
    z = z_ref[...]
    buf_ref[0:POOL_HALO, :] = jnp.where(pos0 == 0, 0.0, halo_ref[...])
    buf_ref[POOL_HALO:, :] = z
    lane = lax.broadcasted_iota(jnp.int32, (TM, POOL_WIDTH), 1)
    pos = lax.broadcasted_iota(jnp.int32, (TM, POOL_WIDTH), 0) + pos0
    group = lane // (POOL_WIDTH // len(POOL_WINDOWS))
    acc = z
    total = jnp.zeros_like(z)
    win = jnp.zeros_like(lane)
    prev_w = 1
    for gi, w in enumerate(POOL_WINDOWS):
        for j in range(prev_w, w):
            acc = acc + buf_ref[POOL_HALO - j:POOL_HALO - j + TM, :]
        prev_w = w
        total = jnp.where(group == gi, acc, total)
        win = jnp.where(group == gi, w, win)
    cnt = jnp.minimum(pos + 1, win).astype(F32)
    pooled = total / cnt - z
    y = _dot(pooled.astype(BF16), w_ref[...]) * scale_ref[...]
    o_ref[...] = y.astype(BF16)


def _pool(zs, w_bd, scale):
    t = zs.shape[0]
    cb = ZP_COL // POOL_WIDTH
    r = TM // POOL_HALO
    return pl.pallas_call(
        _pool_kernel,
        grid=(t // TM,),
        in_specs=[
            pl.BlockSpec((TM, POOL_WIDTH), lambda i: (i, cb)),
            pl.BlockSpec((POOL_HALO, POOL_WIDTH), lambda i: (jnp.maximum(i * r - 1, 0), cb)),
            pl.BlockSpec((POOL_WIDTH, POOL_WIDTH), lambda i: (0, 0)),
            pl.BlockSpec((1, POOL_WIDTH), lambda i: (0, 0)),
        ],
        out_specs=pl.BlockSpec((TM, POOL_WIDTH), lambda i: (i, 0)),
        out_shape=jax.ShapeDtypeStruct((t, POOL_WIDTH), BF16),
        scratch_shapes=[pltpu.VMEM((TM + POOL_HALO, POOL_WIDTH), F32)],
        compiler_params=_params(("parallel",)),
        name="pool_mixer",
    )(zs, zs, w_bd, scale)


def _glu(z2):
    return z2[:, :CONV_WIDTH] * _sigmoid(z2[:, CONV_WIDTH:])


def _conv_kernel(z_ref, halo_ref, w_ref, b_ref, lg_ref, lb_ref, o_ref, buf_ref, sh_ref):
    i = pl.program_id(0)
    pos0 = (i * TM) % SEQ
    buf_ref[0:CONV_HALO, :] = jnp.where(pos0 == 0, 0.0, _glu(halo_ref[...]))
    buf_ref[CONV_HALO:, :] = _glu(z_ref[...])
    span = TM + CONV_HALO - SUBLANES
    for ph in range(1, SUBLANES):
        sh_ref[ph - 1, 0:span, :] = buf_ref[ph:ph + span, :]
    off = CONV_HALO - (CONV_K - 1)
    y = jnp.zeros((TM, CONV_WIDTH), F32) + b_ref[...]
    for k in range(CONV_K):
        ph = (off + k) % SUBLANES
        r0 = off + k - ph
        tap = buf_ref[r0:r0 + TM, :] if ph == 0 else sh_ref[ph - 1, r0:r0 + TM, :]
        y = y + w_ref[k:k + 1, :] * tap
    yn = _layernorm(y, lg_ref[...], lb_ref[...])
    o_ref[...] = (yn * _sigmoid(yn)).astype(BF16)


def _conv(zs, conv_w, conv_b, ln_g, ln_b):
    t = zs.shape[0]
    cb = ZC_COL // (2 * CONV_WIDTH)
    r = TM // CONV_HALO
    vec = pl.BlockSpec((1, CONV_WIDTH), lambda i: (0, 0))
    return pl.pallas_call(
        _conv_kernel,
        grid=(t // TM,),
        in_specs=[
            pl.BlockSpec((TM, 2 * CONV_WIDTH), lambda i: (i, cb)),
            pl.BlockSpec((CONV_HALO, 2 * CONV_WIDTH), lambda i: (jnp.maximum(i * r - 1, 0), cb)),
            pl.BlockSpec((CONV_K, CONV_WIDTH), lambda i: (0, 0)),
            vec, vec, vec,
        ],
        out_specs=pl.BlockSpec((TM, CONV_WIDTH), lambda i: (i, 0)),
        out_shape=jax.ShapeDtypeStruct((t, CONV_WIDTH), BF16),
        scratch_shapes=[pltpu.VMEM((TM + CONV_HALO, CONV_WIDTH), F32),
                        pltpu.VMEM((SUBLANES - 1, TM + CONV_HALO, CONV_WIDTH), F32)],
        compiler_params=_params(("parallel",)),
        name="conv_mixer",
    )(zs, zs, conv_w, conv_b, ln_g, ln_b)


def _sgu_kernel(z_ref, lg_ref, lb_ref, ws_ref, bias_ref, o_ref):
    z = z_ref[...]
    z = 0.5 * z * (1.0 + lax.erf(z * (2.0 ** -0.5)))
    u = z[:, :SGU_WIDTH]
    v = _layernorm(z[:, SGU_WIDTH:], lg_ref[...], lb_ref[...]).astype(BF16)
    rows = SGU_GROUPS * SGU_LEN
    row = lax.broadcasted_iota(jnp.int32, (rows, SGU_LEN), 0) % SGU_LEN
    col = lax.broadcasted_iota(jnp.int32, (rows, SGU_LEN), 1)
    ws = jnp.where(col <= row, ws_ref[...], 0.0).astype(BF16)
    lane_group = lax.broadcasted_iota(jnp.int32, (SGU_LEN, SGU_WIDTH), 1) // (SGU_WIDTH // SGU_GROUPS)
    for blk in range(TM // SGU_LEN):
        r0 = blk * SGU_LEN
        full = _dot(ws, v[r0:r0 + SGU_LEN, :])
        mixed = full[0:SGU_LEN, :]
        for g in range(1, SGU_GROUPS):
            mixed = jnp.where(lane_group == g, full[g * SGU_LEN:(g + 1) * SGU_LEN, :], mixed)
        mixed = mixed + bias_ref[...]
        o_ref[r0:r0 + SGU_LEN, :] = (u[r0:r0 + SGU_LEN, :] * mixed).astype(BF16)


def _sgu(zs, ln_g, ln_b, ws_stack, bias_full):
    t = zs.shape[0]
    cb = ZS_COL // (2 * SGU_WIDTH)
    vec = pl.BlockSpec((1, SGU_WIDTH), lambda i: (0, 0))
    return pl.pallas_call(
        _sgu_kernel,
        grid=(t // TM,),
        in_specs=[
            pl.BlockSpec((TM, 2 * SGU_WIDTH), lambda i: (i, cb)),
            vec, vec,
            pl.BlockSpec((SGU_GROUPS * SGU_LEN, SGU_LEN), lambda i: (0, 0)),
            pl.BlockSpec((SGU_LEN, SGU_WIDTH), lambda i: (0, 0)),
        ],
        out_specs=pl.BlockSpec((TM, SGU_WIDTH), lambda i: (i, 0)),
        out_shape=jax.ShapeDtypeStruct((t, SGU_WIDTH), BF16),
        compiler_params=_params(("parallel",)),
        name="sgu_mixer",
    )(zs, ln_g, ln_b, ws_stack, bias_full)


def _tile_heads(tab):
    return jnp.concatenate([tab] * MLA_HEADS, axis=1)


def _mla_prep_kernel(cq_ref, ckv_ref, kr_ref, qg_ref, kvg_ref, wqa_ref, wqb_ref, wk_ref, wv_ref,
                     place_ref, cq_tab_ref, sq_tab_ref, ck_tab_ref, sk_tab_ref,
                     qt_ref, k_ref, vt_ref):
    cqn = _rms(cq_ref[...], qg_ref[...]).astype(BF16)
    qa = _dot(cqn, wqa_ref[...])
    qb = _dot(cqn, wqb_ref[...])
    scale = (QK_NOPE + QK_ROPE) ** -0.5 * LOG2E
    q = (qa * _tile_heads(cq_tab_ref[...]) + qb * _tile_heads(sq_tab_ref[...])) * scale
    qt_ref[...] = q.T.astype(BF16)
    kvn = _rms(ckv_ref[...], kvg_ref[...]).astype(BF16)
    vt = _dot(kvn, wv_ref[...]).T
    ones_row = lax.broadcasted_iota(jnp.int32, vt.shape, 0) % V_ROWS == V_HEAD
    vt = jnp.where(ones_row, 1.0, vt)
    for c in range(TM // TK):
        vt_ref[c] = vt[:, c * TK:(c + 1) * TK].astype(BF16)
    kr = kr_ref[...]
    kro = kr * ck_tab_ref[...] + pltpu.roll(kr, LANES - QK_ROPE, 1) * sk_tab_ref[...]
    k = _dot(kvn, wk_ref[...]) + _dot(kro.astype(BF16), place_ref[...])
    k_ref[...] = k.astype(BF16)


def _mla_prep(zs, q_g, kv_g, wqa, wqb, wk, wv, place, tabs):
    t = zs.shape[0]
    hw = MLA_HEADS * HEAD_PAD
    hv = MLA_HEADS * V_ROWS
    ns = SEQ // TM
    tab = pl.BlockSpec((TM, LANES), lambda i: (i % ns, 0))
    full = lambda a: pl.BlockSpec(a.shape, lambda i: (0, 0))
    return pl.pallas_call(
        _mla_prep_kernel,
        grid=(t // TM,),
        in_specs=[
            pl.BlockSpec((TM, Q_LORA), lambda i: (i, CQ_COL // Q_LORA)),
            pl.BlockSpec((TM, KV_LORA), lambda i: (i, CKV_COL // KV_LORA)),
            pl.BlockSpec((TM, LANES), lambda i: (i, KR_COL // LANES)),
            full(q_g), full(kv_g), full(wqa), full(wqb), full(wk), full(wv), full(place),
            tab, tab, tab, tab,
        ],
        out_specs=[
            pl.BlockSpec((hw, TM), lambda i: (0, i)),
            pl.BlockSpec((TM, hw), lambda i: (i, 0)),
            pl.BlockSpec((TM // TK, hv, TK), lambda i: (i, 0, 0)),
        ],
        out_shape=[
            jax.ShapeDtypeStruct((hw, t), BF16),
            jax.ShapeDtypeStruct((t, hw), BF16),
            jax.ShapeDtypeStruct((t // TK, hv, TK), BF16),
        ],
        compiler_params=_params(("parallel",)),
        name="mla_prep",
    )(zs, zs, zs, q_g, kv_g, wqa, wqb, wk, wv, place, *tabs)


def _attn_kernel(qt_ref, k_ref, vt_ref, o_ref, m_ref, acc_ref, st_ref):
    qi = pl.program_id(1)
    m_ref[...] = jnp.full(m_ref.shape, NEG, F32)
    acc_ref[...] = jnp.zeros(acc_ref.shape, F32)

    def step(j, masked):
        ks = pl.multiple_of(j * TK, TK)
        tile_max = []
        for h in range(MLA_HEADS):
            hs = slice(h * HEAD_PAD, (h + 1) * HEAD_PAD)
            st = _dot(k_ref[pl.ds(ks, TK), hs], qt_ref[hs, :])
            if masked:
                key_chunk = lax.broadcasted_iota(jnp.int32, (TK, TQ), 0) // CHUNK
                qry_chunk = lax.broadcasted_iota(jnp.int32, (TK, TQ), 1) // CHUNK
                st = jnp.where(key_chunk <= qry_chunk, st, NEG)
            st_ref[h] = st
            tile_max.append(jnp.max(st, axis=0, keepdims=True))
        for h in range(MLA_HEADS):
            m_prev = m_ref[h]
            m_new = jnp.maximum(m_prev, tile_max[h])
            alpha = jnp.exp2(m_prev - m_new)
            p = jnp.exp2(st_ref[h] - m_new).astype(BF16)
            m_ref[h] = m_new
            rows = slice(h * V_ROWS, (h + 1) * V_ROWS)
            acc_ref[rows, :] = alpha * acc_ref[rows, :] + _dot(vt_ref[j, rows, :], p)

    def body(j, carry):
        step(j, False)
        return carry

    lax.fori_loop(0, qi, body, 0)
    step(qi, True)
    outs = [acc_ref[h * V_ROWS:h * V_ROWS + V_HEAD, :] / acc_ref[h * V_ROWS + V_HEAD:h * V_ROWS + V_HEAD + 1, :]
            for h in range(MLA_HEADS)]
    o_ref[...] = jnp.concatenate(outs, axis=0).T.astype(BF16)


def _attention(qt, k, vt, batch):
    t = k.shape[0]
    nq = SEQ // TQ
    nk = SEQ // TK
    hw = MLA_HEADS * HEAD_PAD
    hv = MLA_HEADS * V_HEAD
    hr = MLA_HEADS * V_ROWS
    once = pl.Buffered(1)
    return pl.pallas_call(
        _attn_kernel,
        grid=(batch, nq),
        in_specs=[
            pl.BlockSpec((hw, TQ), lambda b, i: (0, b * nq + i)),
            pl.BlockSpec((SEQ, hw), lambda b, i: (b, 0), pipeline_mode=once),
            pl.BlockSpec((nk, hr, TK), lambda b, i: (b, 0, 0), pipeline_mode=once),
        ],
        out_specs=pl.BlockSpec((TQ, hv), lambda b, i: (b * nq + i, 0)),
        out_shape=jax.ShapeDtypeStruct((t, hv), BF16),
        scratch_shapes=[
            pltpu.VMEM((MLA_HEADS, 1, TQ), F32),
            pltpu.VMEM((hr, TQ), F32),
            pltpu.VMEM((MLA_HEADS, TK, TQ), F32),
        ],
        compiler_params=_params(("parallel", "arbitrary")),
        name="mla_attention",
    )(qt, k, vt)


def _merge_kernel(x_ref, ng_ref, mp_ref, ma_ref, mc_ref, ms_ref, wgate_ref, bg_ref,
                  wp_ref, wa_ref, wc_ref, ws_ref, wo_ref, o_ref, mg_ref):
    x = x_ref[...]
    h = _rms(x, ng_ref[...]).astype(BF16)
    branches = ((mp_ref, wp_ref), (ma_ref, wa_ref), (mc_ref, wc_ref), (ms_ref, ws_ref))
    half = D_MODEL // 2
    for n0 in range(0, D_MODEL, half):
        merged = jnp.zeros((TM, half), F32)
        for b, (m_ref, w_ref) in enumerate(branches):
            c0 = b * D_MODEL + n0
            gate = _sigmoid(_dot(h, wgate_ref[:, c0:c0 + half]) + bg_ref[:, c0:c0 + half])
            merged = merged + gate * _dot(m_ref[...], w_ref[:, n0:n0 + half])
        mg_ref[:, n0:n0 + half] = merged.astype(BF16)
    o_ref[...] = x + _dot(mg_ref[...], wo_ref[...])


def _merge(x, norm_g, mp, ma, mc, ms, w_gate, b_gate, wp, wa, wc, ws, wo):
    t = x.shape[0]
    row = lambda w: pl.BlockSpec((TM, w), lambda i: (i, 0))
    full = lambda a: pl.BlockSpec(a.shape, lambda i: (0, 0))
    return pl.pallas_call(
        _merge_kernel,
        grid=(t // TM,),
        in_specs=[row(D_MODEL), full(norm_g), row(POOL_WIDTH), row(MLA_HEADS * V_HEAD), row(CONV_WIDTH),
                  row(SGU_WIDTH), full(w_gate), full(b_gate), full(wp), full(wa), full(wc), full(ws), full(wo)],
        out_specs=row(D_MODEL),
        out_shape=jax.ShapeDtypeStruct((t, D_MODEL), F32),
        scratch_shapes=[pltpu.VMEM((TM, D_MODEL), BF16)],
        compiler_params=_params(("parallel",)),
        name="merge_out",
    )(x, norm_g, mp, ma, mc, ms, w_gate, b_gate, wp, wa, wc, ws, wo)


def _ffn_kernel(x_ref, g_ref, wg_ref, wu_ref, wd_ref, o_ref):
    x = x_ref[...]
    h = _rms(x, g_ref[...]).astype(BF16)
    half = D_FF // 2
    acc = x
    for f0 in range(0, D_FF, half):
        gate = _dot(h, wg_ref[:, f0:f0 + half])
        up = _dot(h, wu_ref[:, f0:f0 + half])
        act = (gate * _sigmoid(gate) * up).astype(BF16)
        acc = acc + _dot(act, wd_ref[f0:f0 + half, :])
    o_ref[...] = acc


def _ffn(x, g, wg, wu, wd):
    t = x.shape[0]
    full = lambda a: pl.BlockSpec(a.shape, lambda i: (0, 0))
    return pl.pallas_call(
        _ffn_kernel,
        grid=(t // TM,),
        in_specs=[pl.BlockSpec((TM, D_MODEL), lambda i: (i, 0)), full(g), full(wg), full(wu), full(wd)],
        out_specs=pl.BlockSpec((TM, D_MODEL), lambda i: (i, 0)),
        out_shape=jax.ShapeDtypeStruct((t, D_MODEL), F32),
        compiler_params=_params(("parallel",)),
        name="ffn_dense",
    )(x, g, wg, wu, wd)


def _router_kernel(x_ref, g_ref, wr_ref, br_ref, h_ref, info_ref, cnt_ref, carry_ref):
    @pl.when(pl.program_id(0) == 0)
    def _():
        carry_ref[...] = jnp.zeros(carry_ref.shape, F32)

    h = _rms(x_ref[...], g_ref[...])
    h_ref[...] = h
    logits = jnp.dot(h, wr_ref[...], preferred_element_type=F32,
                     precision=lax.Precision.HIGHEST) + br_ref[...]
    lane = lax.broadcasted_iota(jnp.int32, logits.shape, 1).astype(F32)
    logits = jnp.where(lane < N_EXPERTS, logits, NEG)
    v1 = jnp.max(logits, axis=1, keepdims=True)
    i1 = jnp.min(jnp.where(logits == v1, lane, float(LANES)), axis=1, keepdims=True)
    rest = jnp.where(lane == i1, NEG, logits)
    v2 = jnp.max(rest, axis=1, keepdims=True)
    i2 = jnp.min(jnp.where(rest == v2, lane, float(LANES)), axis=1, keepdims=True)
    e = jnp.exp(v2 - v1)
    w1 = 1.0 / (1.0 + e)
    w2 = e / (1.0 + e)
    chosen = jnp.where((lane == i1) | (lane == i2), 1.0, 0.0)
    row = lax.broadcasted_iota(jnp.int32, (TR, TR), 0)
    col = lax.broadcasted_iota(jnp.int32, (TR, TR), 1)
    before = jnp.where(col < row, 1.0, 0.0)
    pos = _dot(before, chosen) + carry_ref[0:1, :]
    r1 = jnp.sum(jnp.where(lane == i1, pos, 0.0), axis=1, keepdims=True)
    r2 = jnp.sum(jnp.where(lane == i2, pos, 0.0), axis=1, keepdims=True)
    carry_ref[0:1, :] = carry_ref[0:1, :] + jnp.sum(chosen, axis=0, keepdims=True)
    cnt_ref[...] = carry_ref[...]
    fields = (i1, i2, r1, r2, w1, w2)
    info = jnp.zeros(logits.shape, F32)
    for n, val in enumerate(fields):
        info = jnp.where(lane == n, val, info)
    info_ref[...] = info[:, :INFO_W]


def _router(x, g, wr, br):
    t = x.shape[0]
    full = lambda a: pl.BlockSpec(a.shape, lambda i: (0, 0))
    return pl.pallas_call(
        _router_kernel,
        grid=(t // TR,),
        in_specs=[pl.BlockSpec((TR, D_MODEL), lambda i: (i, 0)), full(g), full(wr), full(br)],
        out_specs=[
            pl.BlockSpec((TR, D_MODEL), lambda i: (i, 0)),
            pl.BlockSpec((TR, INFO_W), lambda i: (i, 0)),
            pl.BlockSpec((8, LANES), lambda i: (0, 0)),
        ],
        out_shape=[
            jax.ShapeDtypeStruct((t, D_MODEL), F32),
            jax.ShapeDtypeStruct((t, INFO_W), F32),
            jax.ShapeDtypeStruct((8, LANES), F32),
        ],
        scratch_shapes=[pltpu.VMEM((8, LANES), F32)],
        compiler_params=_params(("arbitrary",)),
        name="moe_router",
    )(x, g, wr, br)


def _num_row_tiles(t):
    return (2 * t) // TMG + N_EXPERTS


def _route_plan(info, cnt, t):
    counts = cnt[0, :N_EXPERTS].astype(jnp.int32)
    padded = ((counts + TMG - 1) // TMG) * TMG
    ends = jnp.cumsum(padded)
    off = ends - padded
    slot = off[info[:, 0:2].astype(jnp.int32)] + info[:, 2:4].astype(jnp.int32)
    slots = slot.reshape(t // BLK, BLK, 2).transpose(0, 2, 1).reshape(-1)
    starts = jnp.arange(_num_row_tiles(t), dtype=jnp.int32) * TMG
    first_row = jnp.minimum(starts, ends[-1] - 1)
    tile_expert = jnp.sum(ends[None, :] <= first_row[:, None], axis=1)
    n_tiles = (ends[-1] // TMG).reshape(1)
    return slots.astype(jnp.int32), tile_expert.astype(jnp.int32), n_tiles.astype(jnp.int32)


def _row_copy(src_ref, src_row, dst_ref, dst_row, sem):
    return pltpu.make_async_copy(src_ref.at[pl.ds(src_row, 1), :], dst_ref.at[pl.ds(dst_row, 1), :], sem)


def _dispatch_kernel(slots_ref, h_ref, xs_in_ref, xs_ref, sem):
    del xs_in_ref
    base = pl.program_id(0) * (2 * BLK)

    def body(r, carry):
        for j in range(2):
            _row_copy(h_ref, r, xs_ref, slots_ref[base + j * BLK + r], sem).start()
        return carry

    lax.fori_loop(0, BLK, body, 0, unroll=ROW_UNROLL)
    for j in range(2):
        pltpu.make_async_copy(h_ref, xs_ref.at[pl.ds(0, BLK), :], sem).wait()


def _dispatch(slots, h, n_rows):
    t = h.shape[0]
    return pl.pallas_call(
        _dispatch_kernel,
        grid_spec=pltpu.PrefetchScalarGridSpec(
            num_scalar_prefetch=1,
            grid=(t // BLK,),
            in_specs=[pl.BlockSpec((BLK, D_MODEL), lambda i, s: (i, 0)),
                      pl.BlockSpec(memory_space=pl.ANY)],
            out_specs=pl.BlockSpec(memory_space=pl.ANY),
            scratch_shapes=[pltpu.SemaphoreType.DMA(())],
        ),
        out_shape=jax.ShapeDtypeStruct((n_rows, D_MODEL), F32),
        input_output_aliases={2: 0},
        compiler_params=_params(("arbitrary",)),
        name="moe_dispatch",
    )(slots, h, jnp.zeros((n_rows, D_MODEL), F32))


def _experts_kernel(te_ref, nt_ref, xs_ref, wg_ref, wu_ref, wd_ref, y_ref, xb_ref, acc_ref):
    del te_ref, nt_ref
    f = pl.program_id(1)

    @pl.when(f == 0)
    def _():
        xb_ref[...] = xs_ref[...].astype(BF16)
        acc_ref[...] = jnp.zeros(acc_ref.shape, F32)

    h = xb_ref[...]
    part = jnp.zeros((TMG, D_MODEL), F32)
    for c0 in range(0, TF_G, TF_G // 2):
        cs = slice(c0, c0 + TF_G // 2)
        gate = _dot(h, wg_ref[:, cs])
        up = _dot(h, wu_ref[:, cs])
        part = part + _dot((gate * _sigmoid(gate) * up).astype(BF16), wd_ref[cs, :])
    acc_ref[...] += part

    @pl.when(f == pl.num_programs(1) - 1)
    def _():
        y_ref[...] = acc_ref[...]


def _experts(tile_expert, n_tiles, xs, wg, wu, wd):
    n_rows = xs.shape[0]
    nf = D_FF_EXPERT // TF_G

    def f_eff(g, f, nt):
        return jnp.where(g < nt[0], f, nf - 1)

    return pl.pallas_call(
        _experts_kernel,
        grid_spec=pltpu.PrefetchScalarGridSpec(
            num_scalar_prefetch=2,
            grid=(n_rows // TMG, nf),
            in_specs=[
                pl.BlockSpec((TMG, D_MODEL), lambda g, f, te, nt: (g, 0)),
                pl.BlockSpec((None, D_MODEL, TF_G), lambda g, f, te, nt: (te[g], 0, f_eff(g, f, nt))),
                pl.BlockSpec((None, D_MODEL, TF_G), lambda g, f, te, nt: (te[g], 0, f_eff(g, f, nt))),
                pl.BlockSpec((None, TF_G, D_MODEL), lambda g, f, te, nt: (te[g], f_eff(g, f, nt), 0)),
            ],
            out_specs=pl.BlockSpec((TMG, D_MODEL), lambda g, f, te, nt: (g, 0)),
            scratch_shapes=[pltpu.VMEM((TMG, D_MODEL), BF16), pltpu.VMEM((TMG, D_MODEL), F32)],
        ),
        out_shape=jax.ShapeDtypeStruct((n_rows, D_MODEL), F32),
        compiler_params=_params(("arbitrary", "arbitrary")),
        name="moe_experts",
    )(tile_expert, n_tiles, xs, wg, wu, wd)


def _combine_kernel(slots_ref, x_ref, info_ref, fg_ref, y_ref, o_ref, rows_ref, sem):
    base = pl.program_id(0) * (2 * BLK)

    def body(r, carry):
        for j in range(2):
            _row_copy(y_ref, slots_ref[base + j * BLK + r], rows_ref.at[j], r, sem).start()
        return carry

    lax.fori_loop(0, BLK, body, 0, unroll=ROW_UNROLL)
    for j in range(2):
        pltpu.make_async_copy(y_ref.at[pl.ds(0, BLK), :], rows_ref.at[j], sem).wait()
    w1 = info_ref[:, 4:5]
    w2 = info_ref[:, 5:6]
    o_ref[...] = _rms(x_ref[...] + w1 * rows_ref[0] + w2 * rows_ref[1], fg_ref[...])


def _combine(slots, x, info, final_g, y):
    t = x.shape[0]
    return pl.pallas_call(
        _combine_kernel,
        grid_spec=pltpu.PrefetchScalarGridSpec(
            num_scalar_prefetch=1,
            grid=(t // BLK,),
            in_specs=[pl.BlockSpec((BLK, D_MODEL), lambda i, s: (i, 0)),
                      pl.BlockSpec((BLK, INFO_W), lambda i, s: (i, 0)),
                      pl.BlockSpec((1, D_MODEL), lambda i, s: (0, 0)),
                      pl.BlockSpec(memory_space=pl.ANY)],
            out_specs=pl.BlockSpec((BLK, D_MODEL), lambda i, s: (i, 0)),
            scratch_shapes=[pltpu.VMEM((2, BLK, D_MODEL), F32), pltpu.SemaphoreType.DMA(())],
        ),
        out_shape=jax.ShapeDtypeStruct((t, D_MODEL), F32),
        compiler_params=_params(("arbitrary",)),
        name="moe_combine",
    )(slots, x, info, final_g, y)


def _rotate_half_cols(w):
    half = w.shape[-1] // 2
    return jnp.concatenate([-w[..., half:], w[..., :half]], axis=-1)


def _rope_tables():
    pos = jnp.arange(SEQ, dtype=F32)
    inv = ROPE_THETA ** (-jnp.arange(0, QK_ROPE, 2, dtype=F32) / QK_ROPE)
    ang = pos[:, None] * inv[None, :]
    cos, sin = jnp.cos(ang), jnp.sin(ang)
    one = jnp.ones((SEQ, QK_NOPE), F32)
    z_nope = jnp.zeros((SEQ, QK_NOPE), F32)
    z_pad = jnp.zeros((SEQ, HEAD_PAD - QK_NOPE - QK_ROPE), F32)
    z_rest = jnp.zeros((SEQ, LANES - QK_ROPE), F32)
    cq_tab = jnp.concatenate([one, cos, cos, z_pad], axis=1)
    sq_tab = jnp.concatenate([z_nope, sin, sin, z_pad], axis=1)
    ck_tab = jnp.concatenate([cos, cos, z_rest], axis=1)
    sk_tab = jnp.concatenate([sin, sin, z_rest], axis=1)
    return cq_tab, sq_tab, ck_tab, sk_tab


def _rope_placement():
    r = jnp.arange(QK_ROPE)
    place = jnp.zeros((LANES, MLA_HEADS * HEAD_PAD), F32)
    for h in range(MLA_HEADS):
        place = place.at[r, h * HEAD_PAD + QK_NOPE + r].set(1.0)
    return place.astype(BF16)


def _split_in_proj(w):
    zp, cq, ckv, kr, zc, zs, zg = jnp.split(
        w, (256, 512, 640, 672, 1184, 1696), axis=1)
    pad = jnp.zeros((D_MODEL, LANES - 2 * QK_ROPE), w.dtype)
    small = jnp.concatenate([zc, zs, zp, cq, ckv, kr, _rotate_half_cols(kr), pad], axis=1)
    return small.astype(BF16), zg.astype(BF16)


def _mla_weights(w_uq, w_ukv):
    wq = w_uq.reshape(Q_LORA, MLA_HEADS, QK_NOPE + QK_ROPE)
    nope, rope = wq[..., :QK_NOPE], wq[..., QK_NOPE:]
    zq = jnp.zeros((Q_LORA, MLA_HEADS, HEAD_PAD - QK_NOPE - QK_ROPE), w_uq.dtype)
    wqa = jnp.concatenate([nope, rope, zq], axis=-1).reshape(Q_LORA, MLA_HEADS * HEAD_PAD)
    wqb = jnp.concatenate([jnp.zeros_like(nope), _rotate_half_cols(rope), zq], axis=-1)
    wqb = wqb.reshape(Q_LORA, MLA_HEADS * HEAD_PAD)
    wkv = w_ukv.reshape(KV_LORA, MLA_HEADS, QK_NOPE + V_HEAD)
    k_nope, v = wkv[..., :QK_NOPE], wkv[..., QK_NOPE:]
    zk = jnp.zeros((KV_LORA, MLA_HEADS, HEAD_PAD - QK_NOPE), w_ukv.dtype)
    wk = jnp.concatenate([k_nope, zk], axis=-1).reshape(KV_LORA, MLA_HEADS * HEAD_PAD)
    zv = jnp.zeros((KV_LORA, MLA_HEADS, V_ROWS - V_HEAD), w_ukv.dtype)
    wv = jnp.concatenate([v, zv], axis=-1).reshape(KV_LORA, MLA_HEADS * V_ROWS)
    return wqa.astype(BF16), wqb.astype(BF16), wk.astype(BF16), wv.astype(BF16)


def kernel(x, norm_mix_g, w_in, b_gate, w_pool, pool_scale, w_pool_out, q_norm_g, w_uq, kv_norm_g, w_ukv, w_mla_out, conv_w, conv_b, conv_ln_g, conv_ln_b, w_conv_out, sgu_ln_g, sgu_ln_b, sgu_w, sgu_b, w_sgu_out, w_o, norm_ffn_g, w_ffn_gate, w_ffn_up, w_ffn_down, w_router, b_router, w_moe_gate, w_moe_up, w_moe_down, final_norm_g):
    batch, seq, d = x.shape
    assert (seq, d) == (SEQ, D_MODEL)
    depth = w_in.shape[0]
    xt = x.reshape(batch * seq, d)
    tabs = _rope_tables()
    place = _rope_placement()
    row = lambda a: a.reshape(1, -1)

    for l in range(depth):
        w_small, w_gate = _split_in_proj(w_in[l])
        zs = _in_proj(xt, row(norm_mix_g[l]), w_small)

        w_bd = jax.scipy.linalg.block_diag(*[w_pool[l, g] for g in range(w_pool.shape[1])])
        m_pool = _pool(zs, w_bd.astype(BF16), row(pool_scale[l]))
        m_conv = _conv(zs, conv_w[l], row(conv_b[l]), row(conv_ln_g[l]), row(conv_ln_b[l]))
        bias_full = jnp.repeat(sgu_b[l].T, SGU_WIDTH // SGU_GROUPS, axis=1)
        m_sgu = _sgu(zs, row(sgu_ln_g[l]), row(sgu_ln_b[l]),
                     sgu_w[l].reshape(SGU_GROUPS * SGU_LEN, SGU_LEN), bias_full)
        wqa, wqb, wk, wv = _mla_weights(w_uq[l], w_ukv[l])
        q, k, v = _mla_prep(zs, row(q_norm_g[l]), row(kv_norm_g[l]), wqa, wqb, wk, wv, place, tabs)
        m_mla = _attention(q, k, v, batch)

        xt = _merge(xt, row(norm_mix_g[l]), m_pool, m_mla, m_conv, m_sgu, w_gate, row(b_gate[l]),
                    w_pool_out[l].astype(BF16), w_mla_out[l].astype(BF16),
                    w_conv_out[l].astype(BF16), w_sgu_out[l].astype(BF16), w_o[l].astype(BF16))

        j = l // 2
        if l % 2 == 0:
            xt = _ffn(xt, row(norm_ffn_g[l]), w_ffn_gate[j].astype(BF16),
                      w_ffn_up[j].astype(BF16), w_ffn_down[j].astype(BF16))
        else:
            wr = jnp.pad(w_router[j], ((0, 0), (0, LANES - N_EXPERTS)))
            br = jnp.pad(b_router[j], (0, LANES - N_EXPERTS)).reshape(1, LANES)
            t = xt.shape[0]
            h, info, cnt = _router(xt, row(norm_ffn_g[l]), wr, br)
            slots, tile_expert, n_tiles = _route_plan(info, cnt, t)
            xs = _dispatch(slots, h, _num_row_tiles(t) * TMG)
            y = _experts(tile_expert, n_tiles, xs, w_moe_gate[j].astype(BF16),
                         w_moe_up[j].astype(BF16), w_moe_down[j].astype(BF16))
            assert l == depth - 1
            xt = _combine(slots, xt, info, row(final_norm_g), y)
    return xt.reshape(batch, seq, d)
```

```python
import functools

import jax
import jax.numpy as jnp
from jax import lax
from jax.experimental import pallas as pl
from jax.experimental.pallas import tpu as pltpu

F32 = jnp.float32
BF16 = jnp.bfloat16

D_MODEL = 1024
SEQ = 8192
CHUNK = 64
POOL_WIDTH = 256
POOL_WINDOWS = (2, 4, 8, 16)
MLA_HEADS = 8
Q_LORA = 256
KV_LORA = 128
QK_NOPE = 64
QK_ROPE = 32
V_HEAD = 64
V_ROWS = 80
ROPE_THETA = 10000.0
CONV_WIDTH = 256
CONV_K = 31
SGU_WIDTH = 256
SGU_GROUPS = 4
SGU_LEN = 128
N_BRANCH = 4
D_FF = 2816
N_EXPERTS = 8
D_FF_EXPERT = 3584
EPS = 1e-6
NEG = -1e30
LOG2E = 1.4426950408889634

LANES = 128
SUBLANES = 8
MXU_TILE = 256
HEAD_PAD = 128
ZC_COL, ZS_COL, ZP_COL, CQ_COL, CKV_COL, KR_COL = 0, 512, 1024, 1280, 1536, 1664
Z_SMALL = 1792
POOL_HALO = 16
CONV_HALO = 32

TM = 512
TQ = 512
TK = 512
TR = 512
BLK = 512
ROW_UNROLL = 8
TMG = 512
TF_G = 1792
INFO_W = 8
VMEM_LIMIT = 56 * 1024 * 1024


def _params(sem):
    return pltpu.CompilerParams(dimension_semantics=sem, vmem_limit_bytes=VMEM_LIMIT)


def _rms(x, g):
    return x * lax.rsqrt(jnp.mean(x * x, axis=-1, keepdims=True) + EPS) * g


def _layernorm(x, g, b):
    mu = jnp.mean(x, axis=-1, keepdims=True)
    xc = x - mu
    var = jnp.mean(xc * xc, axis=-1, keepdims=True)
    return xc * lax.rsqrt(var + EPS) * g + b


def _sigmoid(x):
    return 0.5 * jnp.tanh(0.5 * x) + 0.5


def _dot(a, b):
    return jnp.dot(a, b, preferred_element_type=F32)


def _mxu_chunks(total, max_chunk):
    assert total % MXU_TILE == 0 and max_chunk % MXU_TILE == 0
    return [slice(s, min(s + max_chunk, total)) for s in range(0, total, max_chunk)]


def _in_proj_kernel(x_ref, g_ref, ws_ref, zs_ref):
    h = _rms(x_ref[...], g_ref[...]).astype(BF16)
    for cs in _mxu_chunks(Z_SMALL, 1024):
        zs_ref[:, cs] = _dot(h, ws_ref[:, cs])


def _in_proj(x, g, w_small):
    t = x.shape[0]
    return pl.pallas_call(
        _in_proj_kernel,
        grid=(t // TM,),
        in_specs=[
            pl.BlockSpec((TM, D_MODEL), lambda i: (i, 0)),
            pl.BlockSpec((1, D_MODEL), lambda i: (0, 0)),
            pl.BlockSpec((D_MODEL, Z_SMALL), lambda i: (0, 0)),
        ],
        out_specs=pl.BlockSpec((TM, Z_SMALL), lambda i: (i, 0)),
        out_shape=jax.ShapeDtypeStruct((t, Z_SMALL), F32),
        compiler_params=_params(("parallel",)),
        name="in_proj",
    )(x, g, w_small)


def _pool_kernel(z_ref, halo_ref, w_ref, scale_ref, o_ref, buf_ref):
    i = pl.program_id(0)
    pos0 = (i * TM) % SEQ
    z = z_ref[...]
    buf_ref[0:POOL_HALO, :] = jnp.where(pos0 == 0, 0.0, halo_ref[...])
    buf_ref[POOL_HALO:, :] = z
    lane = lax.broadcasted_iota(jnp.int32, (TM, POOL_WIDTH), 1)
    pos = lax.broadcasted_iota(jnp.int32, (TM, POOL_WIDTH), 0) + pos0
    group = lane // (POOL_WIDTH // len(POOL_WINDOWS))
    acc = z
    total = jnp.zeros_like(z)
    win = jnp.zeros_like(lane)
    prev_w = 1
    for gi, w in enumerate(POOL_WINDOWS):
        for j in range(prev_w, w):
            acc = acc + buf_ref[POOL_HALO - j:POOL_HALO - j + TM, :]
        prev_w = w
        total = jnp.where(group == gi, acc, total)
        win = jnp.where(group == gi, w, win)
    cnt = jnp.minimum(pos + 1, win).astype(F32)
    pooled = total / cnt - z
    y = _dot(pooled.astype(BF16), w_ref[...]) * scale_ref[...]
    o_ref[...] = y.astype(BF16)


def _pool(zs, w_bd, scale):
    t = zs.shape[0]
    cb = ZP_COL // POOL_WIDTH
    r = TM // POOL_HALO
    return pl.pallas_call(
        _pool_kernel,
        grid=(t // TM,),
        in_specs=[
            pl.BlockSpec((TM, POOL_WIDTH), lambda i: (i, cb)),
            pl.BlockSpec((POOL_HALO, POOL_WIDTH), lambda i: (jnp.maximum(i * r - 1, 0), cb)),
            pl.BlockSpec((POOL_WIDTH, POOL_WIDTH), lambda i: (0, 0)),
            pl.BlockSpec((1, POOL_WIDTH), lambda i: (0, 0)),
        ],
        out_specs=pl.BlockSpec((TM, POOL_WIDTH), lambda i: (i, 0)),
        out_shape=jax.ShapeDtypeStruct((t, POOL_WIDTH), BF16),
        scratch_shapes=[pltpu.VMEM((TM + POOL_HALO, POOL_WIDTH), F32)],
        compiler_params=_params(("parallel",)),
        name="pool_mixer",
    )(zs, zs, w_bd, scale)


def _glu(z2):
    return z2[:, :CONV_WIDTH] * _sigmoid(z2[:, CONV_WIDTH:])


def _conv_kernel(z_ref, halo_ref, w_ref, b_ref, lg_ref, lb_ref, o_ref, buf_ref, sh_ref):
    i = pl.program_id(0)
    pos0 = (i * TM) % SEQ
    buf_ref[0:CONV_HALO, :] = jnp.where(pos0 == 0, 0.0, _glu(halo_ref[...]))
    buf_ref[CONV_HALO:, :] = _glu(z_ref[...])
    span = TM + CONV_HALO - SUBLANES
    for ph in range(1, SUBLANES):
        sh_ref[ph - 1, 0:span, :] = buf_ref[ph:ph + span, :]
    off = CONV_HALO - (CONV_K - 1)
    y = jnp.zeros((TM, CONV_WIDTH), F32) + b_ref[...]
    for k in range(CONV_K):
        ph = (off + k) % SUBLANES
        r0 = off + k - ph
        tap = buf_ref[r0:r0 + TM, :] if ph == 0 else sh_ref[ph - 1, r0:r0 + TM, :]
        y = y + w_ref[k:k + 1, :] * tap
    yn = _layernorm(y, lg_ref[...], lb_ref[...])
    o_ref[...] = (yn * _sigmoid(yn)).astype(BF16)


def _conv(zs, conv_w, conv_b, ln_g, ln_b):
    t = zs.shape[0]
    cb = ZC_COL // (2 * CONV_WIDTH)
    r = TM // CONV_HALO
    vec = pl.BlockSpec((1, CONV_WIDTH), lambda i: (0, 0))
    return pl.pallas_call(
        _conv_kernel,
        grid=(t // TM,),
        in_specs=[
            pl.BlockSpec((TM, 2 * CONV_WIDTH), lambda i: (i, cb)),
            pl.BlockSpec((CONV_HALO, 2 * CONV_WIDTH), lambda i: (jnp.maximum(i * r - 1, 0), cb)),
            pl.BlockSpec((CONV_K, CONV_WIDTH), lambda i: (0, 0)),
            vec, vec, vec,
        ],
        out_specs=pl.BlockSpec((TM, CONV_WIDTH), lambda i: (i, 0)),
        out_shape=jax.ShapeDtypeStruct((t, CONV_WIDTH), BF16),
        scratch_shapes=[pltpu.VMEM((TM + CONV_HALO, CONV_WIDTH), F32),
                        pltpu.VMEM((SUBLANES - 1, TM + CONV_HALO, CONV_WIDTH), F32)],
        compiler_params=_params(("parallel",)),
        name="conv_mixer",
    )(zs, zs, conv_w, conv_b, ln_g, ln_b)


def _sgu_kernel(z_ref, lg_ref, lb_ref, ws_ref, bias_ref, o_ref):
    z = z_ref[...]
    z = 0.5 * z * (1.0 + lax.erf(z * (2.0 ** -0.5)))
    u = z[:, :SGU_WIDTH]
    v = _layernorm(z[:, SGU_WIDTH:], lg_ref[...], lb_ref[...]).astype(BF16)
    rows = SGU_GROUPS * SGU_LEN
    row = lax.broadcasted_iota(jnp.int32, (rows, SGU_LEN), 0) % SGU_LEN
    col = lax.broadcasted_iota(jnp.int32, (rows, SGU_LEN), 1)
    ws = jnp.where(col <= row, ws_ref[...], 0.0).astype(BF16)
    lane_group = lax.broadcasted_iota(jnp.int32, (SGU_LEN, SGU_WIDTH), 1) // (SGU_WIDTH // SGU_GROUPS)
    for blk in range(TM // SGU_LEN):
        r0 = blk * SGU_LEN
        full = _dot(ws, v[r0:r0 + SGU_LEN, :])
        mixed = full[0:SGU_LEN, :]
        for g in range(1, SGU_GROUPS):
            mixed = jnp.where(lane_group == g, full[g * SGU_LEN:(g + 1) * SGU_LEN, :], mixed)
        mixed = mixed + bias_ref[...]
        o_ref[r0:r0 + SGU_LEN, :] = (u[r0:r0 + SGU_LEN, :] * mixed).astype(BF16)


def _sgu(zs, ln_g, ln_b, ws_stack, bias_full):
    t = zs.shape[0]
    cb = ZS_COL // (2 * SGU_WIDTH)
    vec = pl.BlockSpec((1, SGU_WIDTH), lambda i: (0, 0))
    return pl.pallas_call(
        _sgu_kernel,
        grid=(t // TM,),
        in_specs=[
            pl.BlockSpec((TM, 2 * SGU_WIDTH), lambda i: (i, cb)),
            vec, vec,
            pl.BlockSpec((SGU_GROUPS * SGU_LEN, SGU_LEN), lambda i: (0, 0)),
            pl.BlockSpec((SGU_LEN, SGU_WIDTH), lambda i: (0, 0)),
        ],
        out_specs=pl.BlockSpec((TM, SGU_WIDTH), lambda i: (i, 0)),
        out_shape=jax.ShapeDtypeStruct((t, SGU_WIDTH), BF16),
        compiler_params=_params(("parallel",)),
        name="sgu_mixer",
    )(zs, ln_g, ln_b, ws_stack, bias_full)


def _tile_heads(tab):
    return jnp.concatenate([tab] * MLA_HEADS, axis=1)


def _mla_prep_kernel(cq_ref, ckv_ref, kr_ref, qg_ref, kvg_ref, wqa_ref, wqb_ref, wk_ref, wv_ref,
                     place_ref, cq_tab_ref, sq_tab_ref, ck_tab_ref, sk_tab_ref,
                     qt_ref, k_ref, vt_ref):
    cqn = _rms(cq_ref[...], qg_ref[...]).astype(BF16)
    qa = _dot(cqn, wqa_ref[...])
    qb = _dot(cqn, wqb_ref[...])
    scale = (QK_NOPE + QK_ROPE) ** -0.5 * LOG2E
    q = (qa * _tile_heads(cq_tab_ref[...]) + qb * _tile_heads(sq_tab_ref[...])) * scale
    qt_ref[...] = q.T.astype(BF16)
    kvn = _rms(ckv_ref[...], kvg_ref[...]).astype(BF16)
    vt = _dot(kvn, wv_ref[...]).T
    ones_row = lax.broadcasted_iota(jnp.int32, vt.shape, 0) % V_ROWS == V_HEAD
    vt = jnp.where(ones_row, 1.0, vt)
    for c in range(TM // TK):
        vt_ref[c] = vt[:, c * TK:(c + 1) * TK].astype(BF16)
    kr = kr_ref[...]
    kro = kr * ck_tab_ref[...] + pltpu.roll(kr, LANES - QK_ROPE, 1) * sk_tab_ref[...]
    k = _dot(kvn, wk_ref[...]) + _dot(kro.astype(BF16), place_ref[...])
    k_ref[...] = k.astype(BF16)


def _mla_prep(zs, q_g, kv_g, wqa, wqb, wk, wv, place, tabs):
    t = zs.shape[0]
    hw = MLA_HEADS * HEAD_PAD
    hv = MLA_HEADS * V_ROWS
    ns = SEQ // TM
    tab = pl.BlockSpec((TM, LANES), lambda i: (i % ns, 0))
    full = lambda a: pl.BlockSpec(a.shape, lambda i: (0, 0))
    return pl.pallas_call(
        _mla_prep_kernel,
        grid=(t // TM,),
        in_specs=[
            pl.BlockSpec((TM, Q_LORA), lambda i: (i, CQ_COL // Q_LORA)),
            pl.BlockSpec((TM, KV_LORA), lambda i: (i, CKV_COL // KV_LORA)),
            pl.BlockSpec((TM, LANES), lambda i: (i, KR_COL // LANES)),
            full(q_g), full(kv_g), full(wqa), full(wqb), full(wk), full(wv), full(place),
            tab, tab, tab, tab,
        ],
        out_specs=[
            pl.BlockSpec((hw, TM), lambda i: (0, i)),
            pl.BlockSpec((TM, hw), lambda i: (i, 0)),
            pl.BlockSpec((TM // TK, hv, TK), lambda i: (i, 0, 0)),
        ],
        out_shape=[
            jax.ShapeDtypeStruct((hw, t), BF16),
            jax.ShapeDtypeStruct((t, hw), BF16),
            jax.ShapeDtypeStruct((t // TK, hv, TK), BF16),
        ],
        compiler_params=_params(("parallel",)),
        name="mla_prep",
    )(zs, zs, zs, q_g, kv_g, wqa, wqb, wk, wv, place, *tabs)


def _attn_kernel(qt_ref, k_ref, vt_ref, o_ref, m_ref, acc_ref, st_ref):
    qi = pl.program_id(1)
    m_ref[...] = jnp.full(m_ref.shape, NEG, F32)
    acc_ref[...] = jnp.zeros(acc_ref.shape, F32)

    def step(j, masked):
        ks = pl.multiple_of(j * TK, TK)
        tile_max = []
        for h in range(MLA_HEADS):
            hs = slice(h * HEAD_PAD, (h + 1) * HEAD_PAD)
            st = _dot(k_ref[pl.ds(ks, TK), hs], qt_ref[hs, :])
            if masked:
                key_chunk = lax.broadcasted_iota(jnp.int32, (TK, TQ), 0) // CHUNK
                qry_chunk = lax.broadcasted_iota(jnp.int32, (TK, TQ), 1) // CHUNK
                st = jnp.where(key_chunk <= qry_chunk, st, NEG)
            st_ref[h] = st
            tile_max.append(jnp.max(st, axis=0, keepdims=True))
        for h in range(MLA_HEADS):
            m_prev = m_ref[h]
            m_new = jnp.maximum(m_prev, tile_max[h])
            alpha = jnp.exp2(m_prev - m_new)
            p = jnp.exp2(st_ref[h] - m_new).astype(BF16)
            m_ref[h] = m_new
            rows = slice(h * V_ROWS, (h + 1) * V_ROWS)
            acc_ref[rows, :] = alpha * acc_ref[rows, :] + _dot(vt_ref[j, rows, :], p)

    def body(j, carry):
        step(j, False)
        return carry

    lax.fori_loop(0, qi, body, 0)
    step(qi, True)
    outs = [acc_ref[h * V_ROWS:h * V_ROWS + V_HEAD, :] / acc_ref[h * V_ROWS + V_HEAD:h * V_ROWS + V_HEAD + 1, :]
            for h in range(MLA_HEADS)]
    o_ref[...] = jnp.concatenate(outs, axis=0).T.astype(BF16)


def _attention(qt, k, vt, batch):
    t = k.shape[0]
    nq = SEQ // TQ
    nk = SEQ // TK
    hw = MLA_HEADS * HEAD_PAD
    hv = MLA_HEADS * V_HEAD
    hr = MLA_HEADS * V_ROWS
    once = pl.Buffered(1)
    return pl.pallas_call(
        _attn_kernel,
        grid=(batch, nq),
        in_specs=[
            pl.BlockSpec((hw, TQ), lambda b, i: (0, b * nq + i)),
            pl.BlockSpec((SEQ, hw), lambda b, i: (b, 0), pipeline_mode=once),
            pl.BlockSpec((nk, hr, TK), lambda b, i: (b, 0, 0), pipeline_mode=once),
        ],
        out_specs=pl.BlockSpec((TQ, hv), lambda b, i: (b * nq + i, 0)),
        out_shape=jax.ShapeDtypeStruct((t, hv), BF16),
        scratch_shapes=[
            pltpu.VMEM((MLA_HEADS, 1, TQ), F32),
            pltpu.VMEM((hr, TQ), F32),
            pltpu.VMEM((MLA_HEADS, TK, TQ), F32),
        ],
        compiler_params=_params(("parallel", "arbitrary")),
        name="mla_attention",
    )(qt, k, vt)


def _merge_kernel(x_ref, ng_ref, mp_ref, ma_ref, mc_ref, ms_ref, wgate_ref, bg_ref,
                  wp_ref, wa_ref, wc_ref, ws_ref, wo_ref, o_ref, mg_ref):
    x = x_ref[...]
    h = _rms(x, ng_ref[...]).astype(BF16)
    branches = ((mp_ref, wp_ref), (ma_ref, wa_ref), (mc_ref, wc_ref), (ms_ref, ws_ref))
    half = D_MODEL // 2
    for n0 in range(0, D_MODEL, half):
        merged = jnp.zeros((TM, half), F32)
        for b, (m_ref, w_ref) in enumerate(branches):
            c0 = b * D_MODEL + n0
            gate = _sigmoid(_dot(h, wgate_ref[:, c0:c0 + half]) + bg_ref[:, c0:c0 + half])
            merged = merged + gate * _dot(m_ref[...], w_ref[:, n0:n0 + half])
        mg_ref[:, n0:n0 + half] = merged.astype(BF16)
    o_ref[...] = x + _dot(mg_ref[...], wo_ref[...])


def _merge(x, norm_g, mp, ma, mc, ms, w_gate, b_gate, wp, wa, wc, ws, wo):
    t = x.shape[0]
    row = lambda w: pl.BlockSpec((TM, w), lambda i: (i, 0))
    full = lambda a: pl.BlockSpec(a.shape, lambda i: (0, 0))
    return pl.pallas_call(
        _merge_kernel,
        grid=(t // TM,),
        in_specs=[row(D_MODEL), full(norm_g), row(POOL_WIDTH), row(MLA_HEADS * V_HEAD), row(CONV_WIDTH),
                  row(SGU_WIDTH), full(w_gate), full(b_gate), full(wp), full(wa), full(wc), full(ws), full(wo)],
        out_specs=row(D_MODEL),
        out_shape=jax.ShapeDtypeStruct((t, D_MODEL), F32),
        scratch_shapes=[pltpu.VMEM((TM, D_MODEL), BF16)],
        compiler_params=_params(("parallel",)),
        name="merge_out",
    )(x, norm_g, mp, ma, mc, ms, w_gate, b_gate, wp, wa, wc, ws, wo)


def _ffn_kernel(x_ref, g_ref, wg_ref, wu_ref, wd_ref, o_ref):
    x = x_ref[...]
    h = _rms(x, g_ref[...]).astype(BF16)
    acc = x
    for cs in _mxu_chunks(D_FF, 1536):
        gate = _dot(h, wg_ref[:, cs])
        up = _dot(h, wu_ref[:, cs])
        act = (gate * _sigmoid(gate) * up).astype(BF16)
        acc = acc + _dot(act, wd_ref[cs, :])
    o_ref[...] = acc


def _ffn(x, g, wg, wu, wd):
    t = x.shape[0]
    full = lambda a: pl.BlockSpec(a.shape, lambda i: (0, 0))
    return pl.pallas_call(
        _ffn_kernel,
        grid=(t // TM,),
        in_specs=[pl.BlockSpec((TM, D_MODEL), lambda i: (i, 0)), full(g), full(wg), full(wu), full(wd)],
        out_specs=pl.BlockSpec((TM, D_MODEL), lambda i: (i, 0)),
        out_shape=jax.ShapeDtypeStruct((t, D_MODEL), F32),
        compiler_params=_params(("parallel",)),
        name="ffn_dense",
    )(x, g, wg, wu, wd)


def _router_kernel(x_ref, g_ref, wr_ref, br_ref, h_ref, info_ref, cnt_ref, carry_ref):
    @pl.when(pl.program_id(0) == 0)
    def _():
        carry_ref[...] = jnp.zeros(carry_ref.shape, F32)

    h = _rms(x_ref[...], g_ref[...])
    h_ref[...] = h
    logits = jnp.dot(h, wr_ref[...], preferred_element_type=F32,
                     precision=lax.Precision.HIGHEST) + br_ref[...]
    lane = lax.broadcasted_iota(jnp.int32, logits.shape, 1).astype(F32)
    logits = jnp.where(lane < N_EXPERTS, logits, NEG)
    v1 = jnp.max(logits, axis=1, keepdims=True)
    i1 = jnp.min(jnp.where(logits == v1, lane, float(LANES)), axis=1, keepdims=True)
    rest = jnp.where(lane == i1, NEG, logits)
    v2 = jnp.max(rest, axis=1, keepdims=True)
    i2 = jnp.min(jnp.where(rest == v2, lane, float(LANES)), axis=1, keepdims=True)
    e = jnp.exp(v2 - v1)
    w1 = 1.0 / (1.0 + e)
    w2 = e / (1.0 + e)
    chosen = jnp.where((lane == i1) | (lane == i2), 1.0, 0.0)
    row = lax.broadcasted_iota(jnp.int32, (TR, TR), 0)
    col = lax.broadcasted_iota(jnp.int32, (TR, TR), 1)
    before = jnp.where(col < row, 1.0, 0.0)
    pos = _dot(before, chosen) + carry_ref[0:1, :]
    r1 = jnp.sum(jnp.where(lane == i1, pos, 0.0), axis=1, keepdims=True)
    r2 = jnp.sum(jnp.where(lane == i2, pos, 0.0), axis=1, keepdims=True)
    carry_ref[0:1, :] = carry_ref[0:1, :] + jnp.sum(chosen, axis=0, keepdims=True)
    cnt_ref[...] = carry_ref[...]
    fields = (i1, i2, r1, r2, w1, w2)
    info = jnp.zeros(logits.shape, F32)
    for n, val in enumerate(fields):
        info = jnp.where(lane == n, val, info)
    info_ref[...] = info[:, :INFO_W]


def _router(x, g, wr, br):
    t = x.shape[0]
    full = lambda a: pl.BlockSpec(a.shape, lambda i: (0, 0))
    return pl.pallas_call(
        _router_kernel,
        grid=(t // TR,),
        in_specs=[pl.BlockSpec((TR, D_MODEL), lambda i: (i, 0)), full(g), full(wr), full(br)],
        out_specs=[
            pl.BlockSpec((TR, D_MODEL), lambda i: (i, 0)),
            pl.BlockSpec((TR, INFO_W), lambda i: (i, 0)),
            pl.BlockSpec((8, LANES), lambda i: (0, 0)),
        ],
        out_shape=[
            jax.ShapeDtypeStruct((t, D_MODEL), F32),
            jax.ShapeDtypeStruct((t, INFO_W), F32),
            jax.ShapeDtypeStruct((8, LANES), F32),
        ],
        scratch_shapes=[pltpu.VMEM((8, LANES), F32)],
        compiler_params=_params(("arbitrary",)),
        name="moe_router",
    )(x, g, wr, br)


def _num_row_tiles(t):
    return (2 * t) // TMG + N_EXPERTS


def _route_plan(info, cnt, t):
    counts = cnt[0, :N_EXPERTS].astype(jnp.int32)
    padded = ((counts + TMG - 1) // TMG) * TMG
    ends = jnp.cumsum(padded)
    off = ends - padded
    slot = off[info[:, 0:2].astype(jnp.int32)] + info[:, 2:4].astype(jnp.int32)
    slots = slot.reshape(t // BLK, BLK, 2).transpose(0, 2, 1).reshape(-1)
    starts = jnp.arange(_num_row_tiles(t), dtype=jnp.int32) * TMG
    first_row = jnp.minimum(starts, ends[-1] - 1)
    tile_expert = jnp.sum(ends[None, :] <= first_row[:, None], axis=1)
    n_tiles = (ends[-1] // TMG).reshape(1)
    last_tile = jnp.where(padded > 0, ends - TMG, -1)
    tail = starts[-N_EXPERTS:]
    clear = jnp.concatenate([last_tile, jnp.where(tail >= ends[-1], tail, -1)])
    return (slots.astype(jnp.int32), clear.astype(jnp.int32),
            tile_expert.astype(jnp.int32), n_tiles.astype(jnp.int32))


def _row_copy(src_ref, src_row, dst_ref, dst_row, sem):
    return pltpu.make_async_copy(src_ref.at[pl.ds(src_row, 1), :], dst_ref.at[pl.ds(dst_row, 1), :], sem)


def _dispatch_kernel(slots_ref, clear_ref, h_ref, xs_ref, zero_ref, sem):
    @pl.when(pl.program_id(0) == 0)
    def _():
        zero_ref[...] = jnp.zeros(zero_ref.shape, F32)
        for n in range(2 * N_EXPERTS):
            @pl.when(clear_ref[n] >= 0)
            def _():
                start = pl.multiple_of(clear_ref[n], TMG)
                clear = pltpu.make_async_copy(zero_ref, xs_ref.at[pl.ds(start, TMG), :], sem)
                clear.start()
                clear.wait()

    base = pl.program_id(0) * (2 * BLK)

    def body(r, carry):
        for j in range(2):
            _row_copy(h_ref, r, xs_ref, slots_ref[base + j * BLK + r], sem).start()
        return carry

    lax.fori_loop(0, BLK, body, 0, unroll=ROW_UNROLL)
    for j in range(2):
        pltpu.make_async_copy(h_ref, xs_ref.at[pl.ds(0, BLK), :], sem).wait()


def _dispatch(slots, clear, h, n_rows):
    t = h.shape[0]
    return pl.pallas_call(
        _dispatch_kernel,
        grid_spec=pltpu.PrefetchScalarGridSpec(
            num_scalar_prefetch=2,
            grid=(t // BLK,),
            in_specs=[pl.BlockSpec((BLK, D_MODEL), lambda i, s, c: (i, 0))],
            out_specs=pl.BlockSpec(memory_space=pl.ANY),
            scratch_shapes=[pltpu.VMEM((TMG, D_MODEL), F32), pltpu.SemaphoreType.DMA(())],
        ),
        out_shape=jax.ShapeDtypeStruct((n_rows, D_MODEL), F32),
        compiler_params=_params(("arbitrary",)),
        name="moe_dispatch",
    )(slots, clear, h)


def _experts_kernel(te_ref, nt_ref, xs_ref, wg_ref, wu_ref, wd_ref, y_ref, xb_ref, acc_ref):
    del te_ref, nt_ref
    f = pl.program_id(1)

    @pl.when(f == 0)
    def _():
        xb_ref[...] = xs_ref[...].astype(BF16)
        acc_ref[...] = jnp.zeros(acc_ref.shape, F32)

    h = xb_ref[...]
    part = jnp.zeros((TMG, D_MODEL), F32)
    for cs in _mxu_chunks(TF_G, 1024):
        gate = _dot(h, wg_ref[:, cs])
        up = _dot(h, wu_ref[:, cs])
        part = part + _dot((gate * _sigmoid(gate) * up).astype(BF16), wd_ref[cs, :])
    acc_ref[...] += part

    @pl.when(f == pl.num_programs(1) - 1)
    def _():
        y_ref[...] = acc_ref[...]


def _experts(tile_expert, n_tiles, xs, wg, wu, wd):
    n_rows = xs.shape[0]
    nf = D_FF_EXPERT // TF_G

    def f_eff(g, f, nt):
        return jnp.where(g < nt[0], f, nf - 1)

    return pl.pallas_call(
        _experts_kernel,
        grid_spec=pltpu.PrefetchScalarGridSpec(
            num_scalar_prefetch=2,
            grid=(n_rows // TMG, nf),
            in_specs=[
                pl.BlockSpec((TMG, D_MODEL), lambda g, f, te, nt: (g, 0)),
                pl.BlockSpec((None, D_MODEL, TF_G), lambda g, f, te, nt: (te[g], 0, f_eff(g, f, nt))),
                pl.BlockSpec((None, D_MODEL, TF_G), lambda g, f, te, nt: (te[g], 0, f_eff(g, f, nt))),
                pl.BlockSpec((None, TF_G, D_MODEL), lambda g, f, te, nt: (te[g], f_eff(g, f, nt), 0)),
            ],
            out_specs=pl.BlockSpec((TMG, D_MODEL), lambda g, f, te, nt: (g, 0)),
            scratch_shapes=[pltpu.VMEM((TMG, D_MODEL), BF16), pltpu.VMEM((TMG, D_MODEL), F32)],
        ),
        out_shape=jax.ShapeDtypeStruct((n_rows, D_MODEL), F32),
        compiler_params=_params(("arbitrary", "arbitrary")),
        name="moe_experts",
    )(tile_expert, n_tiles, xs, wg, wu, wd)


def _combine_kernel(slots_ref, x_ref, info_ref, fg_ref, y_ref, o_ref, rows_ref, sem):
    base = pl.program_id(0) * (2 * BLK)

    def body(r, carry):
        for j in range(2):
            _row_copy(y_ref, slots_ref[base + j * BLK + r], rows_ref.at[j], r, sem).start()
        return carry

    lax.fori_loop(0, BLK, body, 0, unroll=ROW_UNROLL)
    for j in range(2):
        pltpu.make_async_copy(y_ref.at[pl.ds(0, BLK), :], rows_ref.at[j], sem).wait()
    w1 = info_ref[:, 4:5]
    w2 = info_ref[:, 5:6]
    o_ref[...] = _rms(x_ref[...] + w1 * rows_ref[0] + w2 * rows_ref[1], fg_ref[...])


def _combine(slots, x, info, final_g, y):
    t = x.shape[0]
    return pl.pallas_call(
        _combine_kernel,
        grid_spec=pltpu.PrefetchScalarGridSpec(
            num_scalar_prefetch=1,
            grid=(t // BLK,),
            in_specs=[pl.BlockSpec((BLK, D_MODEL), lambda i, s: (i, 0)),
                      pl.BlockSpec((BLK, INFO_W), lambda i, s: (i, 0)),
                      pl.BlockSpec((1, D_MODEL), lambda i, s: (0, 0)),
                      pl.BlockSpec(memory_space=pl.ANY)],
            out_specs=pl.BlockSpec((BLK, D_MODEL), lambda i, s: (i, 0)),
            scratch_shapes=[pltpu.VMEM((2, BLK, D_MODEL), F32), pltpu.SemaphoreType.DMA(())],
        ),
        out_shape=jax.ShapeDtypeStruct((t, D_MODEL), F32),
        compiler_params=_params(("arbitrary",)),
        name="moe_combine",
    )(slots, x, info, final_g, y)


def _rotate_half_cols(w):
    half = w.shape[-1] // 2
    return jnp.concatenate([-w[..., half:], w[..., :half]], axis=-1)


def _rope_tables():
    pos = jnp.arange(SEQ, dtype=F32)
    inv = ROPE_THETA ** (-jnp.arange(0, QK_ROPE, 2, dtype=F32) / QK_ROPE)
    ang = pos[:, None] * inv[None, :]
    cos, sin = jnp.cos(ang), jnp.sin(ang)
    one = jnp.ones((SEQ, QK_NOPE), F32)
    z_nope = jnp.zeros((SEQ, QK_NOPE), F32)
    z_pad = jnp.zeros((SEQ, HEAD_PAD - QK_NOPE - QK_ROPE), F32)
    z_rest = jnp.zeros((SEQ, LANES - QK_ROPE), F32)
    cq_tab = jnp.concatenate([one, cos, cos, z_pad], axis=1)
    sq_tab = jnp.concatenate([z_nope, sin, sin, z_pad], axis=1)
    ck_tab = jnp.concatenate([cos, cos, z_rest], axis=1)
    sk_tab = jnp.concatenate([sin, sin, z_rest], axis=1)
    return cq_tab, sq_tab, ck_tab, sk_tab


def _rope_placement():
    r = jnp.arange(QK_ROPE)
    place = jnp.zeros((LANES, MLA_HEADS * HEAD_PAD), F32)
    for h in range(MLA_HEADS):
        place = place.at[r, h * HEAD_PAD + QK_NOPE + r].set(1.0)
    return place.astype(BF16)


def _split_in_proj(w):
    zp, cq, ckv, kr, zc, zs, zg = jnp.split(
        w, (256, 512, 640, 672, 1184, 1696), axis=1)
    pad = jnp.zeros((D_MODEL, LANES - 2 * QK_ROPE), w.dtype)
    small = jnp.concatenate([zc, zs, zp, cq, ckv, kr, _rotate_half_cols(kr), pad], axis=1)
    return small.astype(BF16), zg.astype(BF16)


def _mla_weights(w_uq, w_ukv):
    wq = w_uq.reshape(Q_LORA, MLA_HEADS, QK_NOPE + QK_ROPE)
    nope, rope = wq[..., :QK_NOPE], wq[..., QK_NOPE:]
    zq = jnp.zeros((Q_LORA, MLA_HEADS, HEAD_PAD - QK_NOPE - QK_ROPE), w_uq.dtype)
    wqa = jnp.concatenate([nope, rope, zq], axis=-1).reshape(Q_LORA, MLA_HEADS * HEAD_PAD)
    wqb = jnp.concatenate([jnp.zeros_like(nope), _rotate_half_cols(rope), zq], axis=-1)
    wqb = wqb.reshape(Q_LORA, MLA_HEADS * HEAD_PAD)
    wkv = w_ukv.reshape(KV_LORA, MLA_HEADS, QK_NOPE + V_HEAD)
    k_nope, v = wkv[..., :QK_NOPE], wkv[..., QK_NOPE:]
    zk = jnp.zeros((KV_LORA, MLA_HEADS, HEAD_PAD - QK_NOPE), w_ukv.dtype)
    wk = jnp.concatenate([k_nope, zk], axis=-1).reshape(KV_LORA, MLA_HEADS * HEAD_PAD)
    zv = jnp.zeros((KV_LORA, MLA_HEADS, V_ROWS - V_HEAD), w_ukv.dtype)
    wv = jnp.concatenate([v, zv], axis=-1).reshape(KV_LORA, MLA_HEADS * V_ROWS)
    return wqa.astype(BF16), wqb.astype(BF16), wk.astype(BF16), wv.astype(BF16)


def kernel(x, norm_mix_g, w_in, b_gate, w_pool, pool_scale, w_pool_out, q_norm_g, w_uq, kv_norm_g, w_ukv, w_mla_out, conv_w, conv_b, conv_ln_g, conv_ln_b, w_conv_out, sgu_ln_g, sgu_ln_b, sgu_w, sgu_b, w_sgu_out, w_o, norm_ffn_g, w_ffn_gate, w_ffn_up, w_ffn_down, w_router, b_router, w_moe_gate, w_moe_up, w_moe_down, final_norm_g):
    batch, seq, d = x.shape
    assert (seq, d) == (SEQ, D_MODEL)
    depth = w_in.shape[0]
    xt = x.reshape(batch * seq, d)
    tabs = _rope_tables()
    place = _rope_placement()
    row = lambda a: a.reshape(1, -1)

    for l in range(depth):
        w_small, w_gate = _split_in_proj(w_in[l])
        zs = _in_proj(xt, row(norm_mix_g[l]), w_small)

        w_bd = jax.scipy.linalg.block_diag(*[w_pool[l, g] for g in range(w_pool.shape[1])])
        m_pool = _pool(zs, w_bd.astype(BF16), row(pool_scale[l]))
        m_conv = _conv(zs, conv_w[l], row(conv_b[l]), row(conv_ln_g[l]), row(conv_ln_b[l]))
        bias_full = jnp.repeat(sgu_b[l].T, SGU_WIDTH // SGU_GROUPS, axis=1)
        m_sgu = _sgu(zs, row(sgu_ln_g[l]), row(sgu_ln_b[l]),
                     sgu_w[l].reshape(SGU_GROUPS * SGU_LEN, SGU_LEN), bias_full)
        wqa, wqb, wk, wv = _mla_weights(w_uq[l], w_ukv[l])
        q, k, v = _mla_prep(zs, row(q_norm_g[l]), row(kv_norm_g[l]), wqa, wqb, wk, wv, place, tabs)
        m_mla = _attention(q, k, v, batch)

        xt = _merge(xt, row(norm_mix_g[l]), m_pool, m_mla, m_conv, m_sgu, w_gate, row(b_gate[l]),
                    w_pool_out[l].astype(BF16), w_mla_out[l].astype(BF16),
                    w_conv_out[l].astype(BF16), w_sgu_out[l].astype(BF16), w_o[l].astype(BF16))

        j = l // 2
        if l % 2 == 0:
            xt = _ffn(xt, row(norm_ffn_g[l]), w_ffn_gate[j].astype(BF16),
                      w_ffn_up[j].astype(BF16), w_ffn_down[j].astype(BF16))
        else:
            wr = jnp.pad(w_router[j], ((0, 0), (0, LANES - N_EXPERTS)))
            br = jnp.pad(b_router[j], (0, LANES - N_EXPERTS)).reshape(1, LANES)
            t = xt.shape[0]
            h, info, cnt = _router(xt, row(norm_ffn_g[l]), wr, br)
            slots, clear, tile_expert, n_tiles = _route_plan(info, cnt, t)
            xs = _dispatch(slots, clear, h, _num_row_tiles(t) * TMG)
            y = _experts(tile_expert, n_tiles, xs, w_moe_gate[j].astype(BF16),
                         w_moe_up[j].astype(BF16), w_moe_down[j].astype(BF16))
            assert l == depth - 1
            xt = _combine(slots, xt, info, row(final_norm_g), y)
    return xt.reshape(batch, seq, d)
```

```python
import functools

import jax
import jax.numpy as jnp
from jax import lax
from jax.experimental import pallas as pl
from jax.experimental.pallas import tpu as pltpu

F32 = jnp.float32
BF16 = jnp.bfloat16

D_MODEL = 1024
SEQ = 8192
CHUNK = 64
POOL_WIDTH = 256
POOL_WINDOWS = (2, 4, 8, 16)
MLA_HEADS = 8
Q_LORA = 256
KV_LORA = 128
QK_NOPE = 64
QK_ROPE = 32
V_HEAD = 64
V_ROWS = 80
ROPE_THETA = 10000.0
CONV_WIDTH = 256
CONV_K = 31
SGU_WIDTH = 256
SGU_GROUPS = 4
SGU_LEN = 128
N_BRANCH = 4
D_FF = 2816
N_EXPERTS = 8
D_FF_EXPERT = 3584
EPS = 1e-6
NEG = -1e30
LOG2E = 1.4426950408889634

LANES = 128
SUBLANES = 8
MXU_TILE = 256
HEAD_PAD = 128
ZC_COL, ZS_COL, ZP_COL, CQ_COL, CKV_COL, KR_COL = 0, 512, 1024, 1280, 1536, 1664
Z_SMALL = 1792
POOL_HALO = 32
CONV_HALO = 32

TM = 512
TQ = 512
TK = 512
TR = 512
BLK = 512
ROW_UNROLL = 8
TMG = 512
TF_G = 1792
INFO_W = 8
VMEM_LIMIT = 56 * 1024 * 1024


def _params(sem):
    return pltpu.CompilerParams(dimension_semantics=sem, vmem_limit_bytes=VMEM_LIMIT)


def _rms(x, g):
    return x * lax.rsqrt(jnp.mean(x * x, axis=-1, keepdims=True) + EPS) * g


def _layernorm(x, g, b):
    mu = jnp.mean(x, axis=-1, keepdims=True)
    xc = x - mu
    var = jnp.mean(xc * xc, axis=-1, keepdims=True)
    return xc * lax.rsqrt(var + EPS) * g + b


def _sigmoid(x):
    return 0.5 * jnp.tanh(0.5 * x) + 0.5


def _dot(a, b):
    return jnp.dot(a, b, preferred_element_type=F32)


def _mxu_chunks(total, max_chunk):
    assert total % MXU_TILE == 0 and max_chunk % MXU_TILE == 0
    return [slice(s, min(s + max_chunk, total)) for s in range(0, total, max_chunk)]


def _in_proj_kernel(x_ref, g_ref, ws_ref, zs_ref):
    h = _rms(x_ref[...], g_ref[...]).astype(BF16)
    for cs in _mxu_chunks(Z_SMALL, 1024):
        zs_ref[:, cs] = _dot(h, ws_ref[:, cs])


def _in_proj(x, g, w_small):
    t = x.shape[0]
    return pl.pallas_call(
        _in_proj_kernel,
        grid=(t // TM,),
        in_specs=[
            pl.BlockSpec((TM, D_MODEL), lambda i: (i, 0)),
            pl.BlockSpec((1, D_MODEL), lambda i: (0, 0)),
            pl.BlockSpec((D_MODEL, Z_SMALL), lambda i: (0, 0)),
        ],
        out_specs=pl.BlockSpec((TM, Z_SMALL), lambda i: (i, 0)),
        out_shape=jax.ShapeDtypeStruct((t, Z_SMALL), F32),
        compiler_params=_params(("parallel",)),
        name="in_proj",
    )(x, g, w_small)


def _pool_kernel(z_ref, halo_ref, w_ref, scale_ref, o_ref, s1_ref, s2_ref, s4_ref, s8_ref):
    i = pl.program_id(0)
    pos0 = (i * TM) % SEQ
    z = z_ref[...]
    rows = TM + POOL_HALO
    s1_ref[0:POOL_HALO, :] = jnp.where(pos0 == 0, 0.0, halo_ref[...])
    s1_ref[POOL_HALO:, :] = z

    def double(src_ref, dst_ref, shift):
        dst_ref[0:SUBLANES, :] = jnp.zeros((SUBLANES, POOL_WIDTH), F32)
        dst_ref[SUBLANES:, :] = src_ref[SUBLANES:, :] + src_ref[SUBLANES - shift:rows - shift, :]

    double(s1_ref, s2_ref, 1)
    double(s2_ref, s4_ref, 2)
    double(s4_ref, s8_ref, 4)
    s8 = s8_ref[POOL_HALO:, :]
    sums = (s2_ref[POOL_HALO:, :], s4_ref[POOL_HALO:, :], s8, s8 + s8_ref[POOL_HALO - 8:rows - 8, :])
    lane = lax.broadcasted_iota(jnp.int32, (TM, POOL_WIDTH), 1)
    pos = lax.broadcasted_iota(jnp.int32, (TM, POOL_WIDTH), 0) + pos0
    group = lane // (POOL_WIDTH // len(POOL_WINDOWS))
    assert POOL_WINDOWS == (2, 4, 8, 16)
    total = jnp.zeros_like(z)
    win = jnp.zeros_like(lane)
    for gi, w in enumerate(POOL_WINDOWS):
        total = jnp.where(group == gi, sums[gi], total)
        win = jnp.where(group == gi, w, win)
    cnt = jnp.minimum(pos + 1, win).astype(F32)
    pooled = total / cnt - z
    y = _dot(pooled.astype(BF16), w_ref[...]) * scale_ref[...]
    o_ref[...] = y.astype(BF16)


def _pool(zs, w_bd, scale):
    t = zs.shape[0]
    cb = ZP_COL // POOL_WIDTH
    r = TM // POOL_HALO
    return pl.pallas_call(
        _pool_kernel,
        grid=(t // TM,),
        in_specs=[
            pl.BlockSpec((TM, POOL_WIDTH), lambda i: (i, cb)),
            pl.BlockSpec((POOL_HALO, POOL_WIDTH), lambda i: (jnp.maximum(i * r - 1, 0), cb)),
            pl.BlockSpec((POOL_WIDTH, POOL_WIDTH), lambda i: (0, 0)),
            pl.BlockSpec((1, POOL_WIDTH), lambda i: (0, 0)),
        ],
        out_specs=pl.BlockSpec((TM, POOL_WIDTH), lambda i: (i, 0)),
        out_shape=jax.ShapeDtypeStruct((t, POOL_WIDTH), BF16),
        scratch_shapes=[pltpu.VMEM((TM + POOL_HALO, POOL_WIDTH), F32)] * 4,
        compiler_params=_params(("parallel",)),
        name="pool_mixer",
    )(zs, zs, w_bd, scale)


def _glu(z2):
    return z2[:, :CONV_WIDTH] * _sigmoid(z2[:, CONV_WIDTH:])


def _conv_kernel(z_ref, halo_ref, w_ref, b_ref, lg_ref, lb_ref, o_ref, buf_ref, sh_ref):
    i = pl.program_id(0)
    pos0 = (i * TM) % SEQ
    buf_ref[0:CONV_HALO, :] = jnp.where(pos0 == 0, 0.0, _glu(halo_ref[...]))
    buf_ref[CONV_HALO:, :] = _glu(z_ref[...])
    span = TM + CONV_HALO - SUBLANES
    for ph in range(1, SUBLANES):
        sh_ref[ph - 1, 0:span, :] = buf_ref[ph:ph + span, :]
    off = CONV_HALO - (CONV_K - 1)
    y = jnp.zeros((TM, CONV_WIDTH), F32) + b_ref[...]
    for k in range(CONV_K):
        ph = (off + k) % SUBLANES
        r0 = off + k - ph
        tap = buf_ref[r0:r0 + TM, :] if ph == 0 else sh_ref[ph - 1, r0:r0 + TM, :]
        y = y + w_ref[k:k + 1, :] * tap
    yn = _layernorm(y, lg_ref[...], lb_ref[...])
    o_ref[...] = (yn * _sigmoid(yn)).astype(BF16)


def _conv(zs, conv_w, conv_b, ln_g, ln_b):
    t = zs.shape[0]
    cb = ZC_COL // (2 * CONV_WIDTH)
    r = TM // CONV_HALO
    vec = pl.BlockSpec((1, CONV_WIDTH), lambda i: (0, 0))
    return pl.pallas_call(
        _conv_kernel,
        grid=(t // TM,),
        in_specs=[
            pl.BlockSpec((TM, 2 * CONV_WIDTH), lambda i: (i, cb)),
            pl.BlockSpec((CONV_HALO, 2 * CONV_WIDTH), lambda i: (jnp.maximum(i * r - 1, 0), cb)),
            pl.BlockSpec((CONV_K, CONV_WIDTH), lambda i: (0, 0)),
            vec, vec, vec,
        ],
        out_specs=pl.BlockSpec((TM, CONV_WIDTH), lambda i: (i, 0)),
        out_shape=jax.ShapeDtypeStruct((t, CONV_WIDTH), BF16),
        scratch_shapes=[pltpu.VMEM((TM + CONV_HALO, CONV_WIDTH), F32),
                        pltpu.VMEM((SUBLANES - 1, TM + CONV_HALO, CONV_WIDTH), F32)],
        compiler_params=_params(("parallel",)),
        name="conv_mixer",
    )(zs, zs, conv_w, conv_b, ln_g, ln_b)


def _sgu_kernel(z_ref, lg_ref, lb_ref, ws_ref, bias_ref, o_ref):
    z = z_ref[...]
    z = 0.5 * z * (1.0 + lax.erf(z * (2.0 ** -0.5)))
    u = z[:, :SGU_WIDTH]
    v = _layernorm(z[:, SGU_WIDTH:], lg_ref[...], lb_ref[...]).astype(BF16)
    rows = SGU_GROUPS * SGU_LEN
    row = lax.broadcasted_iota(jnp.int32, (rows, SGU_LEN), 0) % SGU_LEN
    col = lax.broadcasted_iota(jnp.int32, (rows, SGU_LEN), 1)
    ws = jnp.where(col <= row, ws_ref[...], 0.0).astype(BF16)
    lane_group = lax.broadcasted_iota(jnp.int32, (SGU_LEN, SGU_WIDTH), 1) // (SGU_WIDTH // SGU_GROUPS)
    for blk in range(TM // SGU_LEN):
        r0 = blk * SGU_LEN
        full = _dot(ws, v[r0:r0 + SGU_LEN, :])
        mixed = full[0:SGU_LEN, :]
        for g in range(1, SGU_GROUPS):
            mixed = jnp.where(lane_group == g, full[g * SGU_LEN:(g + 1) * SGU_LEN, :], mixed)
        mixed = mixed + bias_ref[...]
        o_ref[r0:r0 + SGU_LEN, :] = (u[r0:r0 + SGU_LEN, :] * mixed).astype(BF16)


def _sgu(zs, ln_g, ln_b, ws_stack, bias_full):
    t = zs.shape[0]
    cb = ZS_COL // (2 * SGU_WIDTH)
    vec = pl.BlockSpec((1, SGU_WIDTH), lambda i: (0, 0))
    return pl.pallas_call(
        _sgu_kernel,
        grid=(t // TM,),
        in_specs=[
            pl.BlockSpec((TM, 2 * SGU_WIDTH), lambda i: (i, cb)),
            vec, vec,
            pl.BlockSpec((SGU_GROUPS * SGU_LEN, SGU_LEN), lambda i: (0, 0)),
            pl.BlockSpec((SGU_LEN, SGU_WIDTH), lambda i: (0, 0)),
        ],
        out_specs=pl.BlockSpec((TM, SGU_WIDTH), lambda i: (i, 0)),
        out_shape=jax.ShapeDtypeStruct((t, SGU_WIDTH), BF16),
        compiler_params=_params(("parallel",)),
        name="sgu_mixer",
    )(zs, ln_g, ln_b, ws_stack, bias_full)


def _tile_heads(tab):
    return jnp.concatenate([tab] * MLA_HEADS, axis=1)


def _mla_prep_kernel(cq_ref, ckv_ref, kr_ref, qg_ref, kvg_ref, wqa_ref, wqb_ref, wk_ref, wv_ref,
                     place_ref, cq_tab_ref, sq_tab_ref, ck_tab_ref, sk_tab_ref,
                     qt_ref, k_ref, vt_ref):
    cqn = _rms(cq_ref[...], qg_ref[...]).astype(BF16)
    qa = _dot(cqn, wqa_ref[...])
    qb = _dot(cqn, wqb_ref[...])
    scale = (QK_NOPE + QK_ROPE) ** -0.5 * LOG2E
    q = (qa * _tile_heads(cq_tab_ref[...]) + qb * _tile_heads(sq_tab_ref[...])) * scale
    qt_ref[...] = q.T.astype(BF16)
    kvn = _rms(ckv_ref[...], kvg_ref[...]).astype(BF16)
    vt = _dot(kvn, wv_ref[...]).T
    ones_row = lax.broadcasted_iota(jnp.int32, vt.shape, 0) % V_ROWS == V_HEAD
    vt = jnp.where(ones_row, 1.0, vt)
    for c in range(TM // TK):
        vt_ref[c] = vt[:, c * TK:(c + 1) * TK].astype(BF16)
    kr = kr_ref[...]
    kro = kr * ck_tab_ref[...] + pltpu.roll(kr, LANES - QK_ROPE, 1) * sk_tab_ref[...]
    k = _dot(kvn, wk_ref[...]) + _dot(kro.astype(BF16), place_ref[...])
    k_ref[...] = k.astype(BF16)


def _mla_prep(zs, q_g, kv_g, wqa, wqb, wk, wv, place, tabs):
    t = zs.shape[0]
    hw = MLA_HEADS * HEAD_PAD
    hv = MLA_HEADS * V_ROWS
    ns = SEQ // TM
    tab = pl.BlockSpec((TM, LANES), lambda i: (i % ns, 0))
    full = lambda a: pl.BlockSpec(a.shape, lambda i: (0, 0))
    return pl.pallas_call(
        _mla_prep_kernel,
        grid=(t // TM,),
        in_specs=[
            pl.BlockSpec((TM, Q_LORA), lambda i: (i, CQ_COL // Q_LORA)),
            pl.BlockSpec((TM, KV_LORA), lambda i: (i, CKV_COL // KV_LORA)),
            pl.BlockSpec((TM, LANES), lambda i: (i, KR_COL // LANES)),
            full(q_g), full(kv_g), full(wqa), full(wqb), full(wk), full(wv), full(place),
            tab, tab, tab, tab,
        ],
        out_specs=[
            pl.BlockSpec((hw, TM), lambda i: (0, i)),
            pl.BlockSpec((TM, hw), lambda i: (i, 0)),
            pl.BlockSpec((TM // TK, hv, TK), lambda i: (i, 0, 0)),
        ],
        out_shape=[
            jax.ShapeDtypeStruct((hw, t), BF16),
            jax.ShapeDtypeStruct((t, hw), BF16),
            jax.ShapeDtypeStruct((t // TK, hv, TK), BF16),
        ],
        compiler_params=_params(("parallel",)),
        name="mla_prep",
    )(zs, zs, zs, q_g, kv_g, wqa, wqb, wk, wv, place, *tabs)


def _attn_kernel(qt_ref, k_ref, vt_ref, o_ref, m_ref, acc_ref, st_ref):
    qi = pl.program_id(1)
    m_ref[...] = jnp.full(m_ref.shape, NEG, F32)
    acc_ref[...] = jnp.zeros(acc_ref.shape, F32)

    def step(j, masked):
        ks = pl.multiple_of(j * TK, TK)
        tile_max = []
        for h in range(MLA_HEADS):
            hs = slice(h * HEAD_PAD, (h + 1) * HEAD_PAD)
            st = _dot(k_ref[pl.ds(ks, TK), hs], qt_ref[hs, :])
            if masked:
                key_chunk = lax.broadcasted_iota(jnp.int32, (TK, TQ), 0) // CHUNK
                qry_chunk = lax.broadcasted_iota(jnp.int32, (TK, TQ), 1) // CHUNK
                st = jnp.where(key_chunk <= qry_chunk, st, NEG)
            st_ref[h] = st
            tile_max.append(jnp.max(st, axis=0, keepdims=True))
        for h in range(MLA_HEADS):
            m_prev = m_ref[h]
            m_new = jnp.maximum(m_prev, tile_max[h])
            alpha = jnp.exp2(m_prev - m_new)
            p = jnp.exp2(st_ref[h] - m_new).astype(BF16)
            m_ref[h] = m_new
            rows = slice(h * V_ROWS, (h + 1) * V_ROWS)
            acc_ref[rows, :] = alpha * acc_ref[rows, :] + _dot(vt_ref[j, rows, :], p)

    def body(j, carry):
        step(j, False)
        return carry

    lax.fori_loop(0, qi, body, 0)
    step(qi, True)
    outs = [acc_ref[h * V_ROWS:h * V_ROWS + V_HEAD, :] / acc_ref[h * V_ROWS + V_HEAD:h * V_ROWS + V_HEAD + 1, :]
            for h in range(MLA_HEADS)]
    o_ref[...] = jnp.concatenate(outs, axis=0).T.astype(BF16)


def _attention(qt, k, vt, batch):
    t = k.shape[0]
    nq = SEQ // TQ
    nk = SEQ // TK
    hw = MLA_HEADS * HEAD_PAD
    hv = MLA_HEADS * V_HEAD
    hr = MLA_HEADS * V_ROWS
    once = pl.Buffered(1)
    return pl.pallas_call(
        _attn_kernel,
        grid=(batch, nq),
        in_specs=[
            pl.BlockSpec((hw, TQ), lambda b, i: (0, b * nq + i)),
            pl.BlockSpec((SEQ, hw), lambda b, i: (b, 0), pipeline_mode=once),
            pl.BlockSpec((nk, hr, TK), lambda b, i: (b, 0, 0), pipeline_mode=once),
        ],
        out_specs=pl.BlockSpec((TQ, hv), lambda b, i: (b * nq + i, 0)),
        out_shape=jax.ShapeDtypeStruct((t, hv), BF16),
        scratch_shapes=[
            pltpu.VMEM((MLA_HEADS, 1, TQ), F32),
            pltpu.VMEM((hr, TQ), F32),
            pltpu.VMEM((MLA_HEADS, TK, TQ), F32),
        ],
        compiler_params=_params(("parallel", "arbitrary")),
        name="mla_attention",
    )(qt, k, vt)


def _merge_kernel(x_ref, ng_ref, mp_ref, ma_ref, mc_ref, ms_ref, wgate_ref, bg_ref,
                  wp_ref, wa_ref, wc_ref, ws_ref, wo_ref, o_ref, mg_ref):
    x = x_ref[...]
    h = _rms(x, ng_ref[...]).astype(BF16)
    branches = ((mp_ref, wp_ref), (ma_ref, wa_ref), (mc_ref, wc_ref), (ms_ref, ws_ref))
    half = D_MODEL // 2
    for n0 in range(0, D_MODEL, half):
        merged = jnp.zeros((TM, half), F32)
        for b, (m_ref, w_ref) in enumerate(branches):
            c0 = b * D_MODEL + n0
            gate = _sigmoid(_dot(h, wgate_ref[:, c0:c0 + half]) + bg_ref[:, c0:c0 + half])
            merged = merged + gate * _dot(m_ref[...], w_ref[:, n0:n0 + half])
        mg_ref[:, n0:n0 + half] = merged.astype(BF16)
    o_ref[...] = x + _dot(mg_ref[...], wo_ref[...])


def _merge(x, norm_g, mp, ma, mc, ms, w_gate, b_gate, wp, wa, wc, ws, wo):
    t = x.shape[0]
    row = lambda w: pl.BlockSpec((TM, w), lambda i: (i, 0))
    full = lambda a: pl.BlockSpec(a.shape, lambda i: (0, 0))
    return pl.pallas_call(
        _merge_kernel,
        grid=(t // TM,),
        in_specs=[row(D_MODEL), full(norm_g), row(POOL_WIDTH), row(MLA_HEADS * V_HEAD), row(CONV_WIDTH),
                  row(SGU_WIDTH), full(w_gate), full(b_gate), full(wp), full(wa), full(wc), full(ws), full(wo)],
        out_specs=row(D_MODEL),
        out_shape=jax.ShapeDtypeStruct((t, D_MODEL), F32),
        scratch_shapes=[pltpu.VMEM((TM, D_MODEL), BF16)],
        compiler_params=_params(("parallel",)),
        name="merge_out",
    )(x, norm_g, mp, ma, mc, ms, w_gate, b_gate, wp, wa, wc, ws, wo)


def _ffn_kernel(x_ref, g_ref, wg_ref, wu_ref, wd_ref, o_ref):
    x = x_ref[...]
    h = _rms(x, g_ref[...]).astype(BF16)
    acc = x
    for cs in _mxu_chunks(D_FF, 1536):
        gate = _dot(h, wg_ref[:, cs])
        up = _dot(h, wu_ref[:, cs])
        act = (gate * _sigmoid(gate) * up).astype(BF16)
        acc = acc + _dot(act, wd_ref[cs, :])
    o_ref[...] = acc


def _ffn(x, g, wg, wu, wd):
    t = x.shape[0]
    full = lambda a: pl.BlockSpec(a.shape, lambda i: (0, 0))
    return pl.pallas_call(
        _ffn_kernel,
        grid=(t // TM,),
        in_specs=[pl.BlockSpec((TM, D_MODEL), lambda i: (i, 0)), full(g), full(wg), full(wu), full(wd)],
        out_specs=pl.BlockSpec((TM, D_MODEL), lambda i: (i, 0)),
        out_shape=jax.ShapeDtypeStruct((t, D_MODEL), F32),
        compiler_params=_params(("parallel",)),
        name="ffn_dense",
    )(x, g, wg, wu, wd)


def _router_kernel(x_ref, g_ref, wr_ref, br_ref, h_ref, info_ref, cnt_ref, carry_ref):
    @pl.when(pl.program_id(0) == 0)
    def _():
        carry_ref[...] = jnp.zeros(carry_ref.shape, F32)

    h = _rms(x_ref[...], g_ref[...])
    h_ref[...] = h
    w = wr_ref[...]
    h_hi = h.astype(BF16)
    w_hi = w.astype(BF16)
    h_lo = (h - h_hi.astype(F32)).astype(BF16)
    w_lo = (w - w_hi.astype(F32)).astype(BF16)
    logits = _dot(h_hi, w_hi) + _dot(h_lo, w_hi) + _dot(h_hi, w_lo) + br_ref[...]
    lane = lax.broadcasted_iota(jnp.int32, logits.shape, 1).astype(F32)
    logits = jnp.where(lane < N_EXPERTS, logits, NEG)
    v1 = jnp.max(logits, axis=1, keepdims=True)
    i1 = jnp.min(jnp.where(logits == v1, lane, float(LANES)), axis=1, keepdims=True)
    rest = jnp.where(lane == i1, NEG, logits)
    v2 = jnp.max(rest, axis=1, keepdims=True)
    i2 = jnp.min(jnp.where(rest == v2, lane, float(LANES)), axis=1, keepdims=True)
    e = jnp.exp(v2 - v1)
    w1 = 1.0 / (1.0 + e)
    w2 = e / (1.0 + e)
    chosen = jnp.where((lane == i1) | (lane == i2), 1.0, 0.0)
    row = lax.broadcasted_iota(jnp.int32, (TR, TR), 0)
    col = lax.broadcasted_iota(jnp.int32, (TR, TR), 1)
    before = jnp.where(col < row, 1.0, 0.0)
    pos = _dot(before, chosen) + carry_ref[0:1, :]
    r1 = jnp.sum(jnp.where(lane == i1, pos, 0.0), axis=1, keepdims=True)
    r2 = jnp.sum(jnp.where(lane == i2, pos, 0.0), axis=1, keepdims=True)
    carry_ref[0:1, :] = carry_ref[0:1, :] + jnp.sum(chosen, axis=0, keepdims=True)
    cnt_ref[...] = carry_ref[...]
    fields = (i1, i2, r1, r2, w1, w2)
    info = jnp.zeros(logits.shape, F32)
    for n, val in enumerate(fields):
        info = jnp.where(lane == n, val, info)
    info_ref[...] = info[:, :INFO_W]


def _router(x, g, wr, br):
    t = x.shape[0]
    full = lambda a: pl.BlockSpec(a.shape, lambda i: (0, 0))
    return pl.pallas_call(
        _router_kernel,
        grid=(t // TR,),
        in_specs=[pl.BlockSpec((TR, D_MODEL), lambda i: (i, 0)), full(g), full(wr), full(br)],
        out_specs=[
            pl.BlockSpec((TR, D_MODEL), lambda i: (i, 0)),
            pl.BlockSpec((TR, INFO_W), lambda i: (i, 0)),
            pl.BlockSpec((8, LANES), lambda i: (0, 0)),
        ],
        out_shape=[
            jax.ShapeDtypeStruct((t, D_MODEL), F32),
            jax.ShapeDtypeStruct((t, INFO_W), F32),
            jax.ShapeDtypeStruct((8, LANES), F32),
        ],
        scratch_shapes=[pltpu.VMEM((8, LANES), F32)],
        compiler_params=_params(("arbitrary",)),
        name="moe_router",
    )(x, g, wr, br)


def _num_row_tiles(t):
    return (2 * t) // TMG + N_EXPERTS


def _route_plan(info, cnt, t):
    counts = cnt[0, :N_EXPERTS].astype(jnp.int32)
    padded = ((counts + TMG - 1) // TMG) * TMG
    ends = jnp.cumsum(padded)
    off = ends - padded
    slot = off[info[:, 0:2].astype(jnp.int32)] + info[:, 2:4].astype(jnp.int32)
    slots = slot.reshape(t // BLK, BLK, 2).transpose(0, 2, 1).reshape(-1)
    starts = jnp.arange(_num_row_tiles(t), dtype=jnp.int32) * TMG
    first_row = jnp.minimum(starts, ends[-1] - 1)
    tile_expert = jnp.sum(ends[None, :] <= first_row[:, None], axis=1)
    n_tiles = (ends[-1] // TMG).reshape(1)
    last_tile = jnp.where(padded > 0, ends - TMG, -1)
    tail = starts[-N_EXPERTS:]
    clear = jnp.concatenate([last_tile, jnp.where(tail >= ends[-1], tail, -1)])
    return (slots.astype(jnp.int32), clear.astype(jnp.int32),
            tile_expert.astype(jnp.int32), n_tiles.astype(jnp.int32))


def _row_copy(src_ref, src_row, dst_ref, dst_row, sem):
    return pltpu.make_async_copy(src_ref.at[pl.ds(src_row, 1), :], dst_ref.at[pl.ds(dst_row, 1), :], sem)


def _dispatch_kernel(slots_ref, clear_ref, h_ref, xs_ref, zero_ref, sem):
    @pl.when(pl.program_id(0) == 0)
    def _():
        zero_ref[...] = jnp.zeros(zero_ref.shape, F32)
        for n in range(2 * N_EXPERTS):
            @pl.when(clear_ref[n] >= 0)
            def _():
                start = pl.multiple_of(clear_ref[n], TMG)
                clear = pltpu.make_async_copy(zero_ref, xs_ref.at[pl.ds(start, TMG), :], sem)
                clear.start()
                clear.wait()

    base = pl.program_id(0) * (2 * BLK)

    def body(r, carry):
        for j in range(2):
            _row_copy(h_ref, r, xs_ref, slots_ref[base + j * BLK + r], sem).start()
        return carry

    lax.fori_loop(0, BLK, body, 0, unroll=ROW_UNROLL)
    for j in range(2):
        pltpu.make_async_copy(h_ref, xs_ref.at[pl.ds(0, BLK), :], sem).wait()


def _dispatch(slots, clear, h, n_rows):
    t = h.shape[0]
    return pl.pallas_call(
        _dispatch_kernel,
        grid_spec=pltpu.PrefetchScalarGridSpec(
            num_scalar_prefetch=2,
            grid=(t // BLK,),
            in_specs=[pl.BlockSpec((BLK, D_MODEL), lambda i, s, c: (i, 0))],
            out_specs=pl.BlockSpec(memory_space=pl.ANY),
            scratch_shapes=[pltpu.VMEM((TMG, D_MODEL), F32), pltpu.SemaphoreType.DMA(())],
        ),
        out_shape=jax.ShapeDtypeStruct((n_rows, D_MODEL), F32),
        compiler_params=_params(("arbitrary",)),
        name="moe_dispatch",
    )(slots, clear, h)


def _experts_kernel(te_ref, nt_ref, xs_ref, wg_ref, wu_ref, wd_ref, y_ref, xb_ref, acc_ref):
    del te_ref
    f = pl.program_id(1)
    active = pl.program_id(0) < nt_ref[0]

    @pl.when(f == 0)
    def _():
        xb_ref[...] = xs_ref[...].astype(BF16)
        acc_ref[...] = jnp.zeros(acc_ref.shape, F32)

    @pl.when(active)
    def _():
        h = xb_ref[...]
        part = jnp.zeros((TMG, D_MODEL), F32)
        for cs in _mxu_chunks(TF_G, 1024):
            gate = _dot(h, wg_ref[:, cs])
            up = _dot(h, wu_ref[:, cs])
            part = part + _dot((gate * _sigmoid(gate) * up).astype(BF16), wd_ref[cs, :])
        acc_ref[...] += part

    @pl.when(f == pl.num_programs(1) - 1)
    def _():
        y_ref[...] = acc_ref[...]


def _experts(tile_expert, n_tiles, xs, wg, wu, wd):
    n_rows = xs.shape[0]
    nf = D_FF_EXPERT // TF_G

    def f_eff(g, f, nt):
        return jnp.where(g < nt[0], f, nf - 1)

    return pl.pallas_call(
        _experts_kernel,
        grid_spec=pltpu.PrefetchScalarGridSpec(
            num_scalar_prefetch=2,
            grid=(n_rows // TMG, nf),
            in_specs=[
                pl.BlockSpec((TMG, D_MODEL), lambda g, f, te, nt: (g, 0)),
                pl.BlockSpec((None, D_MODEL, TF_G), lambda g, f, te, nt: (te[g], 0, f_eff(g, f, nt))),
                pl.BlockSpec((None, D_MODEL, TF_G), lambda g, f, te, nt: (te[g], 0, f_eff(g, f, nt))),
                pl.BlockSpec((None, TF_G, D_MODEL), lambda g, f, te, nt: (te[g], f_eff(g, f, nt), 0)),
            ],
            out_specs=pl.BlockSpec((TMG, D_MODEL), lambda g, f, te, nt: (g, 0)),
            scratch_shapes=[pltpu.VMEM((TMG, D_MODEL), BF16), pltpu.VMEM((TMG, D_MODEL), F32)],
        ),
        out_shape=jax.ShapeDtypeStruct((n_rows, D_MODEL), F32),
        compiler_params=_params(("arbitrary", "arbitrary")),
        name="moe_experts",
    )(tile_expert, n_tiles, xs, wg, wu, wd)


def _combine_kernel(slots_ref, x_ref, info_ref, fg_ref, y_ref, o_ref, rows_ref, sem):
    base = pl.program_id(0) * (2 * BLK)

    def body(r, carry):
        for j in range(2):
            _row_copy(y_ref, slots_ref[base + j * BLK + r], rows_ref.at[j], r, sem).start()
        return carry

    lax.fori_loop(0, BLK, body, 0, unroll=ROW_UNROLL)
    for j in range(2):
        pltpu.make_async_copy(y_ref.at[pl.ds(0, BLK), :], rows_ref.at[j], sem).wait()
    w1 = info_ref[:, 4:5]
    w2 = info_ref[:, 5:6]
    o_ref[...] = _rms(x_ref[...] + w1 * rows_ref[0] + w2 * rows_ref[1], fg_ref[...])


def _combine(slots, x, info, final_g, y):
    t = x.shape[0]
    return pl.pallas_call(
        _combine_kernel,
        grid_spec=pltpu.PrefetchScalarGridSpec(
            num_scalar_prefetch=1,
            grid=(t // BLK,),
            in_specs=[pl.BlockSpec((BLK, D_MODEL), lambda i, s: (i, 0)),
                      pl.BlockSpec((BLK, INFO_W), lambda i, s: (i, 0)),
                      pl.BlockSpec((1, D_MODEL), lambda i, s: (0, 0)),
                      pl.BlockSpec(memory_space=pl.ANY)],
            out_specs=pl.BlockSpec((BLK, D_MODEL), lambda i, s: (i, 0)),
            scratch_shapes=[pltpu.VMEM((2, BLK, D_MODEL), F32), pltpu.SemaphoreType.DMA(())],
        ),
        out_shape=jax.ShapeDtypeStruct((t, D_MODEL), F32),
        compiler_params=_params(("arbitrary",)),
        name="moe_combine",
    )(slots, x, info, final_g, y)


def _block_diag(w):
    g, c, _ = w.shape
    eye = jnp.eye(g, dtype=w.dtype)
    return (w[:, :, None, :] * eye[:, None, :, None]).reshape(g * c, g * c)


def _rotate_half_cols(w):
    half = w.shape[-1] // 2
    return jnp.concatenate([-w[..., half:], w[..., :half]], axis=-1)


def _rope_tables():
    pos = jnp.arange(SEQ, dtype=F32)
    inv = ROPE_THETA ** (-jnp.arange(0, QK_ROPE, 2, dtype=F32) / QK_ROPE)
    ang = pos[:, None] * inv[None, :]
    cos, sin = jnp.cos(ang), jnp.sin(ang)
    one = jnp.ones((SEQ, QK_NOPE), F32)
    z_nope = jnp.zeros((SEQ, QK_NOPE), F32)
    z_pad = jnp.zeros((SEQ, HEAD_PAD - QK_NOPE - QK_ROPE), F32)
    z_rest = jnp.zeros((SEQ, LANES - QK_ROPE), F32)
    cq_tab = jnp.concatenate([one, cos, cos, z_pad], axis=1)
    sq_tab = jnp.concatenate([z_nope, sin, sin, z_pad], axis=1)
    ck_tab = jnp.concatenate([cos, cos, z_rest], axis=1)
    sk_tab = jnp.concatenate([sin, sin, z_rest], axis=1)
    return cq_tab, sq_tab, ck_tab, sk_tab


def _rope_placement():
    r = jnp.arange(QK_ROPE)
    place = jnp.zeros((LANES, MLA_HEADS * HEAD_PAD), F32)
    for h in range(MLA_HEADS):
        place = place.at[r, h * HEAD_PAD + QK_NOPE + r].set(1.0)
    return place.astype(BF16)


def _split_in_proj(w):
    zp, cq, ckv, kr, zc, zs, zg = jnp.split(
        w, (256, 512, 640, 672, 1184, 1696), axis=1)
    pad = jnp.zeros((D_MODEL, LANES - 2 * QK_ROPE), w.dtype)
    small = jnp.concatenate([zc, zs, zp, cq, ckv, kr, _rotate_half_cols(kr), pad], axis=1)
    return small.astype(BF16), zg.astype(BF16)


def _mla_weights(w_uq, w_ukv):
    wq = w_uq.reshape(Q_LORA, MLA_HEADS, QK_NOPE + QK_ROPE)
    nope, rope = wq[..., :QK_NOPE], wq[..., QK_NOPE:]
    zq = jnp.zeros((Q_LORA, MLA_HEADS, HEAD_PAD - QK_NOPE - QK_ROPE), w_uq.dtype)
    wqa = jnp.concatenate([nope, rope, zq], axis=-1).reshape(Q_LORA, MLA_HEADS * HEAD_PAD)
    wqb = jnp.concatenate([jnp.zeros_like(nope), _rotate_half_cols(rope), zq], axis=-1)
    wqb = wqb.reshape(Q_LORA, MLA_HEADS * HEAD_PAD)
    wkv = w_ukv.reshape(KV_LORA, MLA_HEADS, QK_NOPE + V_HEAD)
    k_nope, v = wkv[..., :QK_NOPE], wkv[..., QK_NOPE:]
    zk = jnp.zeros((KV_LORA, MLA_HEADS, HEAD_PAD - QK_NOPE), w_ukv.dtype)
    wk = jnp.concatenate([k_nope, zk], axis=-1).reshape(KV_LORA, MLA_HEADS * HEAD_PAD)
    zv = jnp.zeros((KV_LORA, MLA_HEADS, V_ROWS - V_HEAD), w_ukv.dtype)
    wv = jnp.concatenate([v, zv], axis=-1).reshape(KV_LORA, MLA_HEADS * V_ROWS)
    return wqa.astype(BF16), wqb.astype(BF16), wk.astype(BF16), wv.astype(BF16)


def kernel(x, norm_mix_g, w_in, b_gate, w_pool, pool_scale, w_pool_out, q_norm_g, w_uq, kv_norm_g, w_ukv, w_mla_out, conv_w, conv_b, conv_ln_g, conv_ln_b, w_conv_out, sgu_ln_g, sgu_ln_b, sgu_w, sgu_b, w_sgu_out, w_o, norm_ffn_g, w_ffn_gate, w_ffn_up, w_ffn_down, w_router, b_router, w_moe_gate, w_moe_up, w_moe_down, final_norm_g):
    batch, seq, d = x.shape
    assert (seq, d) == (SEQ, D_MODEL)
    depth = w_in.shape[0]
    xt = x.reshape(batch * seq, d)
    tabs = _rope_tables()
    place = _rope_placement()
    row = lambda a: a.reshape(1, -1)

    for l in range(depth):
        w_small, w_gate = _split_in_proj(w_in[l])
        zs = _in_proj(xt, row(norm_mix_g[l]), w_small)

        w_bd = _block_diag(w_pool[l])
        m_pool = _pool(zs, w_bd.astype(BF16), row(pool_scale[l]))
        m_conv = _conv(zs, conv_w[l], row(conv_b[l]), row(conv_ln_g[l]), row(conv_ln_b[l]))
        bias_full = jnp.repeat(sgu_b[l].T, SGU_WIDTH // SGU_GROUPS, axis=1)
        m_sgu = _sgu(zs, row(sgu_ln_g[l]), row(sgu_ln_b[l]),
                     sgu_w[l].reshape(SGU_GROUPS * SGU_LEN, SGU_LEN), bias_full)
        wqa, wqb, wk, wv = _mla_weights(w_uq[l], w_ukv[l])
        q, k, v = _mla_prep(zs, row(q_norm_g[l]), row(kv_norm_g[l]), wqa, wqb, wk, wv, place, tabs)
        m_mla = _attention(q, k, v, batch)

        xt = _merge(xt, row(norm_mix_g[l]), m_pool, m_mla, m_conv, m_sgu, w_gate, row(b_gate[l]),
                    w_pool_out[l].astype(BF16), w_mla_out[l].astype(BF16),
                    w_conv_out[l].astype(BF16), w_sgu_out[l].astype(BF16), w_o[l].astype(BF16))

        j = l // 2
        if l % 2 == 0:
            xt = _ffn(xt, row(norm_ffn_g[l]), w_ffn_gate[j].astype(BF16),
                      w_ffn_up[j].astype(BF16), w_ffn_down[j].astype(BF16))
        else:
            wr = jnp.pad(w_router[j], ((0, 0), (0, LANES - N_EXPERTS)))
            br = jnp.pad(b_router[j], (0, LANES - N_EXPERTS)).reshape(1, LANES)
            t = xt.shape[0]
            h, info, cnt = _router(xt, row(norm_ffn_g[l]), wr, br)
            slots, clear, tile_expert, n_tiles = _route_plan(info, cnt, t)
            xs = _dispatch(slots, clear, h, _num_row_tiles(t) * TMG)
            y = _experts(tile_expert, n_tiles, xs, w_moe_gate[j].astype(BF16),
                         w_moe_up[j].astype(BF16), w_moe_down[j].astype(BF16))
            assert l == depth - 1
            xt = _combine(slots, xt, info, row(final_norm_g), y)
    return xt.reshape(batch, seq, d)
```

```python
import functools

import jax
import jax.numpy as jnp
from jax import lax
from jax.experimental import pallas as pl
from jax.experimental.pallas import tpu as pltpu

F32 = jnp.float32
BF16 = jnp.bfloat16

D_MODEL = 1024
SEQ = 8192
CHUNK = 64
POOL_WIDTH = 256
POOL_WINDOWS = (2, 4, 8, 16)
MLA_HEADS = 8
Q_LORA = 256
KV_LORA = 128
QK_NOPE = 64
QK_ROPE = 32
V_HEAD = 64
V_ROWS = 80
ROPE_THETA = 10000.0
CONV_WIDTH = 256
CONV_K = 31
SGU_WIDTH = 256
SGU_GROUPS = 4
SGU_LEN = 128
N_BRANCH = 4
D_FF = 2816
N_EXPERTS = 8
D_FF_EXPERT = 3584
EPS = 1e-6
NEG = -1e30
LOG2E = 1.4426950408889634

LANES = 128
SUBLANES = 8
MXU_TILE = 256
HEAD_PAD = 128
ZC_COL, ZS_COL, ZP_COL, CQ_COL, CKV_COL, KR_COL = 0, 512, 1024, 1280, 1536, 1664
Z_SMALL = 1792
POOL_HALO = 32
CONV_HALO = 32

TM = 512
TQ = 512
TK = 512
TR = 512
BLK = 512
ROW_UNROLL = 8
TMG = 512
TF_G = 1792
INFO_W = 8
VMEM_LIMIT = 56 * 1024 * 1024


def _params(sem):
    return pltpu.CompilerParams(dimension_semantics=sem, vmem_limit_bytes=VMEM_LIMIT)


def _rms(x, g):
    return x * lax.rsqrt(jnp.mean(x * x, axis=-1, keepdims=True) + EPS) * g


def _layernorm(x, g, b):
    mu = jnp.mean(x, axis=-1, keepdims=True)
    xc = x - mu
    var = jnp.mean(xc * xc, axis=-1, keepdims=True)
    return xc * lax.rsqrt(var + EPS) * g + b


def _sigmoid(x):
    return 0.5 * jnp.tanh(0.5 * x) + 0.5


def _dot(a, b):
    return jnp.dot(a, b, preferred_element_type=F32)


def _mxu_chunks(total, max_chunk):
    assert total % MXU_TILE == 0 and max_chunk % MXU_TILE == 0
    return [slice(s, min(s + max_chunk, total)) for s in range(0, total, max_chunk)]


def _in_proj_kernel(x_ref, g_ref, ws_ref, zs_ref):
    h = _rms(x_ref[...], g_ref[...]).astype(BF16)
    for cs in _mxu_chunks(Z_SMALL, 1024):
        zs_ref[:, cs] = _dot(h, ws_ref[:, cs])


def _in_proj(x, g, w_small):
    t = x.shape[0]
    return pl.pallas_call(
        _in_proj_kernel,
        grid=(t // TM,),
        in_specs=[
            pl.BlockSpec((TM, D_MODEL), lambda i: (i, 0)),
            pl.BlockSpec((1, D_MODEL), lambda i: (0, 0)),
            pl.BlockSpec((D_MODEL, Z_SMALL), lambda i: (0, 0)),
        ],
        out_specs=pl.BlockSpec((TM, Z_SMALL), lambda i: (i, 0)),
        out_shape=jax.ShapeDtypeStruct((t, Z_SMALL), F32),
        compiler_params=_params(("parallel",)),
        name="in_proj",
    )(x, g, w_small)


def _pool_kernel(z_ref, halo_ref, w_ref, scale_ref, o_ref, s1_ref, s2_ref, s4_ref, s8_ref):
    i = pl.program_id(0)
    pos0 = (i * TM) % SEQ
    z = z_ref[...]
    rows = TM + POOL_HALO
    s1_ref[0:POOL_HALO, :] = jnp.where(pos0 == 0, 0.0, halo_ref[...])
    s1_ref[POOL_HALO:, :] = z

    def double(src_ref, dst_ref, shift):
        dst_ref[0:SUBLANES, :] = jnp.zeros((SUBLANES, POOL_WIDTH), F32)
        dst_ref[SUBLANES:, :] = src_ref[SUBLANES:, :] + src_ref[SUBLANES - shift:rows - shift, :]

    double(s1_ref, s2_ref, 1)
    double(s2_ref, s4_ref, 2)
    double(s4_ref, s8_ref, 4)
    s8 = s8_ref[POOL_HALO:, :]
    sums = (s2_ref[POOL_HALO:, :], s4_ref[POOL_HALO:, :], s8, s8 + s8_ref[POOL_HALO - 8:rows - 8, :])
    lane = lax.broadcasted_iota(jnp.int32, (TM, POOL_WIDTH), 1)
    pos = lax.broadcasted_iota(jnp.int32, (TM, POOL_WIDTH), 0) + pos0
    group = lane // (POOL_WIDTH // len(POOL_WINDOWS))
    assert POOL_WINDOWS == (2, 4, 8, 16)
    total = jnp.zeros_like(z)
    win = jnp.zeros_like(lane)
    for gi, w in enumerate(POOL_WINDOWS):
        total = jnp.where(group == gi, sums[gi], total)
        win = jnp.where(group == gi, w, win)
    cnt = jnp.minimum(pos + 1, win).astype(F32)
    pooled = total / cnt - z
    y = _dot(pooled.astype(BF16), w_ref[...]) * scale_ref[...]
    o_ref[...] = y.astype(BF16)


def _pool(zs, w_bd, scale):
    t = zs.shape[0]
    cb = ZP_COL // POOL_WIDTH
    r = TM // POOL_HALO
    return pl.pallas_call(
        _pool_kernel,
        grid=(t // TM,),
        in_specs=[
            pl.BlockSpec((TM, POOL_WIDTH), lambda i: (i, cb)),
            pl.BlockSpec((POOL_HALO, POOL_WIDTH), lambda i: (jnp.maximum(i * r - 1, 0), cb)),
            pl.BlockSpec((POOL_WIDTH, POOL_WIDTH), lambda i: (0, 0)),
            pl.BlockSpec((1, POOL_WIDTH), lambda i: (0, 0)),
        ],
        out_specs=pl.BlockSpec((TM, POOL_WIDTH), lambda i: (i, 0)),
        out_shape=jax.ShapeDtypeStruct((t, POOL_WIDTH), BF16),
        scratch_shapes=[pltpu.VMEM((TM + POOL_HALO, POOL_WIDTH), F32)] * 4,
        compiler_params=_params(("parallel",)),
        name="pool_mixer",
    )(zs, zs, w_bd, scale)


def _glu(z2):
    return z2[:, :CONV_WIDTH] * _sigmoid(z2[:, CONV_WIDTH:])


def _conv_kernel(z_ref, halo_ref, w_ref, b_ref, lg_ref, lb_ref, o_ref, buf_ref, sh_ref):
    i = pl.program_id(0)
    pos0 = (i * TM) % SEQ
    buf_ref[0:CONV_HALO, :] = jnp.where(pos0 == 0, 0.0, _glu(halo_ref[...]))
    buf_ref[CONV_HALO:, :] = _glu(z_ref[...])
    span = TM + CONV_HALO - SUBLANES
    for ph in range(1, SUBLANES):
        sh_ref[ph - 1, 0:span, :] = buf_ref[ph:ph + span, :]
    off = CONV_HALO - (CONV_K - 1)
    y = jnp.zeros((TM, CONV_WIDTH), F32) + b_ref[...]
    for k in range(CONV_K):
        ph = (off + k) % SUBLANES
        r0 = off + k - ph
        tap = buf_ref[r0:r0 + TM, :] if ph == 0 else sh_ref[ph - 1, r0:r0 + TM, :]
        y = y + w_ref[k:k + 1, :] * tap
    yn = _layernorm(y, lg_ref[...], lb_ref[...])
    o_ref[...] = (yn * _sigmoid(yn)).astype(BF16)


def _conv(zs, conv_w, conv_b, ln_g, ln_b):
    t = zs.shape[0]
    cb = ZC_COL // (2 * CONV_WIDTH)
    r = TM // CONV_HALO
    vec = pl.BlockSpec((1, CONV_WIDTH), lambda i: (0, 0))
    return pl.pallas_call(
        _conv_kernel,
        grid=(t // TM,),
        in_specs=[
            pl.BlockSpec((TM, 2 * CONV_WIDTH), lambda i: (i, cb)),
            pl.BlockSpec((CONV_HALO, 2 * CONV_WIDTH), lambda i: (jnp.maximum(i * r - 1, 0), cb)),
            pl.BlockSpec((CONV_K, CONV_WIDTH), lambda i: (0, 0)),
            vec, vec, vec,
        ],
        out_specs=pl.BlockSpec((TM, CONV_WIDTH), lambda i: (i, 0)),
        out_shape=jax.ShapeDtypeStruct((t, CONV_WIDTH), BF16),
        scratch_shapes=[pltpu.VMEM((TM + CONV_HALO, CONV_WIDTH), F32),
                        pltpu.VMEM((SUBLANES - 1, TM + CONV_HALO, CONV_WIDTH), F32)],
        compiler_params=_params(("parallel",)),
        name="conv_mixer",
    )(zs, zs, conv_w, conv_b, ln_g, ln_b)


def _sgu_kernel(z_ref, lg_ref, lb_ref, ws_ref, bias_ref, o_ref):
    z = z_ref[...]
    z = 0.5 * z * (1.0 + lax.erf(z * (2.0 ** -0.5)))
    u = z[:, :SGU_WIDTH]
    v = _layernorm(z[:, SGU_WIDTH:], lg_ref[...], lb_ref[...]).astype(BF16)
    rows = SGU_GROUPS * SGU_LEN
    row = lax.broadcasted_iota(jnp.int32, (rows, SGU_LEN), 0) % SGU_LEN
    col = lax.broadcasted_iota(jnp.int32, (rows, SGU_LEN), 1)
    ws = jnp.where(col <= row, ws_ref[...], 0.0).astype(BF16)
    lane_group = lax.broadcasted_iota(jnp.int32, (SGU_LEN, SGU_WIDTH), 1) // (SGU_WIDTH // SGU_GROUPS)
    for blk in range(TM // SGU_LEN):
        r0 = blk * SGU_LEN
        full = _dot(ws, v[r0:r0 + SGU_LEN, :])
        mixed = full[0:SGU_LEN, :]
        for g in range(1, SGU_GROUPS):
            mixed = jnp.where(lane_group == g, full[g * SGU_LEN:(g + 1) * SGU_LEN, :], mixed)
        mixed = mixed + bias_ref[...]
        o_ref[r0:r0 + SGU_LEN, :] = (u[r0:r0 + SGU_LEN, :] * mixed).astype(BF16)


def _sgu(zs, ln_g, ln_b, ws_stack, bias_full):
    t = zs.shape[0]
    cb = ZS_COL // (2 * SGU_WIDTH)
    vec = pl.BlockSpec((1, SGU_WIDTH), lambda i: (0, 0))
    return pl.pallas_call(
        _sgu_kernel,
        grid=(t // TM,),
        in_specs=[
            pl.BlockSpec((TM, 2 * SGU_WIDTH), lambda i: (i, cb)),
            vec, vec,
            pl.BlockSpec((SGU_GROUPS * SGU_LEN, SGU_LEN), lambda i: (0, 0)),
            pl.BlockSpec((SGU_LEN, SGU_WIDTH), lambda i: (0, 0)),
        ],
        out_specs=pl.BlockSpec((TM, SGU_WIDTH), lambda i: (i, 0)),
        out_shape=jax.ShapeDtypeStruct((t, SGU_WIDTH), BF16),
        compiler_params=_params(("parallel",)),
        name="sgu_mixer",
    )(zs, ln_g, ln_b, ws_stack, bias_full)


def _tile_heads(tab):
    return jnp.concatenate([tab] * MLA_HEADS, axis=1)


def _mla_prep_kernel(cq_ref, ckv_ref, kr_ref, qg_ref, kvg_ref, wqa_ref, wqb_ref, wk_ref, wv_ref,
                     place_ref, cq_tab_ref, sq_tab_ref, ck_tab_ref, sk_tab_ref,
                     qt_ref, k_ref, vt_ref):
    cqn = _rms(cq_ref[...], qg_ref[...]).astype(BF16)
    qa = _dot(cqn, wqa_ref[...])
    qb = _dot(cqn, wqb_ref[...])
    scale = (QK_NOPE + QK_ROPE) ** -0.5 * LOG2E
    q = (qa * _tile_heads(cq_tab_ref[...]) + qb * _tile_heads(sq_tab_ref[...])) * scale
    qt_ref[...] = q.T.astype(BF16)
    kvn = _rms(ckv_ref[...], kvg_ref[...]).astype(BF16)
    vt = _dot(kvn, wv_ref[...]).T
    ones_row = lax.broadcasted_iota(jnp.int32, vt.shape, 0) % V_ROWS == V_HEAD
    vt = jnp.where(ones_row, 1.0, vt)
    for c in range(TM // TK):
        vt_ref[c] = vt[:, c * TK:(c + 1) * TK].astype(BF16)
    kr = kr_ref[...]
    kro = kr * ck_tab_ref[...] + pltpu.roll(kr, LANES - QK_ROPE, 1) * sk_tab_ref[...]
    k = _dot(kvn, wk_ref[...]) + _dot(kro.astype(BF16), place_ref[...])
    k_ref[...] = k.astype(BF16)


def _mla_prep(zs, q_g, kv_g, wqa, wqb, wk, wv, place, tabs):
    t = zs.shape[0]
    hw = MLA_HEADS * HEAD_PAD
    hv = MLA_HEADS * V_ROWS
    ns = SEQ // TM
    tab = pl.BlockSpec((TM, LANES), lambda i: (i % ns, 0))
    full = lambda a: pl.BlockSpec(a.shape, lambda i: (0, 0))
    return pl.pallas_call(
        _mla_prep_kernel,
        grid=(t // TM,),
        in_specs=[
            pl.BlockSpec((TM, Q_LORA), lambda i: (i, CQ_COL // Q_LORA)),
            pl.BlockSpec((TM, KV_LORA), lambda i: (i, CKV_COL // KV_LORA)),
            pl.BlockSpec((TM, LANES), lambda i: (i, KR_COL // LANES)),
            full(q_g), full(kv_g), full(wqa), full(wqb), full(wk), full(wv), full(place),
            tab, tab, tab, tab,
        ],
        out_specs=[
            pl.BlockSpec((hw, TM), lambda i: (0, i)),
            pl.BlockSpec((TM, hw), lambda i: (i, 0)),
            pl.BlockSpec((TM // TK, hv, TK), lambda i: (i, 0, 0)),
        ],
        out_shape=[
            jax.ShapeDtypeStruct((hw, t), BF16),
            jax.ShapeDtypeStruct((t, hw), BF16),
            jax.ShapeDtypeStruct((t // TK, hv, TK), BF16),
        ],
        compiler_params=_params(("parallel",)),
        name="mla_prep",
    )(zs, zs, zs, q_g, kv_g, wqa, wqb, wk, wv, place, *tabs)


def _attn_kernel(qt_ref, k_ref, vt_ref, o_ref, m_ref, acc_ref, st0_ref, st1_ref, mx_ref):
    qi = pl.program_id(1)
    m_ref[...] = jnp.full(m_ref.shape, NEG, F32)
    acc_ref[...] = jnp.zeros(acc_ref.shape, F32)
    bufs = (st0_ref, st1_ref)

    def scores_head(h, j, slot, masked):
        ks = pl.multiple_of(j * TK, TK)
        hs = slice(h * HEAD_PAD, (h + 1) * HEAD_PAD)
        st = _dot(k_ref[pl.ds(ks, TK), hs], qt_ref[hs, :])
        if masked:
            key_chunk = lax.broadcasted_iota(jnp.int32, (TK, TQ), 0) // CHUNK
            qry_chunk = lax.broadcasted_iota(jnp.int32, (TK, TQ), 1) // CHUNK
            st = jnp.where(key_chunk <= qry_chunk, st, NEG)
        bufs[slot][h] = st
        mx_ref[slot, h] = jnp.max(st, axis=0, keepdims=True)

    def update_head(h, j, slot):
        m_prev = m_ref[h]
        m_new = jnp.maximum(m_prev, mx_ref[slot, h])
        alpha = jnp.exp2(m_prev - m_new)
        p = jnp.exp2(bufs[slot][h] - m_new).astype(BF16)
        m_ref[h] = m_new
        rows = slice(h * V_ROWS, (h + 1) * V_ROWS)
        acc_ref[rows, :] = alpha * acc_ref[rows, :] + _dot(vt_ref[j, rows, :], p)

    def scores(j, slot, masked):
        for h in range(MLA_HEADS):
            scores_head(h, j, slot, masked)

    def update(j, slot):
        for h in range(MLA_HEADS):
            update_head(h, j, slot)

    def scores_and_update(j_next, slot_next, masked, j, slot):
        for h in range(MLA_HEADS):
            scores_head(h, j_next, slot_next, masked)
            update_head(h, j, slot)

    @pl.when(qi > 0)
    def _():
        scores(0, 0, False)

    def pair(i, carry):
        scores_and_update(2 * i + 1, 1, False, 2 * i, 0)
        scores_and_update(2 * i + 2, 0, False, 2 * i + 1, 1)
        return carry

    pairs = jnp.maximum(qi - 1, 0) // 2
    lax.fori_loop(0, pairs, pair, 0)
    done = 2 * pairs

    @pl.when(qi == 0)
    def _():
        scores(0, 0, True)
        update(0, 0)

    @pl.when(qi % 2 == 1)
    def _():
        scores_and_update(qi, 1, True, done, 0)
        update(qi, 1)

    @pl.when((qi > 0) & (qi % 2 == 0))
    def _():
        scores_and_update(qi - 1, 1, False, done, 0)
        scores_and_update(qi, 0, True, qi - 1, 1)
        update(qi, 0)

    outs = [acc_ref[h * V_ROWS:h * V_ROWS + V_HEAD, :] / acc_ref[h * V_ROWS + V_HEAD:h * V_ROWS + V_HEAD + 1, :]
            for h in range(MLA_HEADS)]
    o_ref[...] = jnp.concatenate(outs, axis=0).T.astype(BF16)


def _attention(qt, k, vt, batch):
    t = k.shape[0]
    nq = SEQ // TQ
    nk = SEQ // TK
    hw = MLA_HEADS * HEAD_PAD
    hv = MLA_HEADS * V_HEAD
    hr = MLA_HEADS * V_ROWS
    once = pl.Buffered(1)
    return pl.pallas_call(
        _attn_kernel,
        grid=(batch, nq),
        in_specs=[
            pl.BlockSpec((hw, TQ), lambda b, i: (0, b * nq + i)),
            pl.BlockSpec((SEQ, hw), lambda b, i: (b, 0), pipeline_mode=once),
            pl.BlockSpec((nk, hr, TK), lambda b, i: (b, 0, 0), pipeline_mode=once),
        ],
        out_specs=pl.BlockSpec((TQ, hv), lambda b, i: (b * nq + i, 0)),
        out_shape=jax.ShapeDtypeStruct((t, hv), BF16),
        scratch_shapes=[
            pltpu.VMEM((MLA_HEADS, 1, TQ), F32),
            pltpu.VMEM((hr, TQ), F32),
            pltpu.VMEM((MLA_HEADS, TK, TQ), F32),
            pltpu.VMEM((MLA_HEADS, TK, TQ), F32),
            pltpu.VMEM((2, MLA_HEADS, 1, TQ), F32),
        ],
        compiler_params=_params(("parallel", "arbitrary")),
        name="mla_attention",
    )(qt, k, vt)


def _merge_kernel(x_ref, ng_ref, mp_ref, ma_ref, mc_ref, ms_ref, wgate_ref, bg_ref,
                  wp_ref, wa_ref, wc_ref, ws_ref, wo_ref, o_ref, mg_ref):
    x = x_ref[...]
    h = _rms(x, ng_ref[...]).astype(BF16)
    branches = ((mp_ref, wp_ref), (ma_ref, wa_ref), (mc_ref, wc_ref), (ms_ref, ws_ref))
    half = D_MODEL // 2
    for n0 in range(0, D_MODEL, half):
        merged = jnp.zeros((TM, half), F32)
        for b, (m_ref, w_ref) in enumerate(branches):
            c0 = b * D_MODEL + n0
            gate = _sigmoid(_dot(h, wgate_ref[:, c0:c0 + half]) + bg_ref[:, c0:c0 + half])
            merged = merged + gate * _dot(m_ref[...], w_ref[:, n0:n0 + half])
        mg_ref[:, n0:n0 + half] = merged.astype(BF16)
    o_ref[...] = x + _dot(mg_ref[...], wo_ref[...])


def _merge(x, norm_g, mp, ma, mc, ms, w_gate, b_gate, wp, wa, wc, ws, wo):
    t = x.shape[0]
    row = lambda w: pl.BlockSpec((TM, w), lambda i: (i, 0))
    full = lambda a: pl.BlockSpec(a.shape, lambda i: (0, 0))
    return pl.pallas_call(
        _merge_kernel,
        grid=(t // TM,),
        in_specs=[row(D_MODEL), full(norm_g), row(POOL_WIDTH), row(MLA_HEADS * V_HEAD), row(CONV_WIDTH),
                  row(SGU_WIDTH), full(w_gate), full(b_gate), full(wp), full(wa), full(wc), full(ws), full(wo)],
        out_specs=row(D_MODEL),
        out_shape=jax.ShapeDtypeStruct((t, D_MODEL), F32),
        scratch_shapes=[pltpu.VMEM((TM, D_MODEL), BF16)],
        compiler_params=_params(("parallel",)),
        name="merge_out",
    )(x, norm_g, mp, ma, mc, ms, w_gate, b_gate, wp, wa, wc, ws, wo)


def _ffn_kernel(x_ref, g_ref, wg_ref, wu_ref, wd_ref, o_ref):
    x = x_ref[...]
    h = _rms(x, g_ref[...]).astype(BF16)
    acc = x
    for cs in _mxu_chunks(D_FF, 1536):
        gate = _dot(h, wg_ref[:, cs])
        up = _dot(h, wu_ref[:, cs])
        act = (gate * _sigmoid(gate) * up).astype(BF16)
        acc = acc + _dot(act, wd_ref[cs, :])
    o_ref[...] = acc


def _ffn(x, g, wg, wu, wd):
    t = x.shape[0]
    full = lambda a: pl.BlockSpec(a.shape, lambda i: (0, 0))
    return pl.pallas_call(
        _ffn_kernel,
        grid=(t // TM,),
        in_specs=[pl.BlockSpec((TM, D_MODEL), lambda i: (i, 0)), full(g), full(wg), full(wu), full(wd)],
        out_specs=pl.BlockSpec((TM, D_MODEL), lambda i: (i, 0)),
        out_shape=jax.ShapeDtypeStruct((t, D_MODEL), F32),
        compiler_params=_params(("parallel",)),
        name="ffn_dense",
    )(x, g, wg, wu, wd)


def _router_kernel(x_ref, g_ref, wr_ref, br_ref, h_ref, info_ref, cnt_ref, carry_ref):
    @pl.when(pl.program_id(0) == 0)
    def _():
        carry_ref[...] = jnp.zeros(carry_ref.shape, F32)

    h = _rms(x_ref[...], g_ref[...])
    h_ref[...] = h
    w = wr_ref[...]
    h_hi = h.astype(BF16)
    w_hi = w.astype(BF16)
    h_lo = (h - h_hi.astype(F32)).astype(BF16)
    w_lo = (w - w_hi.astype(F32)).astype(BF16)
    logits = _dot(h_hi, w_hi) + _dot(h_lo, w_hi) + _dot(h_hi, w_lo) + br_ref[...]
    lane = lax.broadcasted_iota(jnp.int32, logits.shape, 1).astype(F32)
    logits = jnp.where(lane < N_EXPERTS, logits, NEG)
    v1 = jnp.max(logits, axis=1, keepdims=True)
    i1 = jnp.min(jnp.where(logits == v1, lane, float(LANES)), axis=1, keepdims=True)
    rest = jnp.where(lane == i1, NEG, logits)
    v2 = jnp.max(rest, axis=1, keepdims=True)
    i2 = jnp.min(jnp.where(rest == v2, lane, float(LANES)), axis=1, keepdims=True)
    e = jnp.exp(v2 - v1)
    w1 = 1.0 / (1.0 + e)
    w2 = e / (1.0 + e)
    chosen = jnp.where((lane == i1) | (lane == i2), 1.0, 0.0)
    row = lax.broadcasted_iota(jnp.int32, (TR, TR), 0)
    col = lax.broadcasted_iota(jnp.int32, (TR, TR), 1)
    before = jnp.where(col < row, 1.0, 0.0)
    pos = _dot(before, chosen) + carry_ref[0:1, :]
    r1 = jnp.sum(jnp.where(lane == i1, pos, 0.0), axis=1, keepdims=True)
    r2 = jnp.sum(jnp.where(lane == i2, pos, 0.0), axis=1, keepdims=True)
    carry_ref[0:1, :] = carry_ref[0:1, :] + jnp.sum(chosen, axis=0, keepdims=True)
    cnt_ref[...] = carry_ref[...]
    fields = (i1, i2, r1, r2, w1, w2)
    info = jnp.zeros(logits.shape, F32)
    for n, val in enumerate(fields):
        info = jnp.where(lane == n, val, info)
    info_ref[...] = info[:, :INFO_W]


def _router(x, g, wr, br):
    t = x.shape[0]
    full = lambda a: pl.BlockSpec(a.shape, lambda i: (0, 0))
    return pl.pallas_call(
        _router_kernel,
        grid=(t // TR,),
        in_specs=[pl.BlockSpec((TR, D_MODEL), lambda i: (i, 0)), full(g), full(wr), full(br)],
        out_specs=[
            pl.BlockSpec((TR, D_MODEL), lambda i: (i, 0)),
            pl.BlockSpec((TR, INFO_W), lambda i: (i, 0)),
            pl.BlockSpec((8, LANES), lambda i: (0, 0)),
        ],
        out_shape=[
            jax.ShapeDtypeStruct((t, D_MODEL), F32),
            jax.ShapeDtypeStruct((t, INFO_W), F32),
            jax.ShapeDtypeStruct((8, LANES), F32),
        ],
        scratch_shapes=[pltpu.VMEM((8, LANES), F32)],
        compiler_params=_params(("arbitrary",)),
        name="moe_router",
    )(x, g, wr, br)


def _num_row_tiles(t):
    return (2 * t) // TMG + N_EXPERTS


def _route_plan(info, cnt, t):
    counts = cnt[0, :N_EXPERTS].astype(jnp.int32)
    padded = ((counts + TMG - 1) // TMG) * TMG
    ends = jnp.cumsum(padded)
    off = ends - padded
    slot = off[info[:, 0:2].astype(jnp.int32)] + info[:, 2:4].astype(jnp.int32)
    slots = slot.reshape(t // BLK, BLK, 2).transpose(0, 2, 1).reshape(-1)
    starts = jnp.arange(_num_row_tiles(t), dtype=jnp.int32) * TMG
    first_row = jnp.minimum(starts, ends[-1] - 1)
    tile_expert = jnp.sum(ends[None, :] <= first_row[:, None], axis=1)
    n_tiles = (ends[-1] // TMG).reshape(1)
    last_tile = jnp.where(padded > 0, ends - TMG, -1)
    tail = starts[-N_EXPERTS:]
    clear = jnp.concatenate([last_tile, jnp.where(tail >= ends[-1], tail, -1)])
    return (slots.astype(jnp.int32), clear.astype(jnp.int32),
            tile_expert.astype(jnp.int32), n_tiles.astype(jnp.int32))


def _row_copy(src_ref, src_row, dst_ref, dst_row, sem):
    return pltpu.make_async_copy(src_ref.at[pl.ds(src_row, 1), :], dst_ref.at[pl.ds(dst_row, 1), :], sem)


def _dispatch_kernel(slots_ref, clear_ref, h_ref, xs_ref, zero_ref, sem):
    @pl.when(pl.program_id(0) == 0)
    def _():
        zero_ref[...] = jnp.zeros(zero_ref.shape, F32)
        for n in range(2 * N_EXPERTS):
            @pl.when(clear_ref[n] >= 0)
            def _():
                start = pl.multiple_of(clear_ref[n], TMG)
                clear = pltpu.make_async_copy(zero_ref, xs_ref.at[pl.ds(start, TMG), :], sem)
                clear.start()
                clear.wait()

    base = pl.program_id(0) * (2 * BLK)

    def body(r, carry):
        for j in range(2):
            _row_copy(h_ref, r, xs_ref, slots_ref[base + j * BLK + r], sem).start()
        return carry

    lax.fori_loop(0, BLK, body, 0, unroll=ROW_UNROLL)
    for j in range(2):
        pltpu.make_async_copy(h_ref, xs_ref.at[pl.ds(0, BLK), :], sem).wait()


def _dispatch(slots, clear, h, n_rows):
    t = h.shape[0]
    return pl.pallas_call(
        _dispatch_kernel,
        grid_spec=pltpu.PrefetchScalarGridSpec(
            num_scalar_prefetch=2,
            grid=(t // BLK,),
            in_specs=[pl.BlockSpec((BLK, D_MODEL), lambda i, s, c: (i, 0))],
            out_specs=pl.BlockSpec(memory_space=pl.ANY),
            scratch_shapes=[pltpu.VMEM((TMG, D_MODEL), F32), pltpu.SemaphoreType.DMA(())],
        ),
        out_shape=jax.ShapeDtypeStruct((n_rows, D_MODEL), F32),
        compiler_params=_params(("arbitrary",)),
        name="moe_dispatch",
    )(slots, clear, h)


def _experts_kernel(te_ref, nt_ref, xs_ref, wg_ref, wu_ref, wd_ref, y_ref, xb_ref, acc_ref):
    del te_ref
    f = pl.program_id(1)
    active = pl.program_id(0) < nt_ref[0]

    @pl.when(f == 0)
    def _():
        xb_ref[...] = xs_ref[...].astype(BF16)
        acc_ref[...] = jnp.zeros(acc_ref.shape, F32)

    @pl.when(active)
    def _():
        h = xb_ref[...]
        part = jnp.zeros((TMG, D_MODEL), F32)
        for cs in _mxu_chunks(TF_G, 1024):
            gate = _dot(h, wg_ref[:, cs])
            up = _dot(h, wu_ref[:, cs])
            part = part + _dot((gate * _sigmoid(gate) * up).astype(BF16), wd_ref[cs, :])
        acc_ref[...] += part

    @pl.when(f == pl.num_programs(1) - 1)
    def _():
        y_ref[...] = acc_ref[...]


def _experts(tile_expert, n_tiles, xs, wg, wu, wd):
    n_rows = xs.shape[0]
    nf = D_FF_EXPERT // TF_G

    def f_eff(g, f, nt):
        return jnp.where(g < nt[0], f, nf - 1)

    return pl.pallas_call(
        _experts_kernel,
        grid_spec=pltpu.PrefetchScalarGridSpec(
            num_scalar_prefetch=2,
            grid=(n_rows // TMG, nf),
            in_specs=[
                pl.BlockSpec((TMG, D_MODEL), lambda g, f, te, nt: (g, 0)),
                pl.BlockSpec((None, D_MODEL, TF_G), lambda g, f, te, nt: (te[g], 0, f_eff(g, f, nt))),
                pl.BlockSpec((None, D_MODEL, TF_G), lambda g, f, te, nt: (te[g], 0, f_eff(g, f, nt))),
                pl.BlockSpec((None, TF_G, D_MODEL), lambda g, f, te, nt: (te[g], f_eff(g, f, nt), 0)),
            ],
            out_specs=pl.BlockSpec((TMG, D_MODEL), lambda g, f, te, nt: (g, 0)),
            scratch_shapes=[pltpu.VMEM((TMG, D_MODEL), BF16), pltpu.VMEM((TMG, D_MODEL), F32)],
        ),
        out_shape=jax.ShapeDtypeStruct((n_rows, D_MODEL), F32),
        compiler_params=_params(("arbitrary", "arbitrary")),
        name="moe_experts",
    )(tile_expert, n_tiles, xs, wg, wu, wd)


def _combine_kernel(slots_ref, x_ref, info_ref, fg_ref, y_ref, o_ref, rows_ref, sem):
    base = pl.program_id(0) * (2 * BLK)

    def body(r, carry):
        for j in range(2):
            _row_copy(y_ref, slots_ref[base + j * BLK + r], rows_ref.at[j], r, sem).start()
        return carry

    lax.fori_loop(0, BLK, body, 0, unroll=ROW_UNROLL)
    for j in range(2):
        pltpu.make_async_copy(y_ref.at[pl.ds(0, BLK), :], rows_ref.at[j], sem).wait()
    w1 = info_ref[:, 4:5]
    w2 = info_ref[:, 5:6]
    o_ref[...] = _rms(x_ref[...] + w1 * rows_ref[0] + w2 * rows_ref[1], fg_ref[...])


def _combine(slots, x, info, final_g, y):
    t = x.shape[0]
    return pl.pallas_call(
        _combine_kernel,
        grid_spec=pltpu.PrefetchScalarGridSpec(
            num_scalar_prefetch=1,
            grid=(t // BLK,),
            in_specs=[pl.BlockSpec((BLK, D_MODEL), lambda i, s: (i, 0)),
                      pl.BlockSpec((BLK, INFO_W), lambda i, s: (i, 0)),
                      pl.BlockSpec((1, D_MODEL), lambda i, s: (0, 0)),
                      pl.BlockSpec(memory_space=pl.ANY)],
            out_specs=pl.BlockSpec((BLK, D_MODEL), lambda i, s: (i, 0)),
            scratch_shapes=[pltpu.VMEM((2, BLK, D_MODEL), F32), pltpu.SemaphoreType.DMA(())],
        ),
        out_shape=jax.ShapeDtypeStruct((t, D_MODEL), F32),
        compiler_params=_params(("arbitrary",)),
        name="moe_combine",
    )(slots, x, info, final_g, y)


def _block_diag(w):
    g, c, _ = w.shape
    eye = jnp.eye(g, dtype=w.dtype)
    return (w[:, :, None, :] * eye[:, None, :, None]).reshape(g * c, g * c)


def _rotate_half_cols(w):
    half = w.shape[-1] // 2
    return jnp.concatenate([-w[..., half:], w[..., :half]], axis=-1)


def _rope_tables():
    pos = jnp.arange(SEQ, dtype=F32)
    inv = ROPE_THETA ** (-jnp.arange(0, QK_ROPE, 2, dtype=F32) / QK_ROPE)
    ang = pos[:, None] * inv[None, :]
    cos, sin = jnp.cos(ang), jnp.sin(ang)
    one = jnp.ones((SEQ, QK_NOPE), F32)
    z_nope = jnp.zeros((SEQ, QK_NOPE), F32)
    z_pad = jnp.zeros((SEQ, HEAD_PAD - QK_NOPE - QK_ROPE), F32)
    z_rest = jnp.zeros((SEQ, LANES - QK_ROPE), F32)
    cq_tab = jnp.concatenate([one, cos, cos, z_pad], axis=1)
    sq_tab = jnp.concatenate([z_nope, sin, sin, z_pad], axis=1)
    ck_tab = jnp.concatenate([cos, cos, z_rest], axis=1)
    sk_tab = jnp.concatenate([sin, sin, z_rest], axis=1)
    return cq_tab, sq_tab, ck_tab, sk_tab


def _rope_placement():
    r = jnp.arange(QK_ROPE)
    place = jnp.zeros((LANES, MLA_HEADS * HEAD_PAD), F32)
    for h in range(MLA_HEADS):
        place = place.at[r, h * HEAD_PAD + QK_NOPE + r].set(1.0)
    return place.astype(BF16)


def _split_in_proj(w):
    zp, cq, ckv, kr, zc, zs, zg = jnp.split(
        w, (256, 512, 640, 672, 1184, 1696), axis=1)
    pad = jnp.zeros((D_MODEL, LANES - 2 * QK_ROPE), w.dtype)
    small = jnp.concatenate([zc, zs, zp, cq, ckv, kr, _rotate_half_cols(kr), pad], axis=1)
    return small.astype(BF16), zg.astype(BF16)


def _mla_weights(w_uq, w_ukv):
    wq = w_uq.reshape(Q_LORA, MLA_HEADS, QK_NOPE + QK_ROPE)
    nope, rope = wq[..., :QK_NOPE], wq[..., QK_NOPE:]
    zq = jnp.zeros((Q_LORA, MLA_HEADS, HEAD_PAD - QK_NOPE - QK_ROPE), w_uq.dtype)
    wqa = jnp.concatenate([nope, rope, zq], axis=-1).reshape(Q_LORA, MLA_HEADS * HEAD_PAD)
    wqb = jnp.concatenate([jnp.zeros_like(nope), _rotate_half_cols(rope), zq], axis=-1)
    wqb = wqb.reshape(Q_LORA, MLA_HEADS * HEAD_PAD)
    wkv = w_ukv.reshape(KV_LORA, MLA_HEADS, QK_NOPE + V_HEAD)
    k_nope, v = wkv[..., :QK_NOPE], wkv[..., QK_NOPE:]
    zk = jnp.zeros((KV_LORA, MLA_HEADS, HEAD_PAD - QK_NOPE), w_ukv.dtype)
    wk = jnp.concatenate([k_nope, zk], axis=-1).reshape(KV_LORA, MLA_HEADS * HEAD_PAD)
    zv = jnp.zeros((KV_LORA, MLA_HEADS, V_ROWS - V_HEAD), w_ukv.dtype)
    wv = jnp.concatenate([v, zv], axis=-1).reshape(KV_LORA, MLA_HEADS * V_ROWS)
    return wqa.astype(BF16), wqb.astype(BF16), wk.astype(BF16), wv.astype(BF16)


def kernel(x, norm_mix_g, w_in, b_gate, w_pool, pool_scale, w_pool_out, q_norm_g, w_uq, kv_norm_g, w_ukv, w_mla_out, conv_w, conv_b, conv_ln_g, conv_ln_b, w_conv_out, sgu_ln_g, sgu_ln_b, sgu_w, sgu_b, w_sgu_out, w_o, norm_ffn_g, w_ffn_gate, w_ffn_up, w_ffn_down, w_router, b_router, w_moe_gate, w_moe_up, w_moe_down, final_norm_g):
    batch, seq, d = x.shape
    assert (seq, d) == (SEQ, D_MODEL)
    depth = w_in.shape[0]
    xt = x.reshape(batch * seq, d)
    tabs = _rope_tables()
    place = _rope_placement()
    row = lambda a: a.reshape(1, -1)

    for l in range(depth):
        w_small, w_gate = _split_in_proj(w_in[l])
        zs = _in_proj(xt, row(norm_mix_g[l]), w_small)

        w_bd = _block_diag(w_pool[l])
        m_pool = _pool(zs, w_bd.astype(BF16), row(pool_scale[l]))
        m_conv = _conv(zs, conv_w[l], row(conv_b[l]), row(conv_ln_g[l]), row(conv_ln_b[l]))
        bias_full = jnp.repeat(sgu_b[l].T, SGU_WIDTH // SGU_GROUPS, axis=1)
        m_sgu = _sgu(zs, row(sgu_ln_g[l]), row(sgu_ln_b[l]),
                     sgu_w[l].reshape(SGU_GROUPS * SGU_LEN, SGU_LEN), bias_full)
        wqa, wqb, wk, wv = _mla_weights(w_uq[l], w_ukv[l])
        q, k, v = _mla_prep(zs, row(q_norm_g[l]), row(kv_norm_g[l]), wqa, wqb, wk, wv, place, tabs)
        m_mla = _attention(q, k, v, batch)

        xt = _merge(xt, row(norm_mix_g[l]), m_pool, m_mla, m_conv, m_sgu, w_gate, row(b_gate[l]),
                    w_pool_out[l].astype(BF16), w_mla_out[l].astype(BF16),
                    w_conv_out[l].astype(BF16), w_sgu_out[l].astype(BF16), w_o[l].astype(BF16))

        j = l // 2
        if l % 2 == 0:
            xt = _ffn(xt, row(norm_ffn_g[l]), w_ffn_gate[j].astype(BF16),
                      w_ffn_up[j].astype(BF16), w_ffn_down[j].astype(BF16))
        else:
            wr = jnp.pad(w_router[j], ((0, 0), (0, LANES - N_EXPERTS)))
            br = jnp.pad(b_router[j], (0, LANES - N_EXPERTS)).reshape(1, LANES)
            t = xt.shape[0]
            h, info, cnt = _router(xt, row(norm_ffn_g[l]), wr, br)
            slots, clear, tile_expert, n_tiles = _route_plan(info, cnt, t)
            xs = _dispatch(slots, clear, h, _num_row_tiles(t) * TMG)
            y = _experts(tile_expert, n_tiles, xs, w_moe_gate[j].astype(BF16),
                         w_moe_up[j].astype(BF16), w_moe_down[j].astype(BF16))
            assert l == depth - 1
            xt = _combine(slots, xt, info, row(final_norm_g), y)
    return xt.reshape(batch, seq, d)
```

```python
import functools

import jax
import jax.numpy as jnp
from jax import lax
from jax.experimental import pallas as pl
from jax.experimental.pallas import tpu as pltpu

F32 = jnp.float32
BF16 = jnp.bfloat16

D_MODEL = 1024
SEQ = 8192
CHUNK = 64
POOL_WIDTH = 256
POOL_WINDOWS = (2, 4, 8, 16)
MLA_HEADS = 8
Q_LORA = 256
KV_LORA = 128
QK_NOPE = 64
QK_ROPE = 32
V_HEAD = 64
V_ROWS = 80
ROPE_THETA = 10000.0
CONV_WIDTH = 256
CONV_K = 31
SGU_WIDTH = 256
SGU_GROUPS = 4
SGU_LEN = 128
N_BRANCH = 4
D_FF = 2816
N_EXPERTS = 8
D_FF_EXPERT = 3584
EPS = 1e-6
NEG = -1e30
LOG2E = 1.4426950408889634

LANES = 128
SUBLANES = 8
MXU_TILE = 256
HEAD_PAD = 128
ZC_COL, ZS_COL, ZP_COL, CQ_COL, CKV_COL, KR_COL = 0, 512, 1024, 1280, 1536, 1664
Z_SMALL = 1792
POOL_HALO = 32
CONV_HALO = 32

TM = 512
TQ = 512
TK = 512
TR = 512
BLK = 1024
ROW_UNROLL = 8
TMG = 512
TF_G = 1792
INFO_W = 8
VMEM_LIMIT = 56 * 1024 * 1024


def _params(sem):
    return pltpu.CompilerParams(dimension_semantics=sem, vmem_limit_bytes=VMEM_LIMIT)


def _rms(x, g):
    return x * lax.rsqrt(jnp.mean(x * x, axis=-1, keepdims=True) + EPS) * g


def _layernorm(x, g, b):
    mu = jnp.mean(x, axis=-1, keepdims=True)
    xc = x - mu
    var = jnp.mean(xc * xc, axis=-1, keepdims=True)
    return xc * lax.rsqrt(var + EPS) * g + b


def _sigmoid(x):
    return 0.5 * jnp.tanh(0.5 * x) + 0.5


def _dot(a, b):
    return jnp.dot(a, b, preferred_element_type=F32)


def _mxu_chunks(total, max_chunk):
    assert total % MXU_TILE == 0 and max_chunk % MXU_TILE == 0
    return [slice(s, min(s + max_chunk, total)) for s in range(0, total, max_chunk)]


def _in_proj_kernel(x_ref, g_ref, ws_ref, zs_ref):
    h = _rms(x_ref[...], g_ref[...]).astype(BF16)
    for cs in _mxu_chunks(Z_SMALL, 1024):
        zs_ref[:, cs] = _dot(h, ws_ref[:, cs])


def _in_proj(x, g, w_small):
    t = x.shape[0]
    return pl.pallas_call(
        _in_proj_kernel,
        grid=(t // TM,),
        in_specs=[
            pl.BlockSpec((TM, D_MODEL), lambda i: (i, 0)),
            pl.BlockSpec((1, D_MODEL), lambda i: (0, 0)),
            pl.BlockSpec((D_MODEL, Z_SMALL), lambda i: (0, 0)),
        ],
        out_specs=pl.BlockSpec((TM, Z_SMALL), lambda i: (i, 0)),
        out_shape=jax.ShapeDtypeStruct((t, Z_SMALL), F32),
        compiler_params=_params(("parallel",)),
        name="in_proj",
    )(x, g, w_small)


def _pool_kernel(z_ref, halo_ref, w_ref, scale_ref, o_ref, s1_ref, s2_ref, s4_ref, s8_ref):
    i = pl.program_id(0)
    pos0 = (i * TM) % SEQ
    z = z_ref[...]
    rows = TM + POOL_HALO
    s1_ref[0:POOL_HALO, :] = jnp.where(pos0 == 0, 0.0, halo_ref[...])
    s1_ref[POOL_HALO:, :] = z

    def double(src_ref, dst_ref, shift):
        dst_ref[0:SUBLANES, :] = jnp.zeros((SUBLANES, POOL_WIDTH), F32)
        dst_ref[SUBLANES:, :] = src_ref[SUBLANES:, :] + src_ref[SUBLANES - shift:rows - shift, :]

    double(s1_ref, s2_ref, 1)
    double(s2_ref, s4_ref, 2)
    double(s4_ref, s8_ref, 4)
    s8 = s8_ref[POOL_HALO:, :]
    sums = (s2_ref[POOL_HALO:, :], s4_ref[POOL_HALO:, :], s8, s8 + s8_ref[POOL_HALO - 8:rows - 8, :])
    lane = lax.broadcasted_iota(jnp.int32, (TM, POOL_WIDTH), 1)
    pos = lax.broadcasted_iota(jnp.int32, (TM, POOL_WIDTH), 0) + pos0
    group = lane // (POOL_WIDTH // len(POOL_WINDOWS))
    assert POOL_WINDOWS == (2, 4, 8, 16)
    total = jnp.zeros_like(z)
    win = jnp.zeros_like(lane)
    for gi, w in enumerate(POOL_WINDOWS):
        total = jnp.where(group == gi, sums[gi], total)
        win = jnp.where(group == gi, w, win)
    cnt = jnp.minimum(pos + 1, win).astype(F32)
    pooled = total / cnt - z
    y = _dot(pooled.astype(BF16), w_ref[...]) * scale_ref[...]
    o_ref[...] = y.astype(BF16)


def _pool(zs, w_bd, scale):
    t = zs.shape[0]
    cb = ZP_COL // POOL_WIDTH
    r = TM // POOL_HALO
    return pl.pallas_call(
        _pool_kernel,
        grid=(t // TM,),
        in_specs=[
            pl.BlockSpec((TM, POOL_WIDTH), lambda i: (i, cb)),
            pl.BlockSpec((POOL_HALO, POOL_WIDTH), lambda i: (jnp.maximum(i * r - 1, 0), cb)),
            pl.BlockSpec((POOL_WIDTH, POOL_WIDTH), lambda i: (0, 0)),
            pl.BlockSpec((1, POOL_WIDTH), lambda i: (0, 0)),
        ],
        out_specs=pl.BlockSpec((TM, POOL_WIDTH), lambda i: (i, 0)),
        out_shape=jax.ShapeDtypeStruct((t, POOL_WIDTH), BF16),
        scratch_shapes=[pltpu.VMEM((TM + POOL_HALO, POOL_WIDTH), F32)] * 4,
        compiler_params=_params(("parallel",)),
        name="pool_mixer",
    )(zs, zs, w_bd, scale)


def _glu(z2):
    return z2[:, :CONV_WIDTH] * _sigmoid(z2[:, CONV_WIDTH:])


def _conv_kernel(z_ref, halo_ref, w_ref, b_ref, lg_ref, lb_ref, o_ref, buf_ref, sh_ref):
    i = pl.program_id(0)
    pos0 = (i * TM) % SEQ
    buf_ref[0:CONV_HALO, :] = jnp.where(pos0 == 0, 0.0, _glu(halo_ref[...]))
    buf_ref[CONV_HALO:, :] = _glu(z_ref[...])
    span = TM + CONV_HALO - SUBLANES
    for ph in range(1, SUBLANES):
        sh_ref[ph - 1, 0:span, :] = buf_ref[ph:ph + span, :]
    off = CONV_HALO - (CONV_K - 1)
    y = jnp.zeros((TM, CONV_WIDTH), F32) + b_ref[...]
    for k in range(CONV_K):
        ph = (off + k) % SUBLANES
        r0 = off + k - ph
        tap = buf_ref[r0:r0 + TM, :] if ph == 0 else sh_ref[ph - 1, r0:r0 + TM, :]
        y = y + w_ref[k:k + 1, :] * tap
    yn = _layernorm(y, lg_ref[...], lb_ref[...])
    o_ref[...] = (yn * _sigmoid(yn)).astype(BF16)


def _conv(zs, conv_w, conv_b, ln_g, ln_b):
    t = zs.shape[0]
    cb = ZC_COL // (2 * CONV_WIDTH)
    r = TM // CONV_HALO
    vec = pl.BlockSpec((1, CONV_WIDTH), lambda i: (0, 0))
    return pl.pallas_call(
        _conv_kernel,
        grid=(t // TM,),
        in_specs=[
            pl.BlockSpec((TM, 2 * CONV_WIDTH), lambda i: (i, cb)),
            pl.BlockSpec((CONV_HALO, 2 * CONV_WIDTH), lambda i: (jnp.maximum(i * r - 1, 0), cb)),
            pl.BlockSpec((CONV_K, CONV_WIDTH), lambda i: (0, 0)),
            vec, vec, vec,
        ],
        out_specs=pl.BlockSpec((TM, CONV_WIDTH), lambda i: (i, 0)),
        out_shape=jax.ShapeDtypeStruct((t, CONV_WIDTH), BF16),
        scratch_shapes=[pltpu.VMEM((TM + CONV_HALO, CONV_WIDTH), F32),
                        pltpu.VMEM((SUBLANES - 1, TM + CONV_HALO, CONV_WIDTH), F32)],
        compiler_params=_params(("parallel",)),
        name="conv_mixer",
    )(zs, zs, conv_w, conv_b, ln_g, ln_b)


def _sgu_kernel(z_ref, lg_ref, lb_ref, ws_ref, bias_ref, o_ref):
    z = z_ref[...]
    z = 0.5 * z * (1.0 + lax.erf(z * (2.0 ** -0.5)))
    u = z[:, :SGU_WIDTH]
    v = _layernorm(z[:, SGU_WIDTH:], lg_ref[...], lb_ref[...]).astype(BF16)
    rows = SGU_GROUPS * SGU_LEN
    row = lax.broadcasted_iota(jnp.int32, (rows, SGU_LEN), 0) % SGU_LEN
    col = lax.broadcasted_iota(jnp.int32, (rows, SGU_LEN), 1)
    ws = jnp.where(col <= row, ws_ref[...], 0.0).astype(BF16)
    lane_group = lax.broadcasted_iota(jnp.int32, (SGU_LEN, SGU_WIDTH), 1) // (SGU_WIDTH // SGU_GROUPS)
    for blk in range(TM // SGU_LEN):
        r0 = blk * SGU_LEN
        full = _dot(ws, v[r0:r0 + SGU_LEN, :])
        mixed = full[0:SGU_LEN, :]
        for g in range(1, SGU_GROUPS):
            mixed = jnp.where(lane_group == g, full[g * SGU_LEN:(g + 1) * SGU_LEN, :], mixed)
        mixed = mixed + bias_ref[...]
        o_ref[r0:r0 + SGU_LEN, :] = (u[r0:r0 + SGU_LEN, :] * mixed).astype(BF16)


def _sgu(zs, ln_g, ln_b, ws_stack, bias_full):
    t = zs.shape[0]
    cb = ZS_COL // (2 * SGU_WIDTH)
    vec = pl.BlockSpec((1, SGU_WIDTH), lambda i: (0, 0))
    return pl.pallas_call(
        _sgu_kernel,
        grid=(t // TM,),
        in_specs=[
            pl.BlockSpec((TM, 2 * SGU_WIDTH), lambda i: (i, cb)),
            vec, vec,
            pl.BlockSpec((SGU_GROUPS * SGU_LEN, SGU_LEN), lambda i: (0, 0)),
            pl.BlockSpec((SGU_LEN, SGU_WIDTH), lambda i: (0, 0)),
        ],
        out_specs=pl.BlockSpec((TM, SGU_WIDTH), lambda i: (i, 0)),
        out_shape=jax.ShapeDtypeStruct((t, SGU_WIDTH), BF16),
        compiler_params=_params(("parallel",)),
        name="sgu_mixer",
    )(zs, ln_g, ln_b, ws_stack, bias_full)


def _tile_heads(tab):
    return jnp.concatenate([tab] * MLA_HEADS, axis=1)


def _mla_prep_kernel(cq_ref, ckv_ref, kr_ref, qg_ref, kvg_ref, wqa_ref, wqb_ref, wk_ref, wv_ref,
                     place_ref, cq_tab_ref, sq_tab_ref, ck_tab_ref, sk_tab_ref,
                     qt_ref, k_ref, vt_ref):
    cqn = _rms(cq_ref[...], qg_ref[...]).astype(BF16)
    qa = _dot(cqn, wqa_ref[...])
    qb = _dot(cqn, wqb_ref[...])
    scale = (QK_NOPE + QK_ROPE) ** -0.5 * LOG2E
    q = (qa * _tile_heads(cq_tab_ref[...]) + qb * _tile_heads(sq_tab_ref[...])) * scale
    qt_ref[...] = q.T.astype(BF16)
    kvn = _rms(ckv_ref[...], kvg_ref[...]).astype(BF16)
    vt = _dot(kvn, wv_ref[...]).T
    ones_row = lax.broadcasted_iota(jnp.int32, vt.shape, 0) % V_ROWS == V_HEAD
    vt = jnp.where(ones_row, 1.0, vt)
    for c in range(TM // TK):
        vt_ref[c] = vt[:, c * TK:(c + 1) * TK].astype(BF16)
    kr = kr_ref[...]
    kro = kr * ck_tab_ref[...] + pltpu.roll(kr, LANES - QK_ROPE, 1) * sk_tab_ref[...]
    k = _dot(kvn, wk_ref[...]) + _dot(kro.astype(BF16), place_ref[...])
    k_ref[...] = k.astype(BF16)


def _mla_prep(zs, q_g, kv_g, wqa, wqb, wk, wv, place, tabs):
    t = zs.shape[0]
    hw = MLA_HEADS * HEAD_PAD
    hv = MLA_HEADS * V_ROWS
    ns = SEQ // TM
    tab = pl.BlockSpec((TM, LANES), lambda i: (i % ns, 0))
    full = lambda a: pl.BlockSpec(a.shape, lambda i: (0, 0))
    return pl.pallas_call(
        _mla_prep_kernel,
        grid=(t // TM,),
        in_specs=[
            pl.BlockSpec((TM, Q_LORA), lambda i: (i, CQ_COL // Q_LORA)),
            pl.BlockSpec((TM, KV_LORA), lambda i: (i, CKV_COL // KV_LORA)),
            pl.BlockSpec((TM, LANES), lambda i: (i, KR_COL // LANES)),
            full(q_g), full(kv_g), full(wqa), full(wqb), full(wk), full(wv), full(place),
            tab, tab, tab, tab,
        ],
        out_specs=[
            pl.BlockSpec((hw, TM), lambda i: (0, i)),
            pl.BlockSpec((TM, hw), lambda i: (i, 0)),
            pl.BlockSpec((TM // TK, hv, TK), lambda i: (i, 0, 0)),
        ],
        out_shape=[
            jax.ShapeDtypeStruct((hw, t), BF16),
            jax.ShapeDtypeStruct((t, hw), BF16),
            jax.ShapeDtypeStruct((t // TK, hv, TK), BF16),
        ],
        compiler_params=_params(("parallel",)),
        name="mla_prep",
    )(zs, zs, zs, q_g, kv_g, wqa, wqb, wk, wv, place, *tabs)


def _attn_kernel(qt_ref, k_ref, vt_ref, o_ref, m_ref, acc_ref, st0_ref, st1_ref, mx_ref):
    qi = pl.program_id(1)
    m_ref[...] = jnp.full(m_ref.shape, NEG, F32)
    acc_ref[...] = jnp.zeros(acc_ref.shape, F32)
    bufs = (st0_ref, st1_ref)

    def scores_head(h, j, slot, masked):
        ks = pl.multiple_of(j * TK, TK)
        hs = slice(h * HEAD_PAD, (h + 1) * HEAD_PAD)
        st = _dot(k_ref[pl.ds(ks, TK), hs], qt_ref[hs, :])
        if masked:
            key_chunk = lax.broadcasted_iota(jnp.int32, (TK, TQ), 0) // CHUNK
            qry_chunk = lax.broadcasted_iota(jnp.int32, (TK, TQ), 1) // CHUNK
            st = jnp.where(key_chunk <= qry_chunk, st, NEG)
        bufs[slot][h] = st
        mx_ref[slot, h] = jnp.max(st, axis=0, keepdims=True)

    def update_head(h, j, slot):
        m_prev = m_ref[h]
        m_new = jnp.maximum(m_prev, mx_ref[slot, h])
        alpha = jnp.exp2(m_prev - m_new)
        p = jnp.exp2(bufs[slot][h] - m_new).astype(BF16)
        m_ref[h] = m_new
        rows = slice(h * V_ROWS, (h + 1) * V_ROWS)
        acc_ref[rows, :] = alpha * acc_ref[rows, :] + _dot(vt_ref[j, rows, :], p)

    def scores(j, slot, masked):
        for h in range(MLA_HEADS):
            scores_head(h, j, slot, masked)

    def update(j, slot):
        for h in range(MLA_HEADS):
            update_head(h, j, slot)

    def scores_and_update(j_next, slot_next, masked, j, slot):
        for h in range(MLA_HEADS):
            scores_head(h, j_next, slot_next, masked)
            update_head(h, j, slot)

    @pl.when(qi > 0)
    def _():
        scores(0, 0, False)

    def pair(i, carry):
        scores_and_update(2 * i + 1, 1, False, 2 * i, 0)
        scores_and_update(2 * i + 2, 0, False, 2 * i + 1, 1)
        return carry

    pairs = jnp.maximum(qi - 1, 0) // 2
    lax.fori_loop(0, pairs, pair, 0)
    done = 2 * pairs

    @pl.when(qi == 0)
    def _():
        scores(0, 0, True)
        update(0, 0)

    @pl.when(qi % 2 == 1)
    def _():
        scores_and_update(qi, 1, True, done, 0)
        update(qi, 1)

    @pl.when((qi > 0) & (qi % 2 == 0))
    def _():
        scores_and_update(qi - 1, 1, False, done, 0)
        scores_and_update(qi, 0, True, qi - 1, 1)
        update(qi, 0)

    outs = [acc_ref[h * V_ROWS:h * V_ROWS + V_HEAD, :] / acc_ref[h * V_ROWS + V_HEAD:h * V_ROWS + V_HEAD + 1, :]
            for h in range(MLA_HEADS)]
    o_ref[...] = jnp.concatenate(outs, axis=0).T.astype(BF16)


def _attention(qt, k, vt, batch):
    t = k.shape[0]
    nq = SEQ // TQ
    nk = SEQ // TK
    hw = MLA_HEADS * HEAD_PAD
    hv = MLA_HEADS * V_HEAD
    hr = MLA_HEADS * V_ROWS
    once = pl.Buffered(1)
    return pl.pallas_call(
        _attn_kernel,
        grid=(batch, nq),
        in_specs=[
            pl.BlockSpec((hw, TQ), lambda b, i: (0, b * nq + i)),
            pl.BlockSpec((SEQ, hw), lambda b, i: (b, 0), pipeline_mode=once),
            pl.BlockSpec((nk, hr, TK), lambda b, i: (b, 0, 0), pipeline_mode=once),
        ],
        out_specs=pl.BlockSpec((TQ, hv), lambda b, i: (b * nq + i, 0)),
        out_shape=jax.ShapeDtypeStruct((t, hv), BF16),
        scratch_shapes=[
            pltpu.VMEM((MLA_HEADS, 1, TQ), F32),
            pltpu.VMEM((hr, TQ), F32),
            pltpu.VMEM((MLA_HEADS, TK, TQ), F32),
            pltpu.VMEM((MLA_HEADS, TK, TQ), F32),
            pltpu.VMEM((2, MLA_HEADS, 1, TQ), F32),
        ],
        compiler_params=_params(("parallel", "arbitrary")),
        name="mla_attention",
    )(qt, k, vt)


def _merge_kernel(x_ref, ng_ref, mp_ref, ma_ref, mc_ref, ms_ref, wgate_ref, bg_ref,
                  wp_ref, wa_ref, wc_ref, ws_ref, wo_ref, o_ref, mg_ref):
    x = x_ref[...]
    h = _rms(x, ng_ref[...]).astype(BF16)
    branches = ((mp_ref, wp_ref), (ma_ref, wa_ref), (mc_ref, wc_ref), (ms_ref, ws_ref))
    half = D_MODEL // 2
    for n0 in range(0, D_MODEL, half):
        merged = jnp.zeros((TM, half), F32)
        for b, (m_ref, w_ref) in enumerate(branches):
            c0 = b * D_MODEL + n0
            gate = _sigmoid(_dot(h, wgate_ref[:, c0:c0 + half]) + bg_ref[:, c0:c0 + half])
            merged = merged + gate * _dot(m_ref[...], w_ref[:, n0:n0 + half])
        mg_ref[:, n0:n0 + half] = merged.astype(BF16)
    o_ref[...] = x + _dot(mg_ref[...], wo_ref[...])


def _merge(x, norm_g, mp, ma, mc, ms, w_gate, b_gate, wp, wa, wc, ws, wo):
    t = x.shape[0]
    row = lambda w: pl.BlockSpec((TM, w), lambda i: (i, 0))
    full = lambda a: pl.BlockSpec(a.shape, lambda i: (0, 0))
    return pl.pallas_call(
        _merge_kernel,
        grid=(t // TM,),
        in_specs=[row(D_MODEL), full(norm_g), row(POOL_WIDTH), row(MLA_HEADS * V_HEAD), row(CONV_WIDTH),
                  row(SGU_WIDTH), full(w_gate), full(b_gate), full(wp), full(wa), full(wc), full(ws), full(wo)],
        out_specs=row(D_MODEL),
        out_shape=jax.ShapeDtypeStruct((t, D_MODEL), F32),
        scratch_shapes=[pltpu.VMEM((TM, D_MODEL), BF16)],
        compiler_params=_params(("parallel",)),
        name="merge_out",
    )(x, norm_g, mp, ma, mc, ms, w_gate, b_gate, wp, wa, wc, ws, wo)


def _ffn_kernel(x_ref, g_ref, wg_ref, wu_ref, wd_ref, o_ref):
    x = x_ref[...]
    h = _rms(x, g_ref[...]).astype(BF16)
    acc = x
    for cs in _mxu_chunks(D_FF, 1536):
        gate = _dot(h, wg_ref[:, cs])
        up = _dot(h, wu_ref[:, cs])
        act = (gate * _sigmoid(gate) * up).astype(BF16)
        acc = acc + _dot(act, wd_ref[cs, :])
    o_ref[...] = acc


def _ffn(x, g, wg, wu, wd):
    t = x.shape[0]
    full = lambda a: pl.BlockSpec(a.shape, lambda i: (0, 0))
    return pl.pallas_call(
        _ffn_kernel,
        grid=(t // TM,),
        in_specs=[pl.BlockSpec((TM, D_MODEL), lambda i: (i, 0)), full(g), full(wg), full(wu), full(wd)],
        out_specs=pl.BlockSpec((TM, D_MODEL), lambda i: (i, 0)),
        out_shape=jax.ShapeDtypeStruct((t, D_MODEL), F32),
        compiler_params=_params(("parallel",)),
        name="ffn_dense",
    )(x, g, wg, wu, wd)


def _router_kernel(x_ref, g_ref, wr_ref, br_ref, h_ref, info_ref, info_t_ref, cnt_ref, carry_ref):
    @pl.when(pl.program_id(0) == 0)
    def _():
        carry_ref[...] = jnp.zeros(carry_ref.shape, F32)

    h = _rms(x_ref[...], g_ref[...])
    h_ref[...] = h
    w = wr_ref[...]
    h_hi = h.astype(BF16)
    w_hi = w.astype(BF16)
    h_lo = (h - h_hi.astype(F32)).astype(BF16)
    w_lo = (w - w_hi.astype(F32)).astype(BF16)
    logits = _dot(h_hi, w_hi) + _dot(h_lo, w_hi) + _dot(h_hi, w_lo) + br_ref[...]
    lane = lax.broadcasted_iota(jnp.int32, logits.shape, 1).astype(F32)
    logits = jnp.where(lane < N_EXPERTS, logits, NEG)
    v1 = jnp.max(logits, axis=1, keepdims=True)
    i1 = jnp.min(jnp.where(logits == v1, lane, float(LANES)), axis=1, keepdims=True)
    rest = jnp.where(lane == i1, NEG, logits)
    v2 = jnp.max(rest, axis=1, keepdims=True)
    i2 = jnp.min(jnp.where(rest == v2, lane, float(LANES)), axis=1, keepdims=True)
    e = jnp.exp(v2 - v1)
    w1 = 1.0 / (1.0 + e)
    w2 = e / (1.0 + e)
    chosen = jnp.where((lane == i1) | (lane == i2), 1.0, 0.0)
    row = lax.broadcasted_iota(jnp.int32, (TR, TR), 0)
    col = lax.broadcasted_iota(jnp.int32, (TR, TR), 1)
    before = jnp.where(col < row, 1.0, 0.0)
    pos = _dot(before, chosen) + carry_ref[0:1, :]
    r1 = jnp.sum(jnp.where(lane == i1, pos, 0.0), axis=1, keepdims=True)
    r2 = jnp.sum(jnp.where(lane == i2, pos, 0.0), axis=1, keepdims=True)
    carry_ref[0:1, :] = carry_ref[0:1, :] + jnp.sum(chosen, axis=0, keepdims=True)
    cnt_ref[...] = carry_ref[...]
    fields = (i1, i2, r1, r2, w1, w2)
    info = jnp.zeros(logits.shape, F32)
    for n, val in enumerate(fields):
        info = jnp.where(lane == n, val, info)
    info_ref[...] = info[:, :INFO_W]
    info_t_ref[...] = info.T[:INFO_W, :]


def _router(x, g, wr, br):
    t = x.shape[0]
    full = lambda a: pl.BlockSpec(a.shape, lambda i: (0, 0))
    return pl.pallas_call(
        _router_kernel,
        grid=(t // TR,),
        in_specs=[pl.BlockSpec((TR, D_MODEL), lambda i: (i, 0)), full(g), full(wr), full(br)],
        out_specs=[
            pl.BlockSpec((TR, D_MODEL), lambda i: (i, 0)),
            pl.BlockSpec((TR, INFO_W), lambda i: (i, 0)),
            pl.BlockSpec((INFO_W, TR), lambda i: (0, i)),
            pl.BlockSpec((8, LANES), lambda i: (0, 0)),
        ],
        out_shape=[
            jax.ShapeDtypeStruct((t, D_MODEL), F32),
            jax.ShapeDtypeStruct((t, INFO_W), F32),
            jax.ShapeDtypeStruct((INFO_W, t), F32),
            jax.ShapeDtypeStruct((8, LANES), F32),
        ],
        scratch_shapes=[pltpu.VMEM((8, LANES), F32)],
        compiler_params=_params(("arbitrary",)),
        name="moe_router",
    )(x, g, wr, br)


def _num_row_tiles(t):
    return (2 * t) // TMG + N_EXPERTS


def _route_plan(info_t, cnt, t):
    counts = cnt[0, :N_EXPERTS].astype(jnp.int32)
    padded = ((counts + TMG - 1) // TMG) * TMG
    ends = jnp.cumsum(padded)
    off = ends - padded
    slot = off[info_t[0:2].astype(jnp.int32)] + info_t[2:4].astype(jnp.int32)
    slots = slot.reshape(2, t // BLK, BLK).transpose(1, 0, 2).reshape(-1)
    starts = jnp.arange(_num_row_tiles(t), dtype=jnp.int32) * TMG
    first_row = jnp.minimum(starts, ends[-1] - 1)
    tile_expert = jnp.sum(ends[None, :] <= first_row[:, None], axis=1)
    n_tiles = (ends[-1] // TMG).reshape(1)
    last_tile = jnp.where(padded > 0, ends - TMG, -1)
    tail = starts[-N_EXPERTS:]
    clear = jnp.concatenate([last_tile, jnp.where(tail >= ends[-1], tail, -1)])
    return (slots.astype(jnp.int32), clear.astype(jnp.int32),
            tile_expert.astype(jnp.int32), n_tiles.astype(jnp.int32))


def _row_copy(src_ref, src_row, dst_ref, dst_row, sem):
    return pltpu.make_async_copy(src_ref.at[pl.ds(src_row, 1), :], dst_ref.at[pl.ds(dst_row, 1), :], sem)


def _dispatch_kernel(slots_ref, clear_ref, h_ref, xs_ref, zero_ref, sem):
    @pl.when(pl.program_id(0) == 0)
    def _():
        zero_ref[...] = jnp.zeros(zero_ref.shape, F32)
        for n in range(2 * N_EXPERTS):
            @pl.when(clear_ref[n] >= 0)
            def _():
                start = pl.multiple_of(clear_ref[n], TMG)
                clear = pltpu.make_async_copy(zero_ref, xs_ref.at[pl.ds(start, TMG), :], sem)
                clear.start()
                clear.wait()

    base = pl.program_id(0) * (2 * BLK)

    def body(r, carry):
        for j in range(2):
            _row_copy(h_ref, r, xs_ref, slots_ref[base + j * BLK + r], sem).start()
        return carry

    lax.fori_loop(0, BLK, body, 0, unroll=ROW_UNROLL)
    for j in range(2):
        pltpu.make_async_copy(h_ref, xs_ref.at[pl.ds(0, BLK), :], sem).wait()


def _dispatch(slots, clear, h, n_rows):
    t = h.shape[0]
    return pl.pallas_call(
        _dispatch_kernel,
        grid_spec=pltpu.PrefetchScalarGridSpec(
            num_scalar_prefetch=2,
            grid=(t // BLK,),
            in_specs=[pl.BlockSpec((BLK, D_MODEL), lambda i, s, c: (i, 0))],
            out_specs=pl.BlockSpec(memory_space=pl.ANY),
            scratch_shapes=[pltpu.VMEM((TMG, D_MODEL), F32), pltpu.SemaphoreType.DMA(())],
        ),
        out_shape=jax.ShapeDtypeStruct((n_rows, D_MODEL), F32),
        compiler_params=_params(("arbitrary",)),
        name="moe_dispatch",
    )(slots, clear, h)


def _experts_kernel(te_ref, nt_ref, xs_ref, wg_ref, wu_ref, wd_ref, y_ref, xb_ref, acc_ref):
    del te_ref
    f = pl.program_id(1)
    active = pl.program_id(0) < nt_ref[0]

    @pl.when(f == 0)
    def _():
        xb_ref[...] = xs_ref[...].astype(BF16)
        acc_ref[...] = jnp.zeros(acc_ref.shape, F32)

    @pl.when(active)
    def _():
        h = xb_ref[...]
        part = jnp.zeros((TMG, D_MODEL), F32)
        for cs in _mxu_chunks(TF_G, 1024):
            gate = _dot(h, wg_ref[:, cs])
            up = _dot(h, wu_ref[:, cs])
            part = part + _dot((gate * _sigmoid(gate) * up).astype(BF16), wd_ref[cs, :])
        acc_ref[...] += part

    @pl.when(f == pl.num_programs(1) - 1)
    def _():
        y_ref[...] = acc_ref[...]


def _experts(tile_expert, n_tiles, xs, wg, wu, wd):
    n_rows = xs.shape[0]
    nf = D_FF_EXPERT // TF_G

    def f_eff(g, f, nt):
        return jnp.where(g < nt[0], f, nf - 1)

    return pl.pallas_call(
        _experts_kernel,
        grid_spec=pltpu.PrefetchScalarGridSpec(
            num_scalar_prefetch=2,
            grid=(n_rows // TMG, nf),
            in_specs=[
                pl.BlockSpec((TMG, D_MODEL), lambda g, f, te, nt: (g, 0)),
                pl.BlockSpec((None, D_MODEL, TF_G), lambda g, f, te, nt: (te[g], 0, f_eff(g, f, nt))),
                pl.BlockSpec((None, D_MODEL, TF_G), lambda g, f, te, nt: (te[g], 0, f_eff(g, f, nt))),
                pl.BlockSpec((None, TF_G, D_MODEL), lambda g, f, te, nt: (te[g], f_eff(g, f, nt), 0)),
            ],
            out_specs=pl.BlockSpec((TMG, D_MODEL), lambda g, f, te, nt: (g, 0)),
            scratch_shapes=[pltpu.VMEM((TMG, D_MODEL), BF16), pltpu.VMEM((TMG, D_MODEL), F32)],
        ),
        out_shape=jax.ShapeDtypeStruct((n_rows, D_MODEL), F32),
        compiler_params=_params(("arbitrary", "arbitrary")),
        name="moe_experts",
    )(tile_expert, n_tiles, xs, wg, wu, wd)


def _combine_kernel(slots_ref, x_ref, info_ref, fg_ref, y_ref, o_ref, rows_ref, sem):
    base = pl.program_id(0) * (2 * BLK)

    def body(r, carry):
        for j in range(2):
            _row_copy(y_ref, slots_ref[base + j * BLK + r], rows_ref.at[j], r, sem).start()
        return carry

    lax.fori_loop(0, BLK, body, 0, unroll=ROW_UNROLL)
    for j in range(2):
        pltpu.make_async_copy(y_ref.at[pl.ds(0, BLK), :], rows_ref.at[j], sem).wait()
    w1 = info_ref[:, 4:5]
    w2 = info_ref[:, 5:6]
    o_ref[...] = _rms(x_ref[...] + w1 * rows_ref[0] + w2 * rows_ref[1], fg_ref[...])


def _combine(slots, x, info, final_g, y):
    t = x.shape[0]
    return pl.pallas_call(
        _combine_kernel,
        grid_spec=pltpu.PrefetchScalarGridSpec(
            num_scalar_prefetch=1,
            grid=(t // BLK,),
            in_specs=[pl.BlockSpec((BLK, D_MODEL), lambda i, s: (i, 0)),
                      pl.BlockSpec((BLK, INFO_W), lambda i, s: (i, 0)),
                      pl.BlockSpec((1, D_MODEL), lambda i, s: (0, 0)),
                      pl.BlockSpec(memory_space=pl.ANY)],
            out_specs=pl.BlockSpec((BLK, D_MODEL), lambda i, s: (i, 0)),
            scratch_shapes=[pltpu.VMEM((2, BLK, D_MODEL), F32), pltpu.SemaphoreType.DMA(())],
        ),
        out_shape=jax.ShapeDtypeStruct((t, D_MODEL), F32),
        compiler_params=_params(("arbitrary",)),
        name="moe_combine",
    )(slots, x, info, final_g, y)


def _block_diag(w):
    g, c, _ = w.shape
    eye = jnp.eye(g, dtype=w.dtype)
    return (w[:, :, None, :] * eye[:, None, :, None]).reshape(g * c, g * c)


def _rotate_half_cols(w):
    half = w.shape[-1] // 2
    return jnp.concatenate([-w[..., half:], w[..., :half]], axis=-1)


def _rope_tables():
    pos = jnp.arange(SEQ, dtype=F32)
    inv = ROPE_THETA ** (-jnp.arange(0, QK_ROPE, 2, dtype=F32) / QK_ROPE)
    ang = pos[:, None] * inv[None, :]
    cos, sin = jnp.cos(ang), jnp.sin(ang)
    one = jnp.ones((SEQ, QK_NOPE), F32)
    z_nope = jnp.zeros((SEQ, QK_NOPE), F32)
    z_pad = jnp.zeros((SEQ, HEAD_PAD - QK_NOPE - QK_ROPE), F32)
    z_rest = jnp.zeros((SEQ, LANES - QK_ROPE), F32)
    cq_tab = jnp.concatenate([one, cos, cos, z_pad], axis=1)
    sq_tab = jnp.concatenate([z_nope, sin, sin, z_pad], axis=1)
    ck_tab = jnp.concatenate([cos, cos, z_rest], axis=1)
    sk_tab = jnp.concatenate([sin, sin, z_rest], axis=1)
    return cq_tab, sq_tab, ck_tab, sk_tab


def _rope_placement():
    r = jnp.arange(LANES)[:, None]
    c = jnp.arange(MLA_HEADS * HEAD_PAD)[None, :]
    return ((r < QK_ROPE) & (c % HEAD_PAD == QK_NOPE + r)).astype(BF16)


def _split_in_proj(w):
    bounds = (0, 256, 512, 640, 672, 1184, 1696, w.shape[1])
    zp, cq, ckv, kr, zc, zs, zg = [w[:, a:b] for a, b in zip(bounds[:-1], bounds[1:])]
    pad = jnp.zeros((D_MODEL, LANES - 2 * QK_ROPE), w.dtype)
    small = jnp.concatenate([zc, zs, zp, cq, ckv, kr, _rotate_half_cols(kr), pad], axis=1)
    return small.astype(BF16), zg.astype(BF16)


def _mla_weights(w_uq, w_ukv):
    wq = w_uq.reshape(Q_LORA, MLA_HEADS, QK_NOPE + QK_ROPE)
    nope, rope = wq[..., :QK_NOPE], wq[..., QK_NOPE:]
    zq = jnp.zeros((Q_LORA, MLA_HEADS, HEAD_PAD - QK_NOPE - QK_ROPE), w_uq.dtype)
    wqa = jnp.concatenate([nope, rope, zq], axis=-1).reshape(Q_LORA, MLA_HEADS * HEAD_PAD)
    wqb = jnp.concatenate([jnp.zeros_like(nope), _rotate_half_cols(rope), zq], axis=-1)
    wqb = wqb.reshape(Q_LORA, MLA_HEADS * HEAD_PAD)
    wkv = w_ukv.reshape(KV_LORA, MLA_HEADS, QK_NOPE + V_HEAD)
    k_nope, v = wkv[..., :QK_NOPE], wkv[..., QK_NOPE:]
    zk = jnp.zeros((KV_LORA, MLA_HEADS, HEAD_PAD - QK_NOPE), w_ukv.dtype)
    wk = jnp.concatenate([k_nope, zk], axis=-1).reshape(KV_LORA, MLA_HEADS * HEAD_PAD)
    zv = jnp.zeros((KV_LORA, MLA_HEADS, V_ROWS - V_HEAD), w_ukv.dtype)
    wv = jnp.concatenate([v, zv], axis=-1).reshape(KV_LORA, MLA_HEADS * V_ROWS)
    return wqa.astype(BF16), wqb.astype(BF16), wk.astype(BF16), wv.astype(BF16)


def kernel(x, norm_mix_g, w_in, b_gate, w_pool, pool_scale, w_pool_out, q_norm_g, w_uq, kv_norm_g, w_ukv, w_mla_out, conv_w, conv_b, conv_ln_g, conv_ln_b, w_conv_out, sgu_ln_g, sgu_ln_b, sgu_w, sgu_b, w_sgu_out, w_o, norm_ffn_g, w_ffn_gate, w_ffn_up, w_ffn_down, w_router, b_router, w_moe_gate, w_moe_up, w_moe_down, final_norm_g):
    batch, seq, d = x.shape
    assert (seq, d) == (SEQ, D_MODEL)
    depth = w_in.shape[0]
    xt = x.reshape(batch * seq, d)
    tabs = _rope_tables()
    place = _rope_placement()
    row = lambda a: a.reshape(1, -1)

    for l in range(depth):
        w_small, w_gate = _split_in_proj(w_in[l])
        zs = _in_proj(xt, row(norm_mix_g[l]), w_small)

        w_bd = _block_diag(w_pool[l])
        m_pool = _pool(zs, w_bd.astype(BF16), row(pool_scale[l]))
        m_conv = _conv(zs, conv_w[l], row(conv_b[l]), row(conv_ln_g[l]), row(conv_ln_b[l]))
        bias_full = jnp.repeat(sgu_b[l].T, SGU_WIDTH // SGU_GROUPS, axis=1)
        m_sgu = _sgu(zs, row(sgu_ln_g[l]), row(sgu_ln_b[l]),
                     sgu_w[l].reshape(SGU_GROUPS * SGU_LEN, SGU_LEN), bias_full)
        wqa, wqb, wk, wv = _mla_weights(w_uq[l], w_ukv[l])
        q, k, v = _mla_prep(zs, row(q_norm_g[l]), row(kv_norm_g[l]), wqa, wqb, wk, wv, place, tabs)
        m_mla = _attention(q, k, v, batch)

        xt = _merge(xt, row(norm_mix_g[l]), m_pool, m_mla, m_conv, m_sgu, w_gate, row(b_gate[l]),
                    w_pool_out[l].astype(BF16), w_mla_out[l].astype(BF16),
                    w_conv_out[l].astype(BF16), w_sgu_out[l].astype(BF16), w_o[l].astype(BF16))

        j = l // 2
        if l % 2 == 0:
            xt = _ffn(xt, row(norm_ffn_g[l]), w_ffn_gate[j].astype(BF16),
                      w_ffn_up[j].astype(BF16), w_ffn_down[j].astype(BF16))
        else:
            wr = jnp.pad(w_router[j], ((0, 0), (0, LANES - N_EXPERTS)))
            br = jnp.pad(b_router[j], (0, LANES - N_EXPERTS)).reshape(1, LANES)
            t = xt.shape[0]
            h, info, info_t, cnt = _router(xt, row(norm_ffn_g[l]), wr, br)
            slots, clear, tile_expert, n_tiles = _route_plan(info_t, cnt, t)
            xs = _dispatch(slots, clear, h, _num_row_tiles(t) * TMG)
            y = _experts(tile_expert, n_tiles, xs, w_moe_gate[j].astype(BF16),
                         w_moe_up[j].astype(BF16), w_moe_down[j].astype(BF16))
            assert l == depth - 1
            xt = _combine(slots, xt, info, row(final_norm_g), y)
    return xt.reshape(batch, seq, d)
```

```python
import functools

import jax
import jax.numpy as jnp
from jax import lax
from jax.experimental import pallas as pl
from jax.experimental.pallas import tpu as pltpu

F32 = jnp.float32
BF16 = jnp.bfloat16

D_MODEL = 1024
SEQ = 8192
CHUNK = 64
POOL_WIDTH = 256
POOL_WINDOWS = (2, 4, 8, 16)
MLA_HEADS = 8
Q_LORA = 256
KV_LORA = 128
QK_NOPE = 64
QK_ROPE = 32
V_HEAD = 64
V_ROWS = 80
ROPE_THETA = 10000.0
CONV_WIDTH = 256
CONV_K = 31
SGU_WIDTH = 256
SGU_GROUPS = 4
SGU_LEN = 128
N_BRANCH = 4
D_FF = 2816
N_EXPERTS = 8
D_FF_EXPERT = 3584
EPS = 1e-6
NEG = -1e30
LOG2E = 1.4426950408889634

LANES = 128
SUBLANES = 8
MXU_TILE = 256
HEAD_PAD = 128
ZC_COL, ZS_COL, ZP_COL, CQ_COL, CKV_COL, KR_COL = 0, 512, 1024, 1280, 1536, 1664
Z_SMALL = 1792
POOL_HALO = 32
CONV_HALO = 32

TM = 512
TQ = 512
TK = 512
TR = 512
BLK = 1024
ROW_UNROLL = 8
TMG = 512
TF_G = 1792
INFO_W = 8
VMEM_LIMIT = 56 * 1024 * 1024


def _params(sem):
    return pltpu.CompilerParams(dimension_semantics=sem, vmem_limit_bytes=VMEM_LIMIT)


def _rms(x, g):
    return x * lax.rsqrt(jnp.mean(x * x, axis=-1, keepdims=True) + EPS) * g


def _layernorm(x, g, b):
    mu = jnp.mean(x, axis=-1, keepdims=True)
    xc = x - mu
    var = jnp.mean(xc * xc, axis=-1, keepdims=True)
    return xc * lax.rsqrt(var + EPS) * g + b


def _sigmoid(x):
    return 0.5 * jnp.tanh(0.5 * x) + 0.5


def _dot(a, b):
    return jnp.dot(a, b, preferred_element_type=F32)


def _mxu_chunks(total, max_chunk):
    assert total % MXU_TILE == 0 and max_chunk % MXU_TILE == 0
    return [slice(s, min(s + max_chunk, total)) for s in range(0, total, max_chunk)]


def _in_proj_kernel(x_ref, g_ref, ws_ref, zs_ref):
    h = _rms(x_ref[...], g_ref[...]).astype(BF16)
    for cs in _mxu_chunks(Z_SMALL, 1024):
        zs_ref[:, cs] = _dot(h, ws_ref[:, cs])


def _in_proj(x, g, w_small):
    t = x.shape[0]
    return pl.pallas_call(
        _in_proj_kernel,
        grid=(t // TM,),
        in_specs=[
            pl.BlockSpec((TM, D_MODEL), lambda i: (i, 0)),
            pl.BlockSpec((1, D_MODEL), lambda i: (0, 0)),
            pl.BlockSpec((D_MODEL, Z_SMALL), lambda i: (0, 0)),
        ],
        out_specs=pl.BlockSpec((TM, Z_SMALL), lambda i: (i, 0)),
        out_shape=jax.ShapeDtypeStruct((t, Z_SMALL), F32),
        compiler_params=_params(("parallel",)),
        name="in_proj",
    )(x, g, w_small)


def _pool_kernel(z_ref, halo_ref, w_ref, scale_ref, o_ref, s1_ref, s2_ref, s4_ref, s8_ref):
    i = pl.program_id(0)
    pos0 = (i * TM) % SEQ
    z = z_ref[...]
    rows = TM + POOL_HALO
    s1_ref[0:POOL_HALO, :] = jnp.where(pos0 == 0, 0.0, halo_ref[...])
    s1_ref[POOL_HALO:, :] = z

    def double(src_ref, dst_ref, shift):
        dst_ref[0:SUBLANES, :] = jnp.zeros((SUBLANES, POOL_WIDTH), F32)
        dst_ref[SUBLANES:, :] = src_ref[SUBLANES:, :] + src_ref[SUBLANES - shift:rows - shift, :]

    double(s1_ref, s2_ref, 1)
    double(s2_ref, s4_ref, 2)
    double(s4_ref, s8_ref, 4)
    s8 = s8_ref[POOL_HALO:, :]
    sums = (s2_ref[POOL_HALO:, :], s4_ref[POOL_HALO:, :], s8, s8 + s8_ref[POOL_HALO - 8:rows - 8, :])
    lane = lax.broadcasted_iota(jnp.int32, (TM, POOL_WIDTH), 1)
    pos = lax.broadcasted_iota(jnp.int32, (TM, POOL_WIDTH), 0) + pos0
    group = lane // (POOL_WIDTH // len(POOL_WINDOWS))
    assert POOL_WINDOWS == (2, 4, 8, 16)
    total = jnp.zeros_like(z)
    win = jnp.zeros_like(lane)
    for gi, w in enumerate(POOL_WINDOWS):
        total = jnp.where(group == gi, sums[gi], total)
        win = jnp.where(group == gi, w, win)
    cnt = jnp.minimum(pos + 1, win).astype(F32)
    pooled = total / cnt - z
    y = _dot(pooled.astype(BF16), w_ref[...]) * scale_ref[...]
    o_ref[...] = y.astype(BF16)


def _pool(zs, w_bd, scale):
    t = zs.shape[0]
    cb = ZP_COL // POOL_WIDTH
    r = TM // POOL_HALO
    return pl.pallas_call(
        _pool_kernel,
        grid=(t // TM,),
        in_specs=[
            pl.BlockSpec((TM, POOL_WIDTH), lambda i: (i, cb)),
            pl.BlockSpec((POOL_HALO, POOL_WIDTH), lambda i: (jnp.maximum(i * r - 1, 0), cb)),
            pl.BlockSpec((POOL_WIDTH, POOL_WIDTH), lambda i: (0, 0)),
            pl.BlockSpec((1, POOL_WIDTH), lambda i: (0, 0)),
        ],
        out_specs=pl.BlockSpec((TM, POOL_WIDTH), lambda i: (i, 0)),
        out_shape=jax.ShapeDtypeStruct((t, POOL_WIDTH), BF16),
        scratch_shapes=[pltpu.VMEM((TM + POOL_HALO, POOL_WIDTH), F32)] * 4,
        compiler_params=_params(("parallel",)),
        name="pool_mixer",
    )(zs, zs, w_bd, scale)


def _glu(z2):
    return z2[:, :CONV_WIDTH] * _sigmoid(z2[:, CONV_WIDTH:])


def _conv_kernel(z_ref, halo_ref, w_ref, b_ref, lg_ref, lb_ref, o_ref, buf_ref, sh_ref):
    i = pl.program_id(0)
    pos0 = (i * TM) % SEQ
    buf_ref[0:CONV_HALO, :] = jnp.where(pos0 == 0, 0.0, _glu(halo_ref[...]))
    buf_ref[CONV_HALO:, :] = _glu(z_ref[...])
    span = TM + CONV_HALO - SUBLANES
    for ph in range(1, SUBLANES):
        sh_ref[ph - 1, 0:span, :] = buf_ref[ph:ph + span, :]
    off = CONV_HALO - (CONV_K - 1)
    y = jnp.zeros((TM, CONV_WIDTH), F32) + b_ref[...]
    for k in range(CONV_K):
        ph = (off + k) % SUBLANES
        r0 = off + k - ph
        tap = buf_ref[r0:r0 + TM, :] if ph == 0 else sh_ref[ph - 1, r0:r0 + TM, :]
        y = y + w_ref[k:k + 1, :] * tap
    yn = _layernorm(y, lg_ref[...], lb_ref[...])
    o_ref[...] = (yn * _sigmoid(yn)).astype(BF16)


def _conv(zs, conv_w, conv_b, ln_g, ln_b):
    t = zs.shape[0]
    cb = ZC_COL // (2 * CONV_WIDTH)
    r = TM // CONV_HALO
    vec = pl.BlockSpec((1, CONV_WIDTH), lambda i: (0, 0))
    return pl.pallas_call(
        _conv_kernel,
        grid=(t // TM,),
        in_specs=[
            pl.BlockSpec((TM, 2 * CONV_WIDTH), lambda i: (i, cb)),
            pl.BlockSpec((CONV_HALO, 2 * CONV_WIDTH), lambda i: (jnp.maximum(i * r - 1, 0), cb)),
            pl.BlockSpec((CONV_K, CONV_WIDTH), lambda i: (0, 0)),
            vec, vec, vec,
        ],
        out_specs=pl.BlockSpec((TM, CONV_WIDTH), lambda i: (i, 0)),
        out_shape=jax.ShapeDtypeStruct((t, CONV_WIDTH), BF16),
        scratch_shapes=[pltpu.VMEM((TM + CONV_HALO, CONV_WIDTH), F32),
                        pltpu.VMEM((SUBLANES - 1, TM + CONV_HALO, CONV_WIDTH), F32)],
        compiler_params=_params(("parallel",)),
        name="conv_mixer",
    )(zs, zs, conv_w, conv_b, ln_g, ln_b)


def _sgu_kernel(z_ref, lg_ref, lb_ref, ws_ref, bias_ref, o_ref):
    z = z_ref[...]
    z = 0.5 * z * (1.0 + lax.erf(z * (2.0 ** -0.5)))
    u = z[:, :SGU_WIDTH]
    v = _layernorm(z[:, SGU_WIDTH:], lg_ref[...], lb_ref[...]).astype(BF16)
    rows = SGU_GROUPS * SGU_LEN
    row = lax.broadcasted_iota(jnp.int32, (rows, SGU_LEN), 0) % SGU_LEN
    col = lax.broadcasted_iota(jnp.int32, (rows, SGU_LEN), 1)
    ws = jnp.where(col <= row, ws_ref[...], 0.0).astype(BF16)
    lane_group = lax.broadcasted_iota(jnp.int32, (SGU_LEN, SGU_WIDTH), 1) // (SGU_WIDTH // SGU_GROUPS)
    for blk in range(TM // SGU_LEN):
        r0 = blk * SGU_LEN
        full = _dot(ws, v[r0:r0 + SGU_LEN, :])
        mixed = full[0:SGU_LEN, :]
        for g in range(1, SGU_GROUPS):
            mixed = jnp.where(lane_group == g, full[g * SGU_LEN:(g + 1) * SGU_LEN, :], mixed)
        mixed = mixed + bias_ref[...]
        o_ref[r0:r0 + SGU_LEN, :] = (u[r0:r0 + SGU_LEN, :] * mixed).astype(BF16)


def _sgu(zs, ln_g, ln_b, ws_stack, bias_full):
    t = zs.shape[0]
    cb = ZS_COL // (2 * SGU_WIDTH)
    vec = pl.BlockSpec((1, SGU_WIDTH), lambda i: (0, 0))
    return pl.pallas_call(
        _sgu_kernel,
        grid=(t // TM,),
        in_specs=[
            pl.BlockSpec((TM, 2 * SGU_WIDTH), lambda i: (i, cb)),
            vec, vec,
            pl.BlockSpec((SGU_GROUPS * SGU_LEN, SGU_LEN), lambda i: (0, 0)),
            pl.BlockSpec((SGU_LEN, SGU_WIDTH), lambda i: (0, 0)),
        ],
        out_specs=pl.BlockSpec((TM, SGU_WIDTH), lambda i: (i, 0)),
        out_shape=jax.ShapeDtypeStruct((t, SGU_WIDTH), BF16),
        compiler_params=_params(("parallel",)),
        name="sgu_mixer",
    )(zs, ln_g, ln_b, ws_stack, bias_full)


def _tile_heads(tab):
    return jnp.concatenate([tab] * MLA_HEADS, axis=1)


def _mla_prep_kernel(cq_ref, ckv_ref, kr_ref, qg_ref, kvg_ref, wqa_ref, wqb_ref, wk_ref, wv_ref,
                     place_ref, cq_tab_ref, sq_tab_ref, ck_tab_ref, sk_tab_ref,
                     qt_ref, k_ref, vt_ref):
    cqn = _rms(cq_ref[...], qg_ref[...]).astype(BF16)
    qa = _dot(cqn, wqa_ref[...])
    qb = _dot(cqn, wqb_ref[...])
    scale = (QK_NOPE + QK_ROPE) ** -0.5 * LOG2E
    q = (qa * _tile_heads(cq_tab_ref[...]) + qb * _tile_heads(sq_tab_ref[...])) * scale
    qt_ref[...] = q.T.astype(BF16)
    kvn = _rms(ckv_ref[...], kvg_ref[...]).astype(BF16)
    vt = _dot(kvn, wv_ref[...]).T
    ones_row = lax.broadcasted_iota(jnp.int32, vt.shape, 0) % V_ROWS == V_HEAD
    vt = jnp.where(ones_row, 1.0, vt)
    for c in range(TM // TK):
        vt_ref[c] = vt[:, c * TK:(c + 1) * TK].astype(BF16)
    kr = kr_ref[...]
    kro = kr * ck_tab_ref[...] + pltpu.roll(kr, LANES - QK_ROPE, 1) * sk_tab_ref[...]
    k = _dot(kvn, wk_ref[...]) + _dot(kro.astype(BF16), place_ref[...])
    k_ref[...] = k.astype(BF16)


def _mla_prep(zs, q_g, kv_g, wqa, wqb, wk, wv, place, tabs):
    t = zs.shape[0]
    hw = MLA_HEADS * HEAD_PAD
    hv = MLA_HEADS * V_ROWS
    ns = SEQ // TM
    tab = pl.BlockSpec((TM, LANES), lambda i: (i % ns, 0))
    full = lambda a: pl.BlockSpec(a.shape, lambda i: (0, 0))
    return pl.pallas_call(
        _mla_prep_kernel,
        grid=(t // TM,),
        in_specs=[
            pl.BlockSpec((TM, Q_LORA), lambda i: (i, CQ_COL // Q_LORA)),
            pl.BlockSpec((TM, KV_LORA), lambda i: (i, CKV_COL // KV_LORA)),
            pl.BlockSpec((TM, LANES), lambda i: (i, KR_COL // LANES)),
            full(q_g), full(kv_g), full(wqa), full(wqb), full(wk), full(wv), full(place),
            tab, tab, tab, tab,
        ],
        out_specs=[
            pl.BlockSpec((hw, TM), lambda i: (0, i)),
            pl.BlockSpec((TM, hw), lambda i: (i, 0)),
            pl.BlockSpec((TM // TK, hv, TK), lambda i: (i, 0, 0)),
        ],
        out_shape=[
            jax.ShapeDtypeStruct((hw, t), BF16),
            jax.ShapeDtypeStruct((t, hw), BF16),
            jax.ShapeDtypeStruct((t // TK, hv, TK), BF16),
        ],
        compiler_params=_params(("parallel",)),
        name="mla_prep",
    )(zs, zs, zs, q_g, kv_g, wqa, wqb, wk, wv, place, *tabs)


def _attn_kernel(qt_ref, k_ref, vt_ref, o_ref, m_ref, acc_ref, st0_ref, st1_ref, mx_ref):
    qi = pl.program_id(1)
    m_ref[...] = jnp.full(m_ref.shape, NEG, F32)
    acc_ref[...] = jnp.zeros(acc_ref.shape, F32)
    bufs = (st0_ref, st1_ref)

    def scores_head(h, j, slot, masked):
        ks = pl.multiple_of(j * TK, TK)
        hs = slice(h * HEAD_PAD, (h + 1) * HEAD_PAD)
        st = _dot(k_ref[pl.ds(ks, TK), hs], qt_ref[hs, :])
        if masked:
            key_chunk = lax.broadcasted_iota(jnp.int32, (TK, TQ), 0) // CHUNK
            qry_chunk = lax.broadcasted_iota(jnp.int32, (TK, TQ), 1) // CHUNK
            st = jnp.where(key_chunk <= qry_chunk, st, NEG)
        bufs[slot][h] = st
        mx_ref[slot, h] = jnp.max(st, axis=0, keepdims=True)

    def update_head(h, j, slot):
        m_prev = m_ref[h]
        m_new = jnp.maximum(m_prev, mx_ref[slot, h])
        alpha = jnp.exp2(m_prev - m_new)
        p = jnp.exp2(bufs[slot][h] - m_new).astype(BF16)
        m_ref[h] = m_new
        rows = slice(h * V_ROWS, (h + 1) * V_ROWS)
        acc_ref[rows, :] = alpha * acc_ref[rows, :] + _dot(vt_ref[j, rows, :], p)

    def scores(j, slot, masked):
        for h in range(MLA_HEADS):
            scores_head(h, j, slot, masked)

    def update(j, slot):
        for h in range(MLA_HEADS):
            update_head(h, j, slot)

    def scores_and_update(j_next, slot_next, masked, j, slot):
        for h in range(MLA_HEADS):
            scores_head(h, j_next, slot_next, masked)
            update_head(h, j, slot)

    @pl.when(qi > 0)
    def _():
        scores(0, 0, False)

    def pair(i, carry):
        scores_and_update(2 * i + 1, 1, False, 2 * i, 0)
        scores_and_update(2 * i + 2, 0, False, 2 * i + 1, 1)
        return carry

    pairs = jnp.maximum(qi - 1, 0) // 2
    lax.fori_loop(0, pairs, pair, 0)
    done = 2 * pairs

    @pl.when(qi == 0)
    def _():
        scores(0, 0, True)
        update(0, 0)

    @pl.when(qi % 2 == 1)
    def _():
        scores_and_update(qi, 1, True, done, 0)
        update(qi, 1)

    @pl.when((qi > 0) & (qi % 2 == 0))
    def _():
        scores_and_update(qi - 1, 1, False, done, 0)
        scores_and_update(qi, 0, True, qi - 1, 1)
        update(qi, 0)

    outs = [acc_ref[h * V_ROWS:h * V_ROWS + V_HEAD, :] / acc_ref[h * V_ROWS + V_HEAD:h * V_ROWS + V_HEAD + 1, :]
            for h in range(MLA_HEADS)]
    o_ref[...] = jnp.concatenate(outs, axis=0).T.astype(BF16)


def _attention(qt, k, vt, batch):
    t = k.shape[0]
    nq = SEQ // TQ
    nk = SEQ // TK
    hw = MLA_HEADS * HEAD_PAD
    hv = MLA_HEADS * V_HEAD
    hr = MLA_HEADS * V_ROWS
    once = pl.Buffered(1)
    return pl.pallas_call(
        _attn_kernel,
        grid=(batch, nq),
        in_specs=[
            pl.BlockSpec((hw, TQ), lambda b, i: (0, b * nq + i)),
            pl.BlockSpec((SEQ, hw), lambda b, i: (b, 0), pipeline_mode=once),
            pl.BlockSpec((nk, hr, TK), lambda b, i: (b, 0, 0), pipeline_mode=once),
        ],
        out_specs=pl.BlockSpec((TQ, hv), lambda b, i: (b * nq + i, 0)),
        out_shape=jax.ShapeDtypeStruct((t, hv), BF16),
        scratch_shapes=[
            pltpu.VMEM((MLA_HEADS, 1, TQ), F32),
            pltpu.VMEM((hr, TQ), F32),
            pltpu.VMEM((MLA_HEADS, TK, TQ), F32),
            pltpu.VMEM((MLA_HEADS, TK, TQ), F32),
            pltpu.VMEM((2, MLA_HEADS, 1, TQ), F32),
        ],
        compiler_params=_params(("parallel", "arbitrary")),
        name="mla_attention",
    )(qt, k, vt)


def _merge_kernel(x_ref, ng_ref, mp_ref, ma_ref, mc_ref, ms_ref, wgate_ref, bg_ref,
                  wp_ref, wa_ref, wc_ref, ws_ref, wo_ref, o_ref, mg_ref):
    x = x_ref[...]
    h = _rms(x, ng_ref[...]).astype(BF16)
    branches = ((mp_ref, wp_ref), (ma_ref, wa_ref), (mc_ref, wc_ref), (ms_ref, ws_ref))
    half = D_MODEL // 2
    for n0 in range(0, D_MODEL, half):
        merged = jnp.zeros((TM, half), F32)
        for b, (m_ref, w_ref) in enumerate(branches):
            c0 = b * D_MODEL + n0
            gate = _sigmoid(_dot(h, wgate_ref[:, c0:c0 + half]) + bg_ref[:, c0:c0 + half])
            merged = merged + gate * _dot(m_ref[...], w_ref[:, n0:n0 + half])
        mg_ref[:, n0:n0 + half] = merged.astype(BF16)
    o_ref[...] = x + _dot(mg_ref[...], wo_ref[...])


def _merge(x, norm_g, mp, ma, mc, ms, w_gate, b_gate, wp, wa, wc, ws, wo):
    t = x.shape[0]
    row = lambda w: pl.BlockSpec((TM, w), lambda i: (i, 0))
    full = lambda a: pl.BlockSpec(a.shape, lambda i: (0, 0))
    return pl.pallas_call(
        _merge_kernel,
        grid=(t // TM,),
        in_specs=[row(D_MODEL), full(norm_g), row(POOL_WIDTH), row(MLA_HEADS * V_HEAD), row(CONV_WIDTH),
                  row(SGU_WIDTH), full(w_gate), full(b_gate), full(wp), full(wa), full(wc), full(ws), full(wo)],
        out_specs=row(D_MODEL),
        out_shape=jax.ShapeDtypeStruct((t, D_MODEL), F32),
        scratch_shapes=[pltpu.VMEM((TM, D_MODEL), BF16)],
        compiler_params=_params(("parallel",)),
        name="merge_out",
    )(x, norm_g, mp, ma, mc, ms, w_gate, b_gate, wp, wa, wc, ws, wo)


def _ffn_kernel(x_ref, g_ref, wg_ref, wu_ref, wd_ref, o_ref):
    x = x_ref[...]
    h = _rms(x, g_ref[...]).astype(BF16)
    acc = x
    for cs in _mxu_chunks(D_FF, 1536):
        gate = _dot(h, wg_ref[:, cs])
        up = _dot(h, wu_ref[:, cs])
        act = (gate * _sigmoid(gate) * up).astype(BF16)
        acc = acc + _dot(act, wd_ref[cs, :])
    o_ref[...] = acc


def _ffn(x, g, wg, wu, wd):
    t = x.shape[0]
    full = lambda a: pl.BlockSpec(a.shape, lambda i: (0, 0))
    return pl.pallas_call(
        _ffn_kernel,
        grid=(t // TM,),
        in_specs=[pl.BlockSpec((TM, D_MODEL), lambda i: (i, 0)), full(g), full(wg), full(wu), full(wd)],
        out_specs=pl.BlockSpec((TM, D_MODEL), lambda i: (i, 0)),
        out_shape=jax.ShapeDtypeStruct((t, D_MODEL), F32),
        compiler_params=_params(("parallel",)),
        name="ffn_dense",
    )(x, g, wg, wu, wd)


def _router_kernel(x_ref, g_ref, wr_ref, br_ref, h_ref, info_ref, info_t_ref, cnt_ref, carry_ref):
    @pl.when(pl.program_id(0) == 0)
    def _():
        carry_ref[...] = jnp.zeros(carry_ref.shape, F32)

    h = _rms(x_ref[...], g_ref[...])
    h_ref[...] = h
    w = wr_ref[...]
    h_hi = h.astype(BF16)
    w_hi = w.astype(BF16)
    h_lo = (h - h_hi.astype(F32)).astype(BF16)
    w_lo = (w - w_hi.astype(F32)).astype(BF16)
    logits = _dot(h_hi, w_hi) + _dot(h_lo, w_hi) + _dot(h_hi, w_lo) + br_ref[...]
    lane = lax.broadcasted_iota(jnp.int32, logits.shape, 1).astype(F32)
    logits = jnp.where(lane < N_EXPERTS, logits, NEG)
    v1 = jnp.max(logits, axis=1, keepdims=True)
    i1 = jnp.min(jnp.where(logits == v1, lane, float(LANES)), axis=1, keepdims=True)
    rest = jnp.where(lane == i1, NEG, logits)
    v2 = jnp.max(rest, axis=1, keepdims=True)
    i2 = jnp.min(jnp.where(rest == v2, lane, float(LANES)), axis=1, keepdims=True)
    e = jnp.exp(v2 - v1)
    w1 = 1.0 / (1.0 + e)
    w2 = e / (1.0 + e)
    chosen = jnp.where((lane == i1) | (lane == i2), 1.0, 0.0)
    row = lax.broadcasted_iota(jnp.int32, (TR, TR), 0)
    col = lax.broadcasted_iota(jnp.int32, (TR, TR), 1)
    before = jnp.where(col < row, 1.0, 0.0)
    pos = _dot(before, chosen) + carry_ref[0:1, :]
    r1 = jnp.sum(jnp.where(lane == i1, pos, 0.0), axis=1, keepdims=True)
    r2 = jnp.sum(jnp.where(lane == i2, pos, 0.0), axis=1, keepdims=True)
    carry_ref[0:1, :] = carry_ref[0:1, :] + jnp.sum(chosen, axis=0, keepdims=True)
    cnt_ref[...] = carry_ref[...]
    fields = (i1, i2, r1, r2, w1, w2)
    info = jnp.zeros(logits.shape, F32)
    for n, val in enumerate(fields):
        info = jnp.where(lane == n, val, info)
    info_ref[...] = info[:, :INFO_W]
    info_t_ref[...] = info.T[:INFO_W, :]


def _router(x, g, wr, br):
    t = x.shape[0]
    full = lambda a: pl.BlockSpec(a.shape, lambda i: (0, 0))
    return pl.pallas_call(
        _router_kernel,
        grid=(t // TR,),
        in_specs=[pl.BlockSpec((TR, D_MODEL), lambda i: (i, 0)), full(g), full(wr), full(br)],
        out_specs=[
            pl.BlockSpec((TR, D_MODEL), lambda i: (i, 0)),
            pl.BlockSpec((TR, INFO_W), lambda i: (i, 0)),
            pl.BlockSpec((INFO_W, TR), lambda i: (0, i)),
            pl.BlockSpec((8, LANES), lambda i: (0, 0)),
        ],
        out_shape=[
            jax.ShapeDtypeStruct((t, D_MODEL), F32),
            jax.ShapeDtypeStruct((t, INFO_W), F32),
            jax.ShapeDtypeStruct((INFO_W, t), F32),
            jax.ShapeDtypeStruct((8, LANES), F32),
        ],
        scratch_shapes=[pltpu.VMEM((8, LANES), F32)],
        compiler_params=_params(("arbitrary",)),
        name="moe_router",
    )(x, g, wr, br)


def _num_row_tiles(t):
    return (2 * t) // TMG + N_EXPERTS


def _route_plan(info_t, cnt, t):
    counts = cnt[0, :N_EXPERTS].astype(jnp.int32)
    padded = ((counts + TMG - 1) // TMG) * TMG
    ends = jnp.cumsum(padded)
    off = ends - padded
    expert = info_t[0:2].astype(jnp.int32)
    slot = info_t[2:4].astype(jnp.int32)
    for e in range(N_EXPERTS):
        slot = slot + jnp.where(expert == e, off[e], 0)
    slots = slot.reshape(2, t // BLK, BLK).transpose(1, 0, 2).reshape(-1)
    starts = jnp.arange(_num_row_tiles(t), dtype=jnp.int32) * TMG
    first_row = jnp.minimum(starts, ends[-1] - 1)
    tile_expert = jnp.sum(ends[None, :] <= first_row[:, None], axis=1)
    n_tiles = (ends[-1] // TMG).reshape(1)
    last_tile = jnp.where(padded > 0, ends - TMG, -1)
    tail = starts[-N_EXPERTS:]
    clear = jnp.concatenate([last_tile, jnp.where(tail >= ends[-1], tail, -1)])
    return (slots.astype(jnp.int32), clear.astype(jnp.int32),
            tile_expert.astype(jnp.int32), n_tiles.astype(jnp.int32))


def _row_copy(src_ref, src_row, dst_ref, dst_row, sem):
    return pltpu.make_async_copy(src_ref.at[pl.ds(src_row, 1), :], dst_ref.at[pl.ds(dst_row, 1), :], sem)


def _dispatch_kernel(slots_ref, clear_ref, h_ref, xs_ref, zero_ref, sem):
    @pl.when(pl.program_id(0) == 0)
    def _():
        zero_ref[...] = jnp.zeros(zero_ref.shape, F32)
        for n in range(2 * N_EXPERTS):
            @pl.when(clear_ref[n] >= 0)
            def _():
                start = pl.multiple_of(clear_ref[n], TMG)
                clear = pltpu.make_async_copy(zero_ref, xs_ref.at[pl.ds(start, TMG), :], sem)
                clear.start()
                clear.wait()

    base = pl.program_id(0) * (2 * BLK)

    def body(r, carry):
        for j in range(2):
            _row_copy(h_ref, r, xs_ref, slots_ref[base + j * BLK + r], sem).start()
        return carry

    lax.fori_loop(0, BLK, body, 0, unroll=ROW_UNROLL)
    for j in range(2):
        pltpu.make_async_copy(h_ref, xs_ref.at[pl.ds(0, BLK), :], sem).wait()


def _dispatch(slots, clear, h, n_rows):
    t = h.shape[0]
    return pl.pallas_call(
        _dispatch_kernel,
        grid_spec=pltpu.PrefetchScalarGridSpec(
            num_scalar_prefetch=2,
            grid=(t // BLK,),
            in_specs=[pl.BlockSpec((BLK, D_MODEL), lambda i, s, c: (i, 0))],
            out_specs=pl.BlockSpec(memory_space=pl.ANY),
            scratch_shapes=[pltpu.VMEM((TMG, D_MODEL), F32), pltpu.SemaphoreType.DMA(())],
        ),
        out_shape=jax.ShapeDtypeStruct((n_rows, D_MODEL), F32),
        compiler_params=_params(("arbitrary",)),
        name="moe_dispatch",
    )(slots, clear, h)


def _experts_kernel(te_ref, nt_ref, xs_ref, wg_ref, wu_ref, wd_ref, y_ref, xb_ref, acc_ref):
    del te_ref
    f = pl.program_id(1)
    active = pl.program_id(0) < nt_ref[0]

    @pl.when(f == 0)
    def _():
        xb_ref[...] = xs_ref[...].astype(BF16)
        acc_ref[...] = jnp.zeros(acc_ref.shape, F32)

    @pl.when(active)
    def _():
        h = xb_ref[...]
        part = jnp.zeros((TMG, D_MODEL), F32)
        for cs in _mxu_chunks(TF_G, 1024):
            gate = _dot(h, wg_ref[:, cs])
            up = _dot(h, wu_ref[:, cs])
            part = part + _dot((gate * _sigmoid(gate) * up).astype(BF16), wd_ref[cs, :])
        acc_ref[...] += part

    @pl.when(f == pl.num_programs(1) - 1)
    def _():
        y_ref[...] = acc_ref[...]


def _experts(tile_expert, n_tiles, xs, wg, wu, wd):
    n_rows = xs.shape[0]
    nf = D_FF_EXPERT // TF_G

    def f_eff(g, f, nt):
        return jnp.where(g < nt[0], f, nf - 1)

    return pl.pallas_call(
        _experts_kernel,
        grid_spec=pltpu.PrefetchScalarGridSpec(
            num_scalar_prefetch=2,
            grid=(n_rows // TMG, nf),
            in_specs=[
                pl.BlockSpec((TMG, D_MODEL), lambda g, f, te, nt: (g, 0)),
                pl.BlockSpec((None, D_MODEL, TF_G), lambda g, f, te, nt: (te[g], 0, f_eff(g, f, nt))),
                pl.BlockSpec((None, D_MODEL, TF_G), lambda g, f, te, nt: (te[g], 0, f_eff(g, f, nt))),
                pl.BlockSpec((None, TF_G, D_MODEL), lambda g, f, te, nt: (te[g], f_eff(g, f, nt), 0)),
            ],
            out_specs=pl.BlockSpec((TMG, D_MODEL), lambda g, f, te, nt: (g, 0)),
            scratch_shapes=[pltpu.VMEM((TMG, D_MODEL), BF16), pltpu.VMEM((TMG, D_MODEL), F32)],
        ),
        out_shape=jax.ShapeDtypeStruct((n_rows, D_MODEL), F32),
        compiler_params=_params(("arbitrary", "arbitrary")),
        name="moe_experts",
    )(tile_expert, n_tiles, xs, wg, wu, wd)


def _combine_kernel(slots_ref, x_ref, info_ref, fg_ref, y_ref, o_ref, rows_ref, sem):
    base = pl.program_id(0) * (2 * BLK)

    def body(r, carry):
        for j in range(2):
            _row_copy(y_ref, slots_ref[base + j * BLK + r], rows_ref.at[j], r, sem).start()
        return carry

    lax.fori_loop(0, BLK, body, 0, unroll=ROW_UNROLL)
    for j in range(2):
        pltpu.make_async_copy(y_ref.at[pl.ds(0, BLK), :], rows_ref.at[j], sem).wait()
    w1 = info_ref[:, 4:5]
    w2 = info_ref[:, 5:6]
    o_ref[...] = _rms(x_ref[...] + w1 * rows_ref[0] + w2 * rows_ref[1], fg_ref[...])


def _combine(slots, x, info, final_g, y):
    t = x.shape[0]
    return pl.pallas_call(
        _combine_kernel,
        grid_spec=pltpu.PrefetchScalarGridSpec(
            num_scalar_prefetch=1,
            grid=(t // BLK,),
            in_specs=[pl.BlockSpec((BLK, D_MODEL), lambda i, s: (i, 0)),
                      pl.BlockSpec((BLK, INFO_W), lambda i, s: (i, 0)),
                      pl.BlockSpec((1, D_MODEL), lambda i, s: (0, 0)),
                      pl.BlockSpec(memory_space=pl.ANY)],
            out_specs=pl.BlockSpec((BLK, D_MODEL), lambda i, s: (i, 0)),
            scratch_shapes=[pltpu.VMEM((2, BLK, D_MODEL), F32), pltpu.SemaphoreType.DMA(())],
        ),
        out_shape=jax.ShapeDtypeStruct((t, D_MODEL), F32),
        compiler_params=_params(("arbitrary",)),
        name="moe_combine",
    )(slots, x, info, final_g, y)


def _block_diag(w):
    g, c, _ = w.shape
    eye = jnp.eye(g, dtype=w.dtype)
    return (w[:, :, None, :] * eye[:, None, :, None]).reshape(g * c, g * c)


def _rotate_half_cols(w):
    half = w.shape[-1] // 2
    return jnp.concatenate([-w[..., half:], w[..., :half]], axis=-1)


def _rope_tables():
    pos = jnp.arange(SEQ, dtype=F32)
    inv = ROPE_THETA ** (-jnp.arange(0, QK_ROPE, 2, dtype=F32) / QK_ROPE)
    ang = pos[:, None] * inv[None, :]
    cos, sin = jnp.cos(ang), jnp.sin(ang)
    one = jnp.ones((SEQ, QK_NOPE), F32)
    z_nope = jnp.zeros((SEQ, QK_NOPE), F32)
    z_pad = jnp.zeros((SEQ, HEAD_PAD - QK_NOPE - QK_ROPE), F32)
    z_rest = jnp.zeros((SEQ, LANES - QK_ROPE), F32)
    cq_tab = jnp.concatenate([one, cos, cos, z_pad], axis=1)
    sq_tab = jnp.concatenate([z_nope, sin, sin, z_pad], axis=1)
    ck_tab = jnp.concatenate([cos, cos, z_rest], axis=1)
    sk_tab = jnp.concatenate([sin, sin, z_rest], axis=1)
    return cq_tab, sq_tab, ck_tab, sk_tab


def _rope_placement():
    r = jnp.arange(LANES)[:, None]
    c = jnp.arange(MLA_HEADS * HEAD_PAD)[None, :]
    return ((r < QK_ROPE) & (c % HEAD_PAD == QK_NOPE + r)).astype(BF16)


def _split_in_proj(w):
    bounds = (0, 256, 512, 640, 672, 1184, 1696, w.shape[1])
    zp, cq, ckv, kr, zc, zs, zg = [w[:, a:b] for a, b in zip(bounds[:-1], bounds[1:])]
    pad = jnp.zeros((D_MODEL, LANES - 2 * QK_ROPE), w.dtype)
    small = jnp.concatenate([zc, zs, zp, cq, ckv, kr, _rotate_half_cols(kr), pad], axis=1)
    return small.astype(BF16), zg.astype(BF16)


def _mla_weights(w_uq, w_ukv):
    wq = w_uq.reshape(Q_LORA, MLA_HEADS, QK_NOPE + QK_ROPE)
    nope, rope = wq[..., :QK_NOPE], wq[..., QK_NOPE:]
    zq = jnp.zeros((Q_LORA, MLA_HEADS, HEAD_PAD - QK_NOPE - QK_ROPE), w_uq.dtype)
    wqa = jnp.concatenate([nope, rope, zq], axis=-1).reshape(Q_LORA, MLA_HEADS * HEAD_PAD)
    wqb = jnp.concatenate([jnp.zeros_like(nope), _rotate_half_cols(rope), zq], axis=-1)
    wqb = wqb.reshape(Q_LORA, MLA_HEADS * HEAD_PAD)
    wkv = w_ukv.reshape(KV_LORA, MLA_HEADS, QK_NOPE + V_HEAD)
    k_nope, v = wkv[..., :QK_NOPE], wkv[..., QK_NOPE:]
    zk = jnp.zeros((KV_LORA, MLA_HEADS, HEAD_PAD - QK_NOPE), w_ukv.dtype)
    wk = jnp.concatenate([k_nope, zk], axis=-1).reshape(KV_LORA, MLA_HEADS * HEAD_PAD)
    zv = jnp.zeros((KV_LORA, MLA_HEADS, V_ROWS - V_HEAD), w_ukv.dtype)
    wv = jnp.concatenate([v, zv], axis=-1).reshape(KV_LORA, MLA_HEADS * V_ROWS)
    return wqa.astype(BF16), wqb.astype(BF16), wk.astype(BF16), wv.astype(BF16)


def kernel(x, norm_mix_g, w_in, b_gate, w_pool, pool_scale, w_pool_out, q_norm_g, w_uq, kv_norm_g, w_ukv, w_mla_out, conv_w, conv_b, conv_ln_g, conv_ln_b, w_conv_out, sgu_ln_g, sgu_ln_b, sgu_w, sgu_b, w_sgu_out, w_o, norm_ffn_g, w_ffn_gate, w_ffn_up, w_ffn_down, w_router, b_router, w_moe_gate, w_moe_up, w_moe_down, final_norm_g):
    batch, seq, d = x.shape
    assert (seq, d) == (SEQ, D_MODEL)
    depth = w_in.shape[0]
    xt = x.reshape(batch * seq, d)
    tabs = _rope_tables()
    place = _rope_placement()
    row = lambda a: a.reshape(1, -1)

    for l in range(depth):
        w_small, w_gate = _split_in_proj(w_in[l])
        zs = _in_proj(xt, row(norm_mix_g[l]), w_small)

        w_bd = _block_diag(w_pool[l])
        m_pool = _pool(zs, w_bd.astype(BF16), row(pool_scale[l]))
        m_conv = _conv(zs, conv_w[l], row(conv_b[l]), row(conv_ln_g[l]), row(conv_ln_b[l]))
        bias_full = jnp.repeat(sgu_b[l].T, SGU_WIDTH // SGU_GROUPS, axis=1)
        m_sgu = _sgu(zs, row(sgu_ln_g[l]), row(sgu_ln_b[l]),
                     sgu_w[l].reshape(SGU_GROUPS * SGU_LEN, SGU_LEN), bias_full)
        wqa, wqb, wk, wv = _mla_weights(w_uq[l], w_ukv[l])
        q, k, v = _mla_prep(zs, row(q_norm_g[l]), row(kv_norm_g[l]), wqa, wqb, wk, wv, place, tabs)
        m_mla = _attention(q, k, v, batch)

        xt = _merge(xt, row(norm_mix_g[l]), m_pool, m_mla, m_conv, m_sgu, w_gate, row(b_gate[l]),
                    w_pool_out[l].astype(BF16), w_mla_out[l].astype(BF16),
                    w_conv_out[l].astype(BF16), w_sgu_out[l].astype(BF16), w_o[l].astype(BF16))

        j = l // 2
        if l % 2 == 0:
            xt = _ffn(xt, row(norm_ffn_g[l]), w_ffn_gate[j].astype(BF16),
                      w_ffn_up[j].astype(BF16), w_ffn_down[j].astype(BF16))
        else:
            wr = jnp.pad(w_router[j], ((0, 0), (0, LANES - N_EXPERTS)))
            br = jnp.pad(b_router[j], (0, LANES - N_EXPERTS)).reshape(1, LANES)
            t = xt.shape[0]
            h, info, info_t, cnt = _router(xt, row(norm_ffn_g[l]), wr, br)
            slots, clear, tile_expert, n_tiles = _route_plan(info_t, cnt, t)
            xs = _dispatch(slots, clear, h, _num_row_tiles(t) * TMG)
            y = _experts(tile_expert, n_tiles, xs, w_moe_gate[j].astype(BF16),
                         w_moe_up[j].astype(BF16), w_moe_down[j].astype(BF16))
            assert l == depth - 1
            xt = _combine(slots, xt, info, row(final_norm_g), y)
    return xt.reshape(batch, seq, d)
```

```python
import jax
import jax.numpy as jnp
import numpy as np
from jax import lax
from jax.experimental import pallas as pl
from jax.experimental.pallas import tpu as pltpu

F32 = jnp.float32
BF16 = jnp.bfloat16

D_MODEL = 1024
SEQ = 8192
CHUNK = 64
POOL_WIDTH = 256
POOL_WINDOWS = (2, 4, 8, 16)
MLA_HEADS = 8
Q_LORA = 256
KV_LORA = 128
QK_NOPE = 64
QK_ROPE = 32
V_HEAD = 64
V_ROWS = 80
ROPE_THETA = 10000.0
CONV_WIDTH = 256
CONV_K = 31
SGU_WIDTH = 256
SGU_GROUPS = 4
SGU_LEN = 128
N_BRANCH = 4
D_FF = 2816
N_EXPERTS = 8
D_FF_EXPERT = 3584
EPS = 1e-6
NEG = -1e30
LOG2E = 1.4426950408889634

LANES = 128
SUBLANES = 8
MXU_TILE = 256
HEAD_PAD = 128
ZC_COL, ZS_COL, ZP_COL, CQ_COL, CKV_COL, KR_COL = 0, 512, 1024, 1280, 1536, 1664
Z_SMALL = 1792
POOL_HALO = 32
CONV_HALO = 32

TM = 512
TQ = 512
TK = 512
TR = 512
BLK = 1024
ROW_UNROLL = 8
TMG = 512
TF_G = 1792
INFO_W = 8
VMEM_LIMIT = 56 * 1024 * 1024


def _params(sem):
    return pltpu.CompilerParams(dimension_semantics=sem, vmem_limit_bytes=VMEM_LIMIT)


def _rms(x, g):
    return x * lax.rsqrt(jnp.mean(x * x, axis=-1, keepdims=True) + EPS) * g


def _layernorm(x, g, b):
    mu = jnp.mean(x, axis=-1, keepdims=True)
    xc = x - mu
    var = jnp.mean(xc * xc, axis=-1, keepdims=True)
    return xc * lax.rsqrt(var + EPS) * g + b


def _sigmoid(x):
    return 0.5 * jnp.tanh(0.5 * x) + 0.5


def _dot(a, b):
    return jnp.dot(a, b, preferred_element_type=F32)


def _mxu_chunks(total, max_chunk):
    assert total % MXU_TILE == 0 and max_chunk % MXU_TILE == 0
    return [slice(s, min(s + max_chunk, total)) for s in range(0, total, max_chunk)]


def _in_proj_kernel(x_ref, g_ref, ws_ref, zs_ref):
    h = _rms(x_ref[...], g_ref[...]).astype(BF16)
    for cs in _mxu_chunks(Z_SMALL, 1024):
        zs_ref[:, cs] = _dot(h, ws_ref[:, cs])


def _in_proj(x, g, w_small):
    t = x.shape[0]
    return pl.pallas_call(
        _in_proj_kernel,
        grid=(t // TM,),
        in_specs=[
            pl.BlockSpec((TM, D_MODEL), lambda i: (i, 0)),
            pl.BlockSpec((1, D_MODEL), lambda i: (0, 0)),
            pl.BlockSpec((D_MODEL, Z_SMALL), lambda i: (0, 0)),
        ],
        out_specs=pl.BlockSpec((TM, Z_SMALL), lambda i: (i, 0)),
        out_shape=jax.ShapeDtypeStruct((t, Z_SMALL), F32),
        compiler_params=_params(("parallel",)),
        name="in_proj",
    )(x, g, w_small)


def _pool_kernel(z_ref, halo_ref, w_ref, scale_ref, o_ref, s1_ref, s2_ref, s4_ref, s8_ref):
    i = pl.program_id(0)
    pos0 = (i * TM) % SEQ
    z = z_ref[...]
    rows = TM + POOL_HALO
    s1_ref[0:POOL_HALO, :] = jnp.where(pos0 == 0, 0.0, halo_ref[...])
    s1_ref[POOL_HALO:, :] = z

    def double(src_ref, dst_ref, shift):
        dst_ref[0:SUBLANES, :] = jnp.zeros((SUBLANES, POOL_WIDTH), F32)
        dst_ref[SUBLANES:, :] = src_ref[SUBLANES:, :] + src_ref[SUBLANES - shift:rows - shift, :]

    double(s1_ref, s2_ref, 1)
    double(s2_ref, s4_ref, 2)
    double(s4_ref, s8_ref, 4)
    s8 = s8_ref[POOL_HALO:, :]
    sums = (s2_ref[POOL_HALO:, :], s4_ref[POOL_HALO:, :], s8, s8 + s8_ref[POOL_HALO - 8:rows - 8, :])
    lane = lax.broadcasted_iota(jnp.int32, (TM, POOL_WIDTH), 1)
    pos = lax.broadcasted_iota(jnp.int32, (TM, POOL_WIDTH), 0) + pos0
    group = lane // (POOL_WIDTH // len(POOL_WINDOWS))
    assert POOL_WINDOWS == (2, 4, 8, 16)
    total = jnp.zeros_like(z)
    win = jnp.zeros_like(lane)
    for gi, w in enumerate(POOL_WINDOWS):
        total = jnp.where(group == gi, sums[gi], total)
        win = jnp.where(group == gi, w, win)
    cnt = jnp.minimum(pos + 1, win).astype(F32)
    pooled = total / cnt - z
    y = _dot(pooled.astype(BF16), w_ref[...]) * scale_ref[...]
    o_ref[...] = y.astype(BF16)


def _pool(zs, w_bd, scale):
    t = zs.shape[0]
    cb = ZP_COL // POOL_WIDTH
    r = TM // POOL_HALO
    return pl.pallas_call(
        _pool_kernel,
        grid=(t // TM,),
        in_specs=[
            pl.BlockSpec((TM, POOL_WIDTH), lambda i: (i, cb)),
            pl.BlockSpec((POOL_HALO, POOL_WIDTH), lambda i: (jnp.maximum(i * r - 1, 0), cb)),
            pl.BlockSpec((POOL_WIDTH, POOL_WIDTH), lambda i: (0, 0)),
            pl.BlockSpec((1, POOL_WIDTH), lambda i: (0, 0)),
        ],
        out_specs=pl.BlockSpec((TM, POOL_WIDTH), lambda i: (i, 0)),
        out_shape=jax.ShapeDtypeStruct((t, POOL_WIDTH), BF16),
        scratch_shapes=[pltpu.VMEM((TM + POOL_HALO, POOL_WIDTH), F32)] * 4,
        compiler_params=_params(("parallel",)),
        name="pool_mixer",
    )(zs, zs, w_bd, scale)


def _glu(z2):
    return z2[:, :CONV_WIDTH] * _sigmoid(z2[:, CONV_WIDTH:])


def _conv_kernel(z_ref, halo_ref, w_ref, b_ref, lg_ref, lb_ref, o_ref, buf_ref, sh_ref):
    i = pl.program_id(0)
    pos0 = (i * TM) % SEQ
    buf_ref[0:CONV_HALO, :] = jnp.where(pos0 == 0, 0.0, _glu(halo_ref[...]))
    buf_ref[CONV_HALO:, :] = _glu(z_ref[...])
    span = TM + CONV_HALO - SUBLANES
    for ph in range(1, SUBLANES):
        sh_ref[ph - 1, 0:span, :] = buf_ref[ph:ph + span, :]
    off = CONV_HALO - (CONV_K - 1)
    y = jnp.zeros((TM, CONV_WIDTH), F32) + b_ref[...]
    for k in range(CONV_K):
        ph = (off + k) % SUBLANES
        r0 = off + k - ph
        tap = buf_ref[r0:r0 + TM, :] if ph == 0 else sh_ref[ph - 1, r0:r0 + TM, :]
        y = y + w_ref[k:k + 1, :] * tap
    yn = _layernorm(y, lg_ref[...], lb_ref[...])
    o_ref[...] = (yn * _sigmoid(yn)).astype(BF16)


def _conv(zs, conv_w, conv_b, ln_g, ln_b):
    t = zs.shape[0]
    cb = ZC_COL // (2 * CONV_WIDTH)
    r = TM // CONV_HALO
    vec = pl.BlockSpec((1, CONV_WIDTH), lambda i: (0, 0))
    return pl.pallas_call(
        _conv_kernel,
        grid=(t // TM,),
        in_specs=[
            pl.BlockSpec((TM, 2 * CONV_WIDTH), lambda i: (i, cb)),
            pl.BlockSpec((CONV_HALO, 2 * CONV_WIDTH), lambda i: (jnp.maximum(i * r - 1, 0), cb)),
            pl.BlockSpec((CONV_K, CONV_WIDTH), lambda i: (0, 0)),
            vec, vec, vec,
        ],
        out_specs=pl.BlockSpec((TM, CONV_WIDTH), lambda i: (i, 0)),
        out_shape=jax.ShapeDtypeStruct((t, CONV_WIDTH), BF16),
        scratch_shapes=[pltpu.VMEM((TM + CONV_HALO, CONV_WIDTH), F32),
                        pltpu.VMEM((SUBLANES - 1, TM + CONV_HALO, CONV_WIDTH), F32)],
        compiler_params=_params(("parallel",)),
        name="conv_mixer",
    )(zs, zs, conv_w, conv_b, ln_g, ln_b)


def _sgu_kernel(z_ref, lg_ref, lb_ref, ws_ref, bias_ref, o_ref):
    z = z_ref[...]
    z = 0.5 * z * (1.0 + lax.erf(z * (2.0 ** -0.5)))
    u = z[:, :SGU_WIDTH]
    v = _layernorm(z[:, SGU_WIDTH:], lg_ref[...], lb_ref[...]).astype(BF16)
    rows = SGU_GROUPS * SGU_LEN
    row = lax.broadcasted_iota(jnp.int32, (rows, SGU_LEN), 0) % SGU_LEN
    col = lax.broadcasted_iota(jnp.int32, (rows, SGU_LEN), 1)
    ws = jnp.where(col <= row, ws_ref[...], 0.0).astype(BF16)
    lane_group = lax.broadcasted_iota(jnp.int32, (SGU_LEN, SGU_WIDTH), 1) // (SGU_WIDTH // SGU_GROUPS)
    for blk in range(TM // SGU_LEN):
        r0 = blk * SGU_LEN
        full = _dot(ws, v[r0:r0 + SGU_LEN, :])
        mixed = full[0:SGU_LEN, :]
        for g in range(1, SGU_GROUPS):
            mixed = jnp.where(lane_group == g, full[g * SGU_LEN:(g + 1) * SGU_LEN, :], mixed)
        mixed = mixed + bias_ref[...]
        o_ref[r0:r0 + SGU_LEN, :] = (u[r0:r0 + SGU_LEN, :] * mixed).astype(BF16)


def _sgu(zs, ln_g, ln_b, ws_stack, bias_full):
    t = zs.shape[0]
    cb = ZS_COL // (2 * SGU_WIDTH)
    vec = pl.BlockSpec((1, SGU_WIDTH), lambda i: (0, 0))
    return pl.pallas_call(
        _sgu_kernel,
        grid=(t // TM,),
        in_specs=[
            pl.BlockSpec((TM, 2 * SGU_WIDTH), lambda i: (i, cb)),
            vec, vec,
            pl.BlockSpec((SGU_GROUPS * SGU_LEN, SGU_LEN), lambda i: (0, 0)),
            pl.BlockSpec((SGU_LEN, SGU_WIDTH), lambda i: (0, 0)),
        ],
        out_specs=pl.BlockSpec((TM, SGU_WIDTH), lambda i: (i, 0)),
        out_shape=jax.ShapeDtypeStruct((t, SGU_WIDTH), BF16),
        compiler_params=_params(("parallel",)),
        name="sgu_mixer",
    )(zs, ln_g, ln_b, ws_stack, bias_full)


def _tile_heads(tab):
    return jnp.concatenate([tab] * MLA_HEADS, axis=1)


def _mla_prep_kernel(cq_ref, ckv_ref, kr_ref, qg_ref, kvg_ref, wqa_ref, wqb_ref, wk_ref, wv_ref,
                     place_ref, cq_tab_ref, sq_tab_ref, ck_tab_ref, sk_tab_ref,
                     qt_ref, k_ref, vt_ref):
    cqn = _rms(cq_ref[...], qg_ref[...]).astype(BF16)
    qa = _dot(cqn, wqa_ref[...])
    qb = _dot(cqn, wqb_ref[...])
    scale = (QK_NOPE + QK_ROPE) ** -0.5 * LOG2E
    q = (qa * _tile_heads(cq_tab_ref[...]) + qb * _tile_heads(sq_tab_ref[...])) * scale
    qt_ref[...] = q.T.astype(BF16)
    kvn = _rms(ckv_ref[...], kvg_ref[...]).astype(BF16)
    vt = _dot(kvn, wv_ref[...]).T
    ones_row = lax.broadcasted_iota(jnp.int32, vt.shape, 0) % V_ROWS == V_HEAD
    vt = jnp.where(ones_row, 1.0, vt)
    for c in range(TM // TK):
        vt_ref[c] = vt[:, c * TK:(c + 1) * TK].astype(BF16)
    kr = kr_ref[...]
    kro = kr * ck_tab_ref[...] + pltpu.roll(kr, LANES - QK_ROPE, 1) * sk_tab_ref[...]
    k = _dot(kvn, wk_ref[...]) + _dot(kro.astype(BF16), place_ref[...])
    k_ref[...] = k.astype(BF16)


def _mla_prep(zs, q_g, kv_g, wqa, wqb, wk, wv, place, tabs):
    t = zs.shape[0]
    hw = MLA_HEADS * HEAD_PAD
    hv = MLA_HEADS * V_ROWS
    ns = SEQ // TM
    tab = pl.BlockSpec((TM, LANES), lambda i: (i % ns, 0))
    full = lambda a: pl.BlockSpec(a.shape, lambda i: (0, 0))
    return pl.pallas_call(
        _mla_prep_kernel,
        grid=(t // TM,),
        in_specs=[
            pl.BlockSpec((TM, Q_LORA), lambda i: (i, CQ_COL // Q_LORA)),
            pl.BlockSpec((TM, KV_LORA), lambda i: (i, CKV_COL // KV_LORA)),
            pl.BlockSpec((TM, LANES), lambda i: (i, KR_COL // LANES)),
            full(q_g), full(kv_g), full(wqa), full(wqb), full(wk), full(wv), full(place),
            tab, tab, tab, tab,
        ],
        out_specs=[
            pl.BlockSpec((hw, TM), lambda i: (0, i)),
            pl.BlockSpec((TM, hw), lambda i: (i, 0)),
            pl.BlockSpec((TM // TK, hv, TK), lambda i: (i, 0, 0)),
        ],
        out_shape=[
            jax.ShapeDtypeStruct((hw, t), BF16),
            jax.ShapeDtypeStruct((t, hw), BF16),
            jax.ShapeDtypeStruct((t // TK, hv, TK), BF16),
        ],
        compiler_params=_params(("parallel",)),
        name="mla_prep",
    )(zs, zs, zs, q_g, kv_g, wqa, wqb, wk, wv, place, *tabs)


def _attn_kernel(qt_ref, k_ref, vt_ref, o_ref, m_ref, acc_ref, st0_ref, st1_ref, mx_ref):
    qi = pl.program_id(1)
    m_ref[...] = jnp.full(m_ref.shape, NEG, F32)
    acc_ref[...] = jnp.zeros(acc_ref.shape, F32)
    bufs = (st0_ref, st1_ref)

    def scores_head(h, j, slot, masked):
        ks = pl.multiple_of(j * TK, TK)
        hs = slice(h * HEAD_PAD, (h + 1) * HEAD_PAD)
        st = _dot(k_ref[pl.ds(ks, TK), hs], qt_ref[hs, :])
        if masked:
            key_chunk = lax.broadcasted_iota(jnp.int32, (TK, TQ), 0) // CHUNK
            qry_chunk = lax.broadcasted_iota(jnp.int32, (TK, TQ), 1) // CHUNK
            st = jnp.where(key_chunk <= qry_chunk, st, NEG)
        bufs[slot][h] = st
        mx_ref[slot, h] = jnp.max(st, axis=0, keepdims=True)

    def update_head(h, j, slot):
        m_prev = m_ref[h]
        m_new = jnp.maximum(m_prev, mx_ref[slot, h])
        alpha = jnp.exp2(m_prev - m_new)
        p = jnp.exp2(bufs[slot][h] - m_new).astype(BF16)
        m_ref[h] = m_new
        rows = slice(h * V_ROWS, (h + 1) * V_ROWS)
        acc_ref[rows, :] = alpha * acc_ref[rows, :] + _dot(vt_ref[j, rows, :], p)

    def scores(j, slot, masked):
        for h in range(MLA_HEADS):
            scores_head(h, j, slot, masked)

    def update(j, slot):
        for h in range(MLA_HEADS):
            update_head(h, j, slot)

    def scores_and_update(j_next, slot_next, masked, j, slot):
        for h in range(MLA_HEADS):
            scores_head(h, j_next, slot_next, masked)
            update_head(h, j, slot)

    @pl.when(qi > 0)
    def _():
        scores(0, 0, False)

    def pair(i, carry):
        scores_and_update(2 * i + 1, 1, False, 2 * i, 0)
        scores_and_update(2 * i + 2, 0, False, 2 * i + 1, 1)
        return carry

    pairs = jnp.maximum(qi - 1, 0) // 2
    lax.fori_loop(0, pairs, pair, 0)
    done = 2 * pairs

    @pl.when(qi == 0)
    def _():
        scores(0, 0, True)
        update(0, 0)

    @pl.when(qi % 2 == 1)
    def _():
        scores_and_update(qi, 1, True, done, 0)
        update(qi, 1)

    @pl.when((qi > 0) & (qi % 2 == 0))
    def _():
        scores_and_update(qi - 1, 1, False, done, 0)
        scores_and_update(qi, 0, True, qi - 1, 1)
        update(qi, 0)

    outs = [acc_ref[h * V_ROWS:h * V_ROWS + V_HEAD, :] / acc_ref[h * V_ROWS + V_HEAD:h * V_ROWS + V_HEAD + 1, :]
            for h in range(MLA_HEADS)]
    o_ref[...] = jnp.concatenate(outs, axis=0).T.astype(BF16)


def _attention(qt, k, vt, batch):
    t = k.shape[0]
    nq = SEQ // TQ
    nk = SEQ // TK
    hw = MLA_HEADS * HEAD_PAD
    hv = MLA_HEADS * V_HEAD
    hr = MLA_HEADS * V_ROWS
    once = pl.Buffered(1)
    return pl.pallas_call(
        _attn_kernel,
        grid=(batch, nq),
        in_specs=[
            pl.BlockSpec((hw, TQ), lambda b, i: (0, b * nq + i)),
            pl.BlockSpec((SEQ, hw), lambda b, i: (b, 0), pipeline_mode=once),
            pl.BlockSpec((nk, hr, TK), lambda b, i: (b, 0, 0), pipeline_mode=once),
        ],
        out_specs=pl.BlockSpec((TQ, hv), lambda b, i: (b * nq + i, 0)),
        out_shape=jax.ShapeDtypeStruct((t, hv), BF16),
        scratch_shapes=[
            pltpu.VMEM((MLA_HEADS, 1, TQ), F32),
            pltpu.VMEM((hr, TQ), F32),
            pltpu.VMEM((MLA_HEADS, TK, TQ), F32),
            pltpu.VMEM((MLA_HEADS, TK, TQ), F32),
            pltpu.VMEM((2, MLA_HEADS, 1, TQ), F32),
        ],
        compiler_params=_params(("parallel", "arbitrary")),
        name="mla_attention",
    )(qt, k, vt)


def _merge_kernel(x_ref, ng_ref, mp_ref, ma_ref, mc_ref, ms_ref, wgate_ref, bg_ref,
                  wp_ref, wa_ref, wc_ref, ws_ref, wo_ref, o_ref, mg_ref):
    x = x_ref[...]
    h = _rms(x, ng_ref[...]).astype(BF16)
    branches = ((mp_ref, wp_ref), (ma_ref, wa_ref), (mc_ref, wc_ref), (ms_ref, ws_ref))
    half = D_MODEL // 2
    for n0 in range(0, D_MODEL, half):
        merged = jnp.zeros((TM, half), F32)
        for b, (m_ref, w_ref) in enumerate(branches):
            c0 = b * D_MODEL + n0
            gate = _sigmoid(_dot(h, wgate_ref[:, c0:c0 + half]) + bg_ref[:, c0:c0 + half])
            merged = merged + gate * _dot(m_ref[...], w_ref[:, n0:n0 + half])
        mg_ref[:, n0:n0 + half] = merged.astype(BF16)
    o_ref[...] = x + _dot(mg_ref[...], wo_ref[...])


def _merge(x, norm_g, mp, ma, mc, ms, w_gate, b_gate, wp, wa, wc, ws, wo):
    t = x.shape[0]
    row = lambda w: pl.BlockSpec((TM, w), lambda i: (i, 0))
    full = lambda a: pl.BlockSpec(a.shape, lambda i: (0, 0))
    return pl.pallas_call(
        _merge_kernel,
        grid=(t // TM,),
        in_specs=[row(D_MODEL), full(norm_g), row(POOL_WIDTH), row(MLA_HEADS * V_HEAD), row(CONV_WIDTH),
                  row(SGU_WIDTH), full(w_gate), full(b_gate), full(wp), full(wa), full(wc), full(ws), full(wo)],
        out_specs=row(D_MODEL),
        out_shape=jax.ShapeDtypeStruct((t, D_MODEL), F32),
        scratch_shapes=[pltpu.VMEM((TM, D_MODEL), BF16)],
        compiler_params=_params(("parallel",)),
        name="merge_out",
    )(x, norm_g, mp, ma, mc, ms, w_gate, b_gate, wp, wa, wc, ws, wo)


def _ffn_kernel(x_ref, g_ref, wg_ref, wu_ref, wd_ref, o_ref):
    x = x_ref[...]
    h = _rms(x, g_ref[...]).astype(BF16)
    acc = x
    for cs in _mxu_chunks(D_FF, 1536):
        gate = _dot(h, wg_ref[:, cs])
        up = _dot(h, wu_ref[:, cs])
        act = (gate * _sigmoid(gate) * up).astype(BF16)
        acc = acc + _dot(act, wd_ref[cs, :])
    o_ref[...] = acc


def _ffn(x, g, wg, wu, wd):
    t = x.shape[0]
    full = lambda a: pl.BlockSpec(a.shape, lambda i: (0, 0))
    return pl.pallas_call(
        _ffn_kernel,
        grid=(t // TM,),
        in_specs=[pl.BlockSpec((TM, D_MODEL), lambda i: (i, 0)), full(g), full(wg), full(wu), full(wd)],
        out_specs=pl.BlockSpec((TM, D_MODEL), lambda i: (i, 0)),
        out_shape=jax.ShapeDtypeStruct((t, D_MODEL), F32),
        compiler_params=_params(("parallel",)),
        name="ffn_dense",
    )(x, g, wg, wu, wd)


def _router_kernel(x_ref, g_ref, wr_ref, br_ref, h_ref, info_ref, info_t_ref, cnt_ref, carry_ref):
    @pl.when(pl.program_id(0) == 0)
    def _():
        carry_ref[...] = jnp.zeros(carry_ref.shape, F32)

    h = _rms(x_ref[...], g_ref[...])
    h_ref[...] = h
    w = wr_ref[...]
    h_hi = h.astype(BF16)
    w_hi = w.astype(BF16)
    h_lo = (h - h_hi.astype(F32)).astype(BF16)
    w_lo = (w - w_hi.astype(F32)).astype(BF16)
    logits = _dot(h_hi, w_hi) + _dot(h_lo, w_hi) + _dot(h_hi, w_lo) + br_ref[...]
    lane = lax.broadcasted_iota(jnp.int32, logits.shape, 1).astype(F32)
    logits = jnp.where(lane < N_EXPERTS, logits, NEG)
    v1 = jnp.max(logits, axis=1, keepdims=True)
    i1 = jnp.min(jnp.where(logits == v1, lane, float(LANES)), axis=1, keepdims=True)
    rest = jnp.where(lane == i1, NEG, logits)
    v2 = jnp.max(rest, axis=1, keepdims=True)
    i2 = jnp.min(jnp.where(rest == v2, lane, float(LANES)), axis=1, keepdims=True)
    e = jnp.exp(v2 - v1)
    w1 = 1.0 / (1.0 + e)
    w2 = e / (1.0 + e)
    chosen = jnp.where((lane == i1) | (lane == i2), 1.0, 0.0)
    row = lax.broadcasted_iota(jnp.int32, (TR, TR), 0)
    col = lax.broadcasted_iota(jnp.int32, (TR, TR), 1)
    before = jnp.where(col < row, 1.0, 0.0)
    pos = _dot(before, chosen) + carry_ref[0:1, :]
    r1 = jnp.sum(jnp.where(lane == i1, pos, 0.0), axis=1, keepdims=True)
    r2 = jnp.sum(jnp.where(lane == i2, pos, 0.0), axis=1, keepdims=True)
    carry_ref[0:1, :] = carry_ref[0:1, :] + jnp.sum(chosen, axis=0, keepdims=True)
    cnt_ref[...] = carry_ref[...]
    fields = (i1, i2, r1, r2, w1, w2)
    info = jnp.zeros(logits.shape, F32)
    for n, val in enumerate(fields):
        info = jnp.where(lane == n, val, info)
    info_ref[...] = info[:, :INFO_W]
    info_t_ref[...] = info.T[:INFO_W, :]


def _router(x, g, wr, br):
    t = x.shape[0]
    full = lambda a: pl.BlockSpec(a.shape, lambda i: (0, 0))
    return pl.pallas_call(
        _router_kernel,
        grid=(t // TR,),
        in_specs=[pl.BlockSpec((TR, D_MODEL), lambda i: (i, 0)), full(g), full(wr), full(br)],
        out_specs=[
            pl.BlockSpec((TR, D_MODEL), lambda i: (i, 0)),
            pl.BlockSpec((TR, INFO_W), lambda i: (i, 0)),
            pl.BlockSpec((INFO_W, TR), lambda i: (0, i)),
            pl.BlockSpec((8, LANES), lambda i: (0, 0)),
        ],
        out_shape=[
            jax.ShapeDtypeStruct((t, D_MODEL), F32),
            jax.ShapeDtypeStruct((t, INFO_W), F32),
            jax.ShapeDtypeStruct((INFO_W, t), F32),
            jax.ShapeDtypeStruct((8, LANES), F32),
        ],
        scratch_shapes=[pltpu.VMEM((8, LANES), F32)],
        compiler_params=_params(("arbitrary",)),
        name="moe_router",
    )(x, g, wr, br)


def _num_row_tiles(t):
    return (2 * t) // TMG + N_EXPERTS


def _route_plan(info_t, cnt, t):
    counts = cnt[0, :N_EXPERTS].astype(jnp.int32)
    padded = ((counts + TMG - 1) // TMG) * TMG
    ends = jnp.cumsum(padded)
    off = ends - padded
    expert = info_t[0:2].astype(jnp.int32)
    slot = info_t[2:4].astype(jnp.int32)
    for e in range(N_EXPERTS):
        slot = slot + jnp.where(expert == e, off[e], 0)
    slots = slot.reshape(2, t // BLK, BLK).transpose(1, 0, 2).reshape(-1)
    starts = jnp.arange(_num_row_tiles(t), dtype=jnp.int32) * TMG
    first_row = jnp.minimum(starts, ends[-1] - 1)
    tile_expert = jnp.sum(ends[None, :] <= first_row[:, None], axis=1)
    n_tiles = (ends[-1] // TMG).reshape(1)
    last_tile = jnp.where(padded > 0, ends - TMG, -1)
    tail = starts[-N_EXPERTS:]
    clear = jnp.concatenate([last_tile, jnp.where(tail >= ends[-1], tail, -1)])
    return (slots.astype(jnp.int32), clear.astype(jnp.int32),
            tile_expert.astype(jnp.int32), n_tiles.astype(jnp.int32))


def _row_copy(src_ref, src_row, dst_ref, dst_row, sem):
    return pltpu.make_async_copy(src_ref.at[pl.ds(src_row, 1), :], dst_ref.at[pl.ds(dst_row, 1), :], sem)


def _dispatch_kernel(slots_ref, clear_ref, h_ref, xs_ref, zero_ref, sem):
    @pl.when(pl.program_id(0) == 0)
    def _():
        zero_ref[...] = jnp.zeros(zero_ref.shape, F32)
        for n in range(2 * N_EXPERTS):
            @pl.when(clear_ref[n] >= 0)
            def _():
                start = pl.multiple_of(clear_ref[n], TMG)
                clear = pltpu.make_async_copy(zero_ref, xs_ref.at[pl.ds(start, TMG), :], sem)
                clear.start()
                clear.wait()

    base = pl.program_id(0) * (2 * BLK)

    def body(r, carry):
        for j in range(2):
            _row_copy(h_ref, r, xs_ref, slots_ref[base + j * BLK + r], sem).start()
        return carry

    lax.fori_loop(0, BLK, body, 0, unroll=ROW_UNROLL)
    for j in range(2):
        pltpu.make_async_copy(h_ref, xs_ref.at[pl.ds(0, BLK), :], sem).wait()


def _dispatch(slots, clear, h, n_rows):
    t = h.shape[0]
    return pl.pallas_call(
        _dispatch_kernel,
        grid_spec=pltpu.PrefetchScalarGridSpec(
            num_scalar_prefetch=2,
            grid=(t // BLK,),
            in_specs=[pl.BlockSpec((BLK, D_MODEL), lambda i, s, c: (i, 0))],
            out_specs=pl.BlockSpec(memory_space=pl.ANY),
            scratch_shapes=[pltpu.VMEM((TMG, D_MODEL), F32), pltpu.SemaphoreType.DMA(())],
        ),
        out_shape=jax.ShapeDtypeStruct((n_rows, D_MODEL), F32),
        compiler_params=_params(("arbitrary",)),
        name="moe_dispatch",
    )(slots, clear, h)


def _experts_kernel(te_ref, nt_ref, xs_ref, wg_ref, wu_ref, wd_ref, y_ref, xb_ref, acc_ref):
    del te_ref
    f = pl.program_id(1)
    active = pl.program_id(0) < nt_ref[0]

    @pl.when(f == 0)
    def _():
        xb_ref[...] = xs_ref[...].astype(BF16)
        acc_ref[...] = jnp.zeros(acc_ref.shape, F32)

    @pl.when(active)
    def _():
        h = xb_ref[...]
        part = jnp.zeros((TMG, D_MODEL), F32)
        for cs in _mxu_chunks(TF_G, 1024):
            gate = _dot(h, wg_ref[:, cs])
            up = _dot(h, wu_ref[:, cs])
            part = part + _dot((gate * _sigmoid(gate) * up).astype(BF16), wd_ref[cs, :])
        acc_ref[...] += part

    @pl.when(f == pl.num_programs(1) - 1)
    def _():
        y_ref[...] = acc_ref[...]


def _experts(tile_expert, n_tiles, xs, wg, wu, wd):
    n_rows = xs.shape[0]
    nf = D_FF_EXPERT // TF_G

    def f_eff(g, f, nt):
        return jnp.where(g < nt[0], f, nf - 1)

    return pl.pallas_call(
        _experts_kernel,
        grid_spec=pltpu.PrefetchScalarGridSpec(
            num_scalar_prefetch=2,
            grid=(n_rows // TMG, nf),
            in_specs=[
                pl.BlockSpec((TMG, D_MODEL), lambda g, f, te, nt: (g, 0)),
                pl.BlockSpec((None, D_MODEL, TF_G), lambda g, f, te, nt: (te[g], 0, f_eff(g, f, nt))),
                pl.BlockSpec((None, D_MODEL, TF_G), lambda g, f, te, nt: (te[g], 0, f_eff(g, f, nt))),
                pl.BlockSpec((None, TF_G, D_MODEL), lambda g, f, te, nt: (te[g], f_eff(g, f, nt), 0)),
            ],
            out_specs=pl.BlockSpec((TMG, D_MODEL), lambda g, f, te, nt: (g, 0)),
            scratch_shapes=[pltpu.VMEM((TMG, D_MODEL), BF16), pltpu.VMEM((TMG, D_MODEL), F32)],
        ),
        out_shape=jax.ShapeDtypeStruct((n_rows, D_MODEL), F32),
        compiler_params=_params(("arbitrary", "arbitrary")),
        name="moe_experts",
    )(tile_expert, n_tiles, xs, wg, wu, wd)


def _combine_kernel(slots_ref, x_ref, info_ref, fg_ref, y_ref, o_ref, rows_ref, sem):
    base = pl.program_id(0) * (2 * BLK)

    def body(r, carry):
        for j in range(2):
            _row_copy(y_ref, slots_ref[base + j * BLK + r], rows_ref.at[j], r, sem).start()
        return carry

    lax.fori_loop(0, BLK, body, 0, unroll=ROW_UNROLL)
    for j in range(2):
        pltpu.make_async_copy(y_ref.at[pl.ds(0, BLK), :], rows_ref.at[j], sem).wait()
    w1 = info_ref[:, 4:5]
    w2 = info_ref[:, 5:6]
    o_ref[...] = _rms(x_ref[...] + w1 * rows_ref[0] + w2 * rows_ref[1], fg_ref[...])


def _combine(slots, x, info, final_g, y):
    t = x.shape[0]
    return pl.pallas_call(
        _combine_kernel,
        grid_spec=pltpu.PrefetchScalarGridSpec(
            num_scalar_prefetch=1,
            grid=(t // BLK,),
            in_specs=[pl.BlockSpec((BLK, D_MODEL), lambda i, s: (i, 0)),
                      pl.BlockSpec((BLK, INFO_W), lambda i, s: (i, 0)),
                      pl.BlockSpec((1, D_MODEL), lambda i, s: (0, 0)),
                      pl.BlockSpec(memory_space=pl.ANY)],
            out_specs=pl.BlockSpec((BLK, D_MODEL), lambda i, s: (i, 0)),
            scratch_shapes=[pltpu.VMEM((2, BLK, D_MODEL), F32), pltpu.SemaphoreType.DMA(())],
        ),
        out_shape=jax.ShapeDtypeStruct((t, D_MODEL), F32),
        compiler_params=_params(("arbitrary",)),
        name="moe_combine",
    )(slots, x, info, final_g, y)


def _block_diag(w):
    g, c, _ = w.shape
    eye = jnp.eye(g, dtype=w.dtype)
    return (w[:, :, None, :] * eye[:, None, :, None]).reshape(g * c, g * c)


def _rotate_half_cols(w):
    half = w.shape[-1] // 2
    return jnp.concatenate([-w[..., half:], w[..., :half]], axis=-1)


def _rope_tables():
    pos = np.arange(SEQ, dtype=np.float32)
    inv = np.float32(ROPE_THETA) ** (-np.arange(0, QK_ROPE, 2, dtype=np.float32) / np.float32(QK_ROPE))
    ang = pos[:, None] * inv[None, :]
    cos, sin = jnp.asarray(np.cos(ang)), jnp.asarray(np.sin(ang))
    one = jnp.ones((SEQ, QK_NOPE), F32)
    z_nope = jnp.zeros((SEQ, QK_NOPE), F32)
    z_pad = jnp.zeros((SEQ, HEAD_PAD - QK_NOPE - QK_ROPE), F32)
    z_rest = jnp.zeros((SEQ, LANES - QK_ROPE), F32)
    cq_tab = jnp.concatenate([one, cos, cos, z_pad], axis=1)
    sq_tab = jnp.concatenate([z_nope, sin, sin, z_pad], axis=1)
    ck_tab = jnp.concatenate([cos, cos, z_rest], axis=1)
    sk_tab = jnp.concatenate([sin, sin, z_rest], axis=1)
    return cq_tab, sq_tab, ck_tab, sk_tab


def _rope_placement():
    r = jnp.arange(LANES)[:, None]
    c = jnp.arange(MLA_HEADS * HEAD_PAD)[None, :]
    return ((r < QK_ROPE) & (c % HEAD_PAD == QK_NOPE + r)).astype(BF16)


def _split_in_proj(w):
    bounds = (0, 256, 512, 640, 672, 1184, 1696, w.shape[1])
    zp, cq, ckv, kr, zc, zs, zg = [w[:, a:b] for a, b in zip(bounds[:-1], bounds[1:])]
    pad = jnp.zeros((D_MODEL, LANES - 2 * QK_ROPE), w.dtype)
    small = jnp.concatenate([zc, zs, zp, cq, ckv, kr, _rotate_half_cols(kr), pad], axis=1)
    return small.astype(BF16), zg.astype(BF16)


def _mla_weights(w_uq, w_ukv):
    wq = w_uq.reshape(Q_LORA, MLA_HEADS, QK_NOPE + QK_ROPE)
    nope, rope = wq[..., :QK_NOPE], wq[..., QK_NOPE:]
    zq = jnp.zeros((Q_LORA, MLA_HEADS, HEAD_PAD - QK_NOPE - QK_ROPE), w_uq.dtype)
    wqa = jnp.concatenate([nope, rope, zq], axis=-1).reshape(Q_LORA, MLA_HEADS * HEAD_PAD)
    wqb = jnp.concatenate([jnp.zeros_like(nope), _rotate_half_cols(rope), zq], axis=-1)
    wqb = wqb.reshape(Q_LORA, MLA_HEADS * HEAD_PAD)
    wkv = w_ukv.reshape(KV_LORA, MLA_HEADS, QK_NOPE + V_HEAD)
    k_nope, v = wkv[..., :QK_NOPE], wkv[..., QK_NOPE:]
    zk = jnp.zeros((KV_LORA, MLA_HEADS, HEAD_PAD - QK_NOPE), w_ukv.dtype)
    wk = jnp.concatenate([k_nope, zk], axis=-1).reshape(KV_LORA, MLA_HEADS * HEAD_PAD)
    zv = jnp.zeros((KV_LORA, MLA_HEADS, V_ROWS - V_HEAD), w_ukv.dtype)
    wv = jnp.concatenate([v, zv], axis=-1).reshape(KV_LORA, MLA_HEADS * V_ROWS)
    return wqa.astype(BF16), wqb.astype(BF16), wk.astype(BF16), wv.astype(BF16)


def kernel(x, norm_mix_g, w_in, b_gate, w_pool, pool_scale, w_pool_out, q_norm_g, w_uq, kv_norm_g, w_ukv, w_mla_out, conv_w, conv_b, conv_ln_g, conv_ln_b, w_conv_out, sgu_ln_g, sgu_ln_b, sgu_w, sgu_b, w_sgu_out, w_o, norm_ffn_g, w_ffn_gate, w_ffn_up, w_ffn_down, w_router, b_router, w_moe_gate, w_moe_up, w_moe_down, final_norm_g):
    batch, seq, d = x.shape
    assert (seq, d) == (SEQ, D_MODEL)
    depth = w_in.shape[0]
    xt = x.reshape(batch * seq, d)
    tabs = _rope_tables()
    place = _rope_placement()
    row = lambda a: a.reshape(1, -1)

    for l in range(depth):
        w_small, w_gate = _split_in_proj(w_in[l])
        zs = _in_proj(xt, row(norm_mix_g[l]), w_small)

        w_bd = _block_diag(w_pool[l])
        m_pool = _pool(zs, w_bd.astype(BF16), row(pool_scale[l]))
        m_conv = _conv(zs, conv_w[l], row(conv_b[l]), row(conv_ln_g[l]), row(conv_ln_b[l]))
        bias_full = jnp.repeat(sgu_b[l].T, SGU_WIDTH // SGU_GROUPS, axis=1)
        m_sgu = _sgu(zs, row(sgu_ln_g[l]), row(sgu_ln_b[l]),
                     sgu_w[l].reshape(SGU_GROUPS * SGU_LEN, SGU_LEN), bias_full)
        wqa, wqb, wk, wv = _mla_weights(w_uq[l], w_ukv[l])
        q, k, v = _mla_prep(zs, row(q_norm_g[l]), row(kv_norm_g[l]), wqa, wqb, wk, wv, place, tabs)
        m_mla = _attention(q, k, v, batch)

        xt = _merge(xt, row(norm_mix_g[l]), m_pool, m_mla, m_conv, m_sgu, w_gate, row(b_gate[l]),
                    w_pool_out[l].astype(BF16), w_mla_out[l].astype(BF16),
                    w_conv_out[l].astype(BF16), w_sgu_out[l].astype(BF16), w_o[l].astype(BF16))

        j = l // 2
        if l % 2 == 0:
            xt = _ffn(xt, row(norm_ffn_g[l]), w_ffn_gate[j].astype(BF16),
                      w_ffn_up[j].astype(BF16), w_ffn_down[j].astype(BF16))
        else:
            wr = jnp.pad(w_router[j], ((0, 0), (0, LANES - N_EXPERTS)))
            br = jnp.pad(b_router[j], (0, LANES - N_EXPERTS)).reshape(1, LANES)
            t = xt.shape[0]
            h, info, info_t, cnt = _router(xt, row(norm_ffn_g[l]), wr, br)
            slots, clear, tile_expert, n_tiles = _route_plan(info_t, cnt, t)
            xs = _dispatch(slots, clear, h, _num_row_tiles(t) * TMG)
            y = _experts(tile_expert, n_tiles, xs, w_moe_gate[j].astype(BF16),
                         w_moe_up[j].astype(BF16), w_moe_down[j].astype(BF16))
            assert l == depth - 1
            xt = _combine(slots, xt, info, row(final_norm_g), y)
    return xt.reshape(batch, seq, d)
```

```python
import jax
import jax.numpy as jnp
import numpy as np
from jax import lax
from jax.experimental import pallas as pl
from jax.experimental.pallas import tpu as pltpu

F32 = jnp.float32
BF16 = jnp.bfloat16

D_MODEL = 1024
SEQ = 8192
CHUNK = 64
POOL_WIDTH = 256
POOL_WINDOWS = (2, 4, 8, 16)
MLA_HEADS = 8
Q_LORA = 256
KV_LORA = 128
QK_NOPE = 64
QK_ROPE = 32
V_HEAD = 64
V_ROWS = 80
ROPE_THETA = 10000.0
CONV_WIDTH = 256
CONV_K = 31
SGU_WIDTH = 256
SGU_GROUPS = 4
SGU_LEN = 128
N_BRANCH = 4
D_FF = 2816
N_EXPERTS = 8
D_FF_EXPERT = 3584
EPS = 1e-6
NEG = -1e30
LOG2E = 1.4426950408889634

LANES = 128
SUBLANES = 8
MXU_TILE = 256
HEAD_PAD = 128
ZC_COL, ZS_COL, ZP_COL, CQ_COL, CKV_COL, KR_COL = 0, 512, 1024, 1280, 1536, 1664
Z_SMALL = 1792
POOL_HALO = 32
CONV_HALO = 32

TM = 512
TQ = 512
TK = 512
TR = 512
BLK = 1024
ROW_UNROLL = 8
TMG = 512
TF_G = 1792
INFO_W = 8
VMEM_LIMIT = 56 * 1024 * 1024


def _params(sem):
    return pltpu.CompilerParams(dimension_semantics=sem, vmem_limit_bytes=VMEM_LIMIT)


def _rms(x, g):
    return x * lax.rsqrt(jnp.mean(x * x, axis=-1, keepdims=True) + EPS) * g


def _layernorm(x, g, b):
    mu = jnp.mean(x, axis=-1, keepdims=True)
    xc = x - mu
    var = jnp.mean(xc * xc, axis=-1, keepdims=True)
    return xc * lax.rsqrt(var + EPS) * g + b


def _sigmoid(x):
    return 0.5 * jnp.tanh(0.5 * x) + 0.5


def _dot(a, b):
    return jnp.dot(a, b, preferred_element_type=F32)


def _mxu_chunks(total, max_chunk):
    assert total % MXU_TILE == 0 and max_chunk % MXU_TILE == 0
    return [slice(s, min(s + max_chunk, total)) for s in range(0, total, max_chunk)]


def _in_proj_kernel(x_ref, g_ref, ws_ref, zs_ref):
    h = _rms(x_ref[...], g_ref[...]).astype(BF16)
    for cs in _mxu_chunks(Z_SMALL, 1024):
        zs_ref[:, cs] = _dot(h, ws_ref[:, cs])


def _in_proj(x, g, w_small):
    t = x.shape[0]
    return pl.pallas_call(
        _in_proj_kernel,
        grid=(t // TM,),
        in_specs=[
            pl.BlockSpec((TM, D_MODEL), lambda i: (i, 0)),
            pl.BlockSpec((1, D_MODEL), lambda i: (0, 0)),
            pl.BlockSpec((D_MODEL, Z_SMALL), lambda i: (0, 0)),
        ],
        out_specs=pl.BlockSpec((TM, Z_SMALL), lambda i: (i, 0)),
        out_shape=jax.ShapeDtypeStruct((t, Z_SMALL), F32),
        compiler_params=_params(("parallel",)),
        name="in_proj",
    )(x, g, w_small)


def _pool_kernel(z_ref, halo_ref, w_ref, scale_ref, o_ref, s1_ref, s2_ref, s4_ref, s8_ref):
    i = pl.program_id(0)
    pos0 = (i * TM) % SEQ
    z = z_ref[...]
    rows = TM + POOL_HALO
    s1_ref[0:POOL_HALO, :] = jnp.where(pos0 == 0, 0.0, halo_ref[...])
    s1_ref[POOL_HALO:, :] = z

    def double(src_ref, dst_ref, shift):
        dst_ref[0:SUBLANES, :] = jnp.zeros((SUBLANES, POOL_WIDTH), F32)
        dst_ref[SUBLANES:, :] = src_ref[SUBLANES:, :] + src_ref[SUBLANES - shift:rows - shift, :]

    double(s1_ref, s2_ref, 1)
    double(s2_ref, s4_ref, 2)
    double(s4_ref, s8_ref, 4)
    s8 = s8_ref[POOL_HALO:, :]
    sums = (s2_ref[POOL_HALO:, :], s4_ref[POOL_HALO:, :], s8, s8 + s8_ref[POOL_HALO - 8:rows - 8, :])
    lane = lax.broadcasted_iota(jnp.int32, (TM, POOL_WIDTH), 1)
    pos = lax.broadcasted_iota(jnp.int32, (TM, POOL_WIDTH), 0) + pos0
    group = lane // (POOL_WIDTH // len(POOL_WINDOWS))
    assert POOL_WINDOWS == (2, 4, 8, 16)
    total = jnp.zeros_like(z)
    win = jnp.zeros_like(lane)
    for gi, w in enumerate(POOL_WINDOWS):
        total = jnp.where(group == gi, sums[gi], total)
        win = jnp.where(group == gi, w, win)
    cnt = jnp.minimum(pos + 1, win).astype(F32)
    pooled = total / cnt - z
    y = _dot(pooled.astype(BF16), w_ref[...]) * scale_ref[...]
    o_ref[...] = y.astype(BF16)


def _pool(zs, w_bd, scale):
    t = zs.shape[0]
    cb = ZP_COL // POOL_WIDTH
    r = TM // POOL_HALO
    return pl.pallas_call(
        _pool_kernel,
        grid=(t // TM,),
        in_specs=[
            pl.BlockSpec((TM, POOL_WIDTH), lambda i: (i, cb)),
            pl.BlockSpec((POOL_HALO, POOL_WIDTH), lambda i: (jnp.maximum(i * r - 1, 0), cb)),
            pl.BlockSpec((POOL_WIDTH, POOL_WIDTH), lambda i: (0, 0)),
            pl.BlockSpec((1, POOL_WIDTH), lambda i: (0, 0)),
        ],
        out_specs=pl.BlockSpec((TM, POOL_WIDTH), lambda i: (i, 0)),
        out_shape=jax.ShapeDtypeStruct((t, POOL_WIDTH), BF16),
        scratch_shapes=[pltpu.VMEM((TM + POOL_HALO, POOL_WIDTH), F32)] * 4,
        compiler_params=_params(("parallel",)),
        name="pool_mixer",
    )(zs, zs, w_bd, scale)


def _glu(z2):
    return z2[:, :CONV_WIDTH] * _sigmoid(z2[:, CONV_WIDTH:])


def _conv_kernel(z_ref, halo_ref, w_ref, b_ref, lg_ref, lb_ref, o_ref, buf_ref, sh_ref):
    i = pl.program_id(0)
    pos0 = (i * TM) % SEQ
    buf_ref[0:CONV_HALO, :] = jnp.where(pos0 == 0, 0.0, _glu(halo_ref[...]))
    buf_ref[CONV_HALO:, :] = _glu(z_ref[...])
    span = TM + CONV_HALO - SUBLANES
    for ph in range(1, SUBLANES):
        sh_ref[ph - 1, 0:span, :] = buf_ref[ph:ph + span, :]
    off = CONV_HALO - (CONV_K - 1)
    y = jnp.zeros((TM, CONV_WIDTH), F32) + b_ref[...]
    for k in range(CONV_K):
        ph = (off + k) % SUBLANES
        r0 = off + k - ph
        tap = buf_ref[r0:r0 + TM, :] if ph == 0 else sh_ref[ph - 1, r0:r0 + TM, :]
        y = y + w_ref[k:k + 1, :] * tap
    yn = _layernorm(y, lg_ref[...], lb_ref[...])
    o_ref[...] = (yn * _sigmoid(yn)).astype(BF16)


def _conv(zs, conv_w, conv_b, ln_g, ln_b):
    t = zs.shape[0]
    cb = ZC_COL // (2 * CONV_WIDTH)
    r = TM // CONV_HALO
    vec = pl.BlockSpec((1, CONV_WIDTH), lambda i: (0, 0))
    return pl.pallas_call(
        _conv_kernel,
        grid=(t // TM,),
        in_specs=[
            pl.BlockSpec((TM, 2 * CONV_WIDTH), lambda i: (i, cb)),
            pl.BlockSpec((CONV_HALO, 2 * CONV_WIDTH), lambda i: (jnp.maximum(i * r - 1, 0), cb)),
            pl.BlockSpec((CONV_K, CONV_WIDTH), lambda i: (0, 0)),
            vec, vec, vec,
        ],
        out_specs=pl.BlockSpec((TM, CONV_WIDTH), lambda i: (i, 0)),
        out_shape=jax.ShapeDtypeStruct((t, CONV_WIDTH), BF16),
        scratch_shapes=[pltpu.VMEM((TM + CONV_HALO, CONV_WIDTH), F32),
                        pltpu.VMEM((SUBLANES - 1, TM + CONV_HALO, CONV_WIDTH), F32)],
        compiler_params=_params(("parallel",)),
        name="conv_mixer",
    )(zs, zs, conv_w, conv_b, ln_g, ln_b)


def _sgu_kernel(z_ref, lg_ref, lb_ref, ws_ref, bias_ref, o_ref):
    z = z_ref[...]
    z = 0.5 * z * (1.0 + lax.erf(z * (2.0 ** -0.5)))
    u = z[:, :SGU_WIDTH]
    v = _layernorm(z[:, SGU_WIDTH:], lg_ref[...], lb_ref[...]).astype(BF16)
    rows = SGU_GROUPS * SGU_LEN
    row = lax.broadcasted_iota(jnp.int32, (rows, SGU_LEN), 0) % SGU_LEN
    col = lax.broadcasted_iota(jnp.int32, (rows, SGU_LEN), 1)
    ws = jnp.where(col <= row, ws_ref[...], 0.0).astype(BF16)
    lane_group = lax.broadcasted_iota(jnp.int32, (SGU_LEN, SGU_WIDTH), 1) // (SGU_WIDTH // SGU_GROUPS)
    for blk in range(TM // SGU_LEN):
        r0 = blk * SGU_LEN
        full = _dot(ws, v[r0:r0 + SGU_LEN, :])
        mixed = full[0:SGU_LEN, :]
        for g in range(1, SGU_GROUPS):
            mixed = jnp.where(lane_group == g, full[g * SGU_LEN:(g + 1) * SGU_LEN, :], mixed)
        mixed = mixed + bias_ref[...]
        o_ref[r0:r0 + SGU_LEN, :] = (u[r0:r0 + SGU_LEN, :] * mixed).astype(BF16)


def _sgu(zs, ln_g, ln_b, ws_stack, bias_full):
    t = zs.shape[0]
    cb = ZS_COL // (2 * SGU_WIDTH)
    vec = pl.BlockSpec((1, SGU_WIDTH), lambda i: (0, 0))
    return pl.pallas_call(
        _sgu_kernel,
        grid=(t // TM,),
        in_specs=[
            pl.BlockSpec((TM, 2 * SGU_WIDTH), lambda i: (i, cb)),
            vec, vec,
            pl.BlockSpec((SGU_GROUPS * SGU_LEN, SGU_LEN), lambda i: (0, 0)),
            pl.BlockSpec((SGU_LEN, SGU_WIDTH), lambda i: (0, 0)),
        ],
        out_specs=pl.BlockSpec((TM, SGU_WIDTH), lambda i: (i, 0)),
        out_shape=jax.ShapeDtypeStruct((t, SGU_WIDTH), BF16),
        compiler_params=_params(("parallel",)),
        name="sgu_mixer",
    )(zs, ln_g, ln_b, ws_stack, bias_full)


def _tile_heads(tab):
    return jnp.concatenate([tab] * MLA_HEADS, axis=1)


def _mla_prep_kernel(cq_ref, ckv_ref, kr_ref, qg_ref, kvg_ref, wqa_ref, wqb_ref, wk_ref, wv_ref,
                     place_ref, cq_tab_ref, sq_tab_ref, ck_tab_ref, sk_tab_ref,
                     qt_ref, k_ref, vt_ref):
    cqn = _rms(cq_ref[...], qg_ref[...]).astype(BF16)
    qa = _dot(cqn, wqa_ref[...])
    qb = _dot(cqn, wqb_ref[...])
    scale = (QK_NOPE + QK_ROPE) ** -0.5 * LOG2E
    q = (qa * _tile_heads(cq_tab_ref[...]) + qb * _tile_heads(sq_tab_ref[...])) * scale
    qt_ref[...] = q.T.astype(BF16)
    kvn = _rms(ckv_ref[...], kvg_ref[...]).astype(BF16)
    vt = _dot(kvn, wv_ref[...]).T
    ones_row = lax.broadcasted_iota(jnp.int32, vt.shape, 0) % V_ROWS == V_HEAD
    vt = jnp.where(ones_row, 1.0, vt)
    for c in range(TM // TK):
        vt_ref[c] = vt[:, c * TK:(c + 1) * TK].astype(BF16)
    kr = kr_ref[...]
    kro = kr * ck_tab_ref[...] + pltpu.roll(kr, LANES - QK_ROPE, 1) * sk_tab_ref[...]
    k = _dot(kvn, wk_ref[...]) + _dot(kro.astype(BF16), place_ref[...])
    k_ref[...] = k.astype(BF16)


def _mla_prep(zs, q_g, kv_g, wqa, wqb, wk, wv, place, tabs):
    t = zs.shape[0]
    hw = MLA_HEADS * HEAD_PAD
    hv = MLA_HEADS * V_ROWS
    ns = SEQ // TM
    tab = pl.BlockSpec((TM, LANES), lambda i: (i % ns, 0))
    full = lambda a: pl.BlockSpec(a.shape, lambda i: (0, 0))
    return pl.pallas_call(
        _mla_prep_kernel,
        grid=(t // TM,),
        in_specs=[
            pl.BlockSpec((TM, Q_LORA), lambda i: (i, CQ_COL // Q_LORA)),
            pl.BlockSpec((TM, KV_LORA), lambda i: (i, CKV_COL // KV_LORA)),
            pl.BlockSpec((TM, LANES), lambda i: (i, KR_COL // LANES)),
            full(q_g), full(kv_g), full(wqa), full(wqb), full(wk), full(wv), full(place),
            tab, tab, tab, tab,
        ],
        out_specs=[
            pl.BlockSpec((hw, TM), lambda i: (0, i)),
            pl.BlockSpec((TM, hw), lambda i: (i, 0)),
            pl.BlockSpec((TM // TK, hv, TK), lambda i: (i, 0, 0)),
        ],
        out_shape=[
            jax.ShapeDtypeStruct((hw, t), BF16),
            jax.ShapeDtypeStruct((t, hw), BF16),
            jax.ShapeDtypeStruct((t // TK, hv, TK), BF16),
        ],
        compiler_params=_params(("parallel",)),
        name="mla_prep",
    )(zs, zs, zs, q_g, kv_g, wqa, wqb, wk, wv, place, *tabs)


def _attn_kernel(qt_ref, k_ref, vt_ref, o_ref, m_ref, acc_ref, st0_ref, st1_ref, mx_ref):
    qi = pl.program_id(1)
    m_ref[...] = jnp.full(m_ref.shape, NEG, F32)
    acc_ref[...] = jnp.zeros(acc_ref.shape, F32)
    bufs = (st0_ref, st1_ref)

    def scores_head(h, j, slot, masked):
        ks = pl.multiple_of(j * TK, TK)
        hs = slice(h * HEAD_PAD, (h + 1) * HEAD_PAD)
        st = _dot(k_ref[pl.ds(ks, TK), hs], qt_ref[hs, :])
        if masked:
            key_chunk = lax.broadcasted_iota(jnp.int32, (TK, TQ), 0) // CHUNK
            qry_chunk = lax.broadcasted_iota(jnp.int32, (TK, TQ), 1) // CHUNK
            st = jnp.where(key_chunk <= qry_chunk, st, NEG)
        bufs[slot][h] = st
        mx_ref[slot, h] = jnp.max(st, axis=0, keepdims=True)

    def update_head(h, j, slot):
        m_prev = m_ref[h]
        m_new = jnp.maximum(m_prev, mx_ref[slot, h])
        alpha = jnp.exp2(m_prev - m_new)
        p = jnp.exp2(bufs[slot][h] - m_new).astype(BF16)
        m_ref[h] = m_new
        rows = slice(h * V_ROWS, (h + 1) * V_ROWS)
        acc_ref[rows, :] = alpha * acc_ref[rows, :] + _dot(vt_ref[j, rows, :], p)

    def scores(j, slot, masked):
        for h in range(MLA_HEADS):
            scores_head(h, j, slot, masked)

    def update(j, slot):
        for h in range(MLA_HEADS):
            update_head(h, j, slot)

    def scores_and_update(j_next, slot_next, masked, j, slot):
        for h in range(MLA_HEADS):
            scores_head(h, j_next, slot_next, masked)
            update_head(h, j, slot)

    @pl.when(qi > 0)
    def _():
        scores(0, 0, False)

    def pair(i, carry):
        scores_and_update(2 * i + 1, 1, False, 2 * i, 0)
        scores_and_update(2 * i + 2, 0, False, 2 * i + 1, 1)
        return carry

    pairs = jnp.maximum(qi - 1, 0) // 2
    lax.fori_loop(0, pairs, pair, 0)
    done = 2 * pairs

    @pl.when(qi == 0)
    def _():
        scores(0, 0, True)
        update(0, 0)

    @pl.when(qi % 2 == 1)
    def _():
        scores_and_update(qi, 1, True, done, 0)
        update(qi, 1)

    @pl.when((qi > 0) & (qi % 2 == 0))
    def _():
        scores_and_update(qi - 1, 1, False, done, 0)
        scores_and_update(qi, 0, True, qi - 1, 1)
        update(qi, 0)

    outs = [acc_ref[h * V_ROWS:h * V_ROWS + V_HEAD, :] / acc_ref[h * V_ROWS + V_HEAD:h * V_ROWS + V_HEAD + 1, :]
            for h in range(MLA_HEADS)]
    o_ref[...] = jnp.concatenate(outs, axis=0).T.astype(BF16)


def _attention(qt, k, vt, batch):
    t = k.shape[0]
    nq = SEQ // TQ
    nk = SEQ // TK
    hw = MLA_HEADS * HEAD_PAD
    hv = MLA_HEADS * V_HEAD
    hr = MLA_HEADS * V_ROWS
    once = pl.Buffered(1)
    return pl.pallas_call(
        _attn_kernel,
        grid=(batch, nq),
        in_specs=[
            pl.BlockSpec((hw, TQ), lambda b, i: (0, b * nq + i)),
            pl.BlockSpec((SEQ, hw), lambda b, i: (b, 0), pipeline_mode=once),
            pl.BlockSpec((nk, hr, TK), lambda b, i: (b, 0, 0), pipeline_mode=once),
        ],
        out_specs=pl.BlockSpec((TQ, hv), lambda b, i: (b * nq + i, 0)),
        out_shape=jax.ShapeDtypeStruct((t, hv), BF16),
        scratch_shapes=[
            pltpu.VMEM((MLA_HEADS, 1, TQ), F32),
            pltpu.VMEM((hr, TQ), F32),
            pltpu.VMEM((MLA_HEADS, TK, TQ), F32),
            pltpu.VMEM((MLA_HEADS, TK, TQ), F32),
            pltpu.VMEM((2, MLA_HEADS, 1, TQ), F32),
        ],
        compiler_params=_params(("parallel", "arbitrary")),
        name="mla_attention",
    )(qt, k, vt)


def _merge_kernel(x_ref, ng_ref, mp_ref, ma_ref, mc_ref, ms_ref, wgate_ref, bg_ref,
                  wp_ref, wa_ref, wc_ref, ws_ref, wo_ref, o_ref, mg_ref):
    x = x_ref[...]
    h = _rms(x, ng_ref[...]).astype(BF16)
    branches = ((mp_ref, wp_ref), (ma_ref, wa_ref), (mc_ref, wc_ref), (ms_ref, ws_ref))
    half = D_MODEL // 2
    for n0 in range(0, D_MODEL, half):
        merged = jnp.zeros((TM, half), F32)
        for b, (m_ref, w_ref) in enumerate(branches):
            c0 = b * D_MODEL + n0
            gate = _sigmoid(_dot(h, wgate_ref[:, c0:c0 + half]) + bg_ref[:, c0:c0 + half])
            merged = merged + gate * _dot(m_ref[...], w_ref[:, n0:n0 + half])
        mg_ref[:, n0:n0 + half] = merged.astype(BF16)
    o_ref[...] = x + _dot(mg_ref[...], wo_ref[...])


def _merge(x, norm_g, mp, ma, mc, ms, w_gate, b_gate, wp, wa, wc, ws, wo):
    t = x.shape[0]
    row = lambda w: pl.BlockSpec((TM, w), lambda i: (i, 0))
    full = lambda a: pl.BlockSpec(a.shape, lambda i: (0, 0))
    return pl.pallas_call(
        _merge_kernel,
        grid=(t // TM,),
        in_specs=[row(D_MODEL), full(norm_g), row(POOL_WIDTH), row(MLA_HEADS * V_HEAD), row(CONV_WIDTH),
                  row(SGU_WIDTH), full(w_gate), full(b_gate), full(wp), full(wa), full(wc), full(ws), full(wo)],
        out_specs=row(D_MODEL),
        out_shape=jax.ShapeDtypeStruct((t, D_MODEL), F32),
        scratch_shapes=[pltpu.VMEM((TM, D_MODEL), BF16)],
        compiler_params=_params(("parallel",)),
        name="merge_out",
    )(x, norm_g, mp, ma, mc, ms, w_gate, b_gate, wp, wa, wc, ws, wo)


def _ffn_kernel(x_ref, g_ref, wg_ref, wu_ref, wd_ref, o_ref):
    x = x_ref[...]
    h = _rms(x, g_ref[...]).astype(BF16)
    acc = x
    for cs in _mxu_chunks(D_FF, 1536):
        gate = _dot(h, wg_ref[:, cs])
        up = _dot(h, wu_ref[:, cs])
        act = (gate * _sigmoid(gate) * up).astype(BF16)
        acc = acc + _dot(act, wd_ref[cs, :])
    o_ref[...] = acc


def _ffn(x, g, wg, wu, wd):
    t = x.shape[0]
    full = lambda a: pl.BlockSpec(a.shape, lambda i: (0, 0))
    return pl.pallas_call(
        _ffn_kernel,
        grid=(t // TM,),
        in_specs=[pl.BlockSpec((TM, D_MODEL), lambda i: (i, 0)), full(g), full(wg), full(wu), full(wd)],
        out_specs=pl.BlockSpec((TM, D_MODEL), lambda i: (i, 0)),
        out_shape=jax.ShapeDtypeStruct((t, D_MODEL), F32),
        compiler_params=_params(("parallel",)),
        name="ffn_dense",
    )(x, g, wg, wu, wd)


def _router_kernel(x_ref, g_ref, wr_ref, br_ref, h_ref, info_ref, info_t_ref, cnt_ref, carry_ref):
    @pl.when(pl.program_id(0) == 0)
    def _():
        carry_ref[...] = jnp.zeros(carry_ref.shape, F32)

    h = _rms(x_ref[...], g_ref[...])
    h_ref[...] = h
    w = wr_ref[...]
    h_hi = h.astype(BF16)
    w_hi = w.astype(BF16)
    h_lo = (h - h_hi.astype(F32)).astype(BF16)
    w_lo = (w - w_hi.astype(F32)).astype(BF16)
    logits = _dot(h_hi, w_hi) + _dot(h_lo, w_hi) + _dot(h_hi, w_lo) + br_ref[...]
    lane = lax.broadcasted_iota(jnp.int32, logits.shape, 1).astype(F32)
    logits = jnp.where(lane < N_EXPERTS, logits, NEG)
    v1 = jnp.max(logits, axis=1, keepdims=True)
    i1 = jnp.min(jnp.where(logits == v1, lane, float(LANES)), axis=1, keepdims=True)
    rest = jnp.where(lane == i1, NEG, logits)
    v2 = jnp.max(rest, axis=1, keepdims=True)
    i2 = jnp.min(jnp.where(rest == v2, lane, float(LANES)), axis=1, keepdims=True)
    e = jnp.exp(v2 - v1)
    w1 = 1.0 / (1.0 + e)
    w2 = e / (1.0 + e)
    chosen = jnp.where((lane == i1) | (lane == i2), 1.0, 0.0)
    row = lax.broadcasted_iota(jnp.int32, (TR, TR), 0)
    col = lax.broadcasted_iota(jnp.int32, (TR, TR), 1)
    before = jnp.where(col < row, 1.0, 0.0)
    pos = _dot(before, chosen) + carry_ref[0:1, :]
    r1 = jnp.sum(jnp.where(lane == i1, pos, 0.0), axis=1, keepdims=True)
    r2 = jnp.sum(jnp.where(lane == i2, pos, 0.0), axis=1, keepdims=True)
    carry_ref[0:1, :] = carry_ref[0:1, :] + jnp.sum(chosen, axis=0, keepdims=True)
    cnt_ref[...] = carry_ref[...]
    fields = (i1, i2, r1, r2, w1, w2)
    info = jnp.zeros(logits.shape, F32)
    for n, val in enumerate(fields):
        info = jnp.where(lane == n, val, info)
    info_ref[...] = info[:, :INFO_W]
    info_t_ref[...] = info.T[:INFO_W, :]


def _router(x, g, wr, br):
    t = x.shape[0]
    full = lambda a: pl.BlockSpec(a.shape, lambda i: (0, 0))
    return pl.pallas_call(
        _router_kernel,
        grid=(t // TR,),
        in_specs=[pl.BlockSpec((TR, D_MODEL), lambda i: (i, 0)), full(g), full(wr), full(br)],
        out_specs=[
            pl.BlockSpec((TR, D_MODEL), lambda i: (i, 0)),
            pl.BlockSpec((TR, INFO_W), lambda i: (i, 0)),
            pl.BlockSpec((INFO_W, TR), lambda i: (0, i)),
            pl.BlockSpec((8, LANES), lambda i: (0, 0)),
        ],
        out_shape=[
            jax.ShapeDtypeStruct((t, D_MODEL), F32),
            jax.ShapeDtypeStruct((t, INFO_W), F32),
            jax.ShapeDtypeStruct((INFO_W, t), F32),
            jax.ShapeDtypeStruct((8, LANES), F32),
        ],
        scratch_shapes=[pltpu.VMEM((8, LANES), F32)],
        compiler_params=_params(("arbitrary",)),
        name="moe_router",
    )(x, g, wr, br)


def _num_row_tiles(t):
    return (2 * t) // TMG + N_EXPERTS


def _route_plan(info_t, cnt, t):
    counts = cnt[0, :N_EXPERTS].astype(jnp.int32)
    padded = ((counts + TMG - 1) // TMG) * TMG
    ends = jnp.cumsum(padded)
    off = ends - padded
    expert = info_t[0:2].astype(jnp.int32)
    slot = info_t[2:4].astype(jnp.int32)
    for e in range(N_EXPERTS):
        slot = slot + jnp.where(expert == e, off[e], 0)
    slots = slot.reshape(2, t // BLK, BLK).transpose(1, 0, 2).reshape(-1)
    starts = jnp.arange(_num_row_tiles(t), dtype=jnp.int32) * TMG
    first_row = jnp.minimum(starts, ends[-1] - 1)
    tile_expert = jnp.sum(ends[None, :] <= first_row[:, None], axis=1)
    n_tiles = (ends[-1] // TMG).reshape(1)
    last_tile = jnp.where(padded > 0, ends - TMG, -1)
    tail = starts[-N_EXPERTS:]
    clear = jnp.concatenate([last_tile, jnp.where(tail >= ends[-1], tail, -1)])
    return (slots.astype(jnp.int32), clear.astype(jnp.int32),
            tile_expert.astype(jnp.int32), n_tiles.astype(jnp.int32))


def _row_copy(src_ref, src_row, dst_ref, dst_row, sem):
    return pltpu.make_async_copy(src_ref.at[pl.ds(src_row, 1), :], dst_ref.at[pl.ds(dst_row, 1), :], sem)


def _dispatch_kernel(slots_ref, clear_ref, h_ref, xs_ref, zero_ref, sem):
    @pl.when(pl.program_id(0) == 0)
    def _():
        zero_ref[...] = jnp.zeros(zero_ref.shape, F32)
        for n in range(2 * N_EXPERTS):
            @pl.when(clear_ref[n] >= 0)
            def _():
                start = pl.multiple_of(clear_ref[n], TMG)
                clear = pltpu.make_async_copy(zero_ref, xs_ref.at[pl.ds(start, TMG), :], sem)
                clear.start()
                clear.wait()

    base = pl.program_id(0) * (2 * BLK)

    def body(r, carry):
        for j in range(2):
            _row_copy(h_ref, r, xs_ref, slots_ref[base + j * BLK + r], sem).start()
        return carry

    lax.fori_loop(0, BLK, body, 0, unroll=ROW_UNROLL)
    for j in range(2):
        pltpu.make_async_copy(h_ref, xs_ref.at[pl.ds(0, BLK), :], sem).wait()


def _dispatch(slots, clear, h, n_rows):
    t = h.shape[0]
    return pl.pallas_call(
        _dispatch_kernel,
        grid_spec=pltpu.PrefetchScalarGridSpec(
            num_scalar_prefetch=2,
            grid=(t // BLK,),
            in_specs=[pl.BlockSpec((BLK, D_MODEL), lambda i, s, c: (i, 0))],
            out_specs=pl.BlockSpec(memory_space=pl.ANY),
            scratch_shapes=[pltpu.VMEM((TMG, D_MODEL), F32), pltpu.SemaphoreType.DMA(())],
        ),
        out_shape=jax.ShapeDtypeStruct((n_rows, D_MODEL), F32),
        compiler_params=_params(("arbitrary",)),
        name="moe_dispatch",
    )(slots, clear, h)


def _experts_kernel(te_ref, nt_ref, xs_ref, wg_ref, wu_ref, wd_ref, y_ref, xb_ref, acc_ref):
    del te_ref
    f = pl.program_id(1)
    active = pl.program_id(0) < nt_ref[0]

    @pl.when(f == 0)
    def _():
        xb_ref[...] = xs_ref[...].astype(BF16)
        acc_ref[...] = jnp.zeros(acc_ref.shape, F32)

    @pl.when(active)
    def _():
        h = xb_ref[...]
        part = jnp.zeros((TMG, D_MODEL), F32)
        for cs in _mxu_chunks(TF_G, 1024):
            gate = _dot(h, wg_ref[:, cs])
            up = _dot(h, wu_ref[:, cs])
            part = part + _dot((gate * _sigmoid(gate) * up).astype(BF16), wd_ref[cs, :].astype(BF16))
        acc_ref[...] += part

    @pl.when(f == pl.num_programs(1) - 1)
    def _():
        y_ref[...] = acc_ref[...]


def _experts(tile_expert, n_tiles, xs, wg, wu, wd):
    n_rows = xs.shape[0]
    nf = D_FF_EXPERT // TF_G

    def f_eff(g, f, nt):
        return jnp.where(g < nt[0], f, nf - 1)

    return pl.pallas_call(
        _experts_kernel,
        grid_spec=pltpu.PrefetchScalarGridSpec(
            num_scalar_prefetch=2,
            grid=(n_rows // TMG, nf),
            in_specs=[
                pl.BlockSpec((TMG, D_MODEL), lambda g, f, te, nt: (g, 0)),
                pl.BlockSpec((None, D_MODEL, TF_G), lambda g, f, te, nt: (te[g], 0, f_eff(g, f, nt))),
                pl.BlockSpec((None, D_MODEL, TF_G), lambda g, f, te, nt: (te[g], 0, f_eff(g, f, nt))),
                pl.BlockSpec((None, TF_G, D_MODEL), lambda g, f, te, nt: (te[g], f_eff(g, f, nt), 0)),
            ],
            out_specs=pl.BlockSpec((TMG, D_MODEL), lambda g, f, te, nt: (g, 0)),
            scratch_shapes=[pltpu.VMEM((TMG, D_MODEL), BF16), pltpu.VMEM((TMG, D_MODEL), F32)],
        ),
        out_shape=jax.ShapeDtypeStruct((n_rows, D_MODEL), F32),
        compiler_params=_params(("arbitrary", "arbitrary")),
        name="moe_experts",
    )(tile_expert, n_tiles, xs, wg, wu, wd)


def _combine_kernel(slots_ref, x_ref, info_ref, fg_ref, y_ref, o_ref, rows_ref, sem):
    base = pl.program_id(0) * (2 * BLK)

    def body(r, carry):
        for j in range(2):
            _row_copy(y_ref, slots_ref[base + j * BLK + r], rows_ref.at[j], r, sem).start()
        return carry

    lax.fori_loop(0, BLK, body, 0, unroll=ROW_UNROLL)
    for j in range(2):
        pltpu.make_async_copy(y_ref.at[pl.ds(0, BLK), :], rows_ref.at[j], sem).wait()
    w1 = info_ref[:, 4:5]
    w2 = info_ref[:, 5:6]
    o_ref[...] = _rms(x_ref[...] + w1 * rows_ref[0] + w2 * rows_ref[1], fg_ref[...])


def _combine(slots, x, info, final_g, y):
    t = x.shape[0]
    return pl.pallas_call(
        _combine_kernel,
        grid_spec=pltpu.PrefetchScalarGridSpec(
            num_scalar_prefetch=1,
            grid=(t // BLK,),
            in_specs=[pl.BlockSpec((BLK, D_MODEL), lambda i, s: (i, 0)),
                      pl.BlockSpec((BLK, INFO_W), lambda i, s: (i, 0)),
                      pl.BlockSpec((1, D_MODEL), lambda i, s: (0, 0)),
                      pl.BlockSpec(memory_space=pl.ANY)],
            out_specs=pl.BlockSpec((BLK, D_MODEL), lambda i, s: (i, 0)),
            scratch_shapes=[pltpu.VMEM((2, BLK, D_MODEL), F32), pltpu.SemaphoreType.DMA(())],
        ),
        out_shape=jax.ShapeDtypeStruct((t, D_MODEL), F32),
        compiler_params=_params(("arbitrary",)),
        name="moe_combine",
    )(slots, x, info, final_g, y)


def _block_diag(w):
    g, c, _ = w.shape
    eye = jnp.eye(g, dtype=w.dtype)
    return (w[:, :, None, :] * eye[:, None, :, None]).reshape(g * c, g * c)


def _rotate_half_cols(w):
    half = w.shape[-1] // 2
    return jnp.concatenate([-w[..., half:], w[..., :half]], axis=-1)


def _rope_tables():
    pos = np.arange(SEQ, dtype=np.float32)
    inv = np.float32(ROPE_THETA) ** (-np.arange(0, QK_ROPE, 2, dtype=np.float32) / np.float32(QK_ROPE))
    ang = pos[:, None] * inv[None, :]
    cos, sin = jnp.asarray(np.cos(ang)), jnp.asarray(np.sin(ang))
    one = jnp.ones((SEQ, QK_NOPE), F32)
    z_nope = jnp.zeros((SEQ, QK_NOPE), F32)
    z_pad = jnp.zeros((SEQ, HEAD_PAD - QK_NOPE - QK_ROPE), F32)
    z_rest = jnp.zeros((SEQ, LANES - QK_ROPE), F32)
    cq_tab = jnp.concatenate([one, cos, cos, z_pad], axis=1)
    sq_tab = jnp.concatenate([z_nope, sin, sin, z_pad], axis=1)
    ck_tab = jnp.concatenate([cos, cos, z_rest], axis=1)
    sk_tab = jnp.concatenate([sin, sin, z_rest], axis=1)
    return cq_tab, sq_tab, ck_tab, sk_tab


def _rope_placement():
    r = jnp.arange(LANES)[:, None]
    c = jnp.arange(MLA_HEADS * HEAD_PAD)[None, :]
    return ((r < QK_ROPE) & (c % HEAD_PAD == QK_NOPE + r)).astype(BF16)


def _split_in_proj(w):
    bounds = (0, 256, 512, 640, 672, 1184, 1696, w.shape[1])
    zp, cq, ckv, kr, zc, zs, zg = [w[:, a:b] for a, b in zip(bounds[:-1], bounds[1:])]
    pad = jnp.zeros((D_MODEL, LANES - 2 * QK_ROPE), w.dtype)
    small = jnp.concatenate([zc, zs, zp, cq, ckv, kr, _rotate_half_cols(kr), pad], axis=1)
    return small.astype(BF16), zg.astype(BF16)


def _mla_weights(w_uq, w_ukv):
    wq = w_uq.reshape(Q_LORA, MLA_HEADS, QK_NOPE + QK_ROPE)
    nope, rope = wq[..., :QK_NOPE], wq[..., QK_NOPE:]
    zq = jnp.zeros((Q_LORA, MLA_HEADS, HEAD_PAD - QK_NOPE - QK_ROPE), w_uq.dtype)
    wqa = jnp.concatenate([nope, rope, zq], axis=-1).reshape(Q_LORA, MLA_HEADS * HEAD_PAD)
    wqb = jnp.concatenate([jnp.zeros_like(nope), _rotate_half_cols(rope), zq], axis=-1)
    wqb = wqb.reshape(Q_LORA, MLA_HEADS * HEAD_PAD)
    wkv = w_ukv.reshape(KV_LORA, MLA_HEADS, QK_NOPE + V_HEAD)
    k_nope, v = wkv[..., :QK_NOPE], wkv[..., QK_NOPE:]
    zk = jnp.zeros((KV_LORA, MLA_HEADS, HEAD_PAD - QK_NOPE), w_ukv.dtype)
    wk = jnp.concatenate([k_nope, zk], axis=-1).reshape(KV_LORA, MLA_HEADS * HEAD_PAD)
    zv = jnp.zeros((KV_LORA, MLA_HEADS, V_ROWS - V_HEAD), w_ukv.dtype)
    wv = jnp.concatenate([v, zv], axis=-1).reshape(KV_LORA, MLA_HEADS * V_ROWS)
    return wqa.astype(BF16), wqb.astype(BF16), wk.astype(BF16), wv.astype(BF16)


def kernel(x, norm_mix_g, w_in, b_gate, w_pool, pool_scale, w_pool_out, q_norm_g, w_uq, kv_norm_g, w_ukv, w_mla_out, conv_w, conv_b, conv_ln_g, conv_ln_b, w_conv_out, sgu_ln_g, sgu_ln_b, sgu_w, sgu_b, w_sgu_out, w_o, norm_ffn_g, w_ffn_gate, w_ffn_up, w_ffn_down, w_router, b_router, w_moe_gate, w_moe_up, w_moe_down, final_norm_g):
    batch, seq, d = x.shape
    assert (seq, d) == (SEQ, D_MODEL)
    depth = w_in.shape[0]
    xt = x.reshape(batch * seq, d)
    tabs = _rope_tables()
    place = _rope_placement()
    row = lambda a: a.reshape(1, -1)

    for l in range(depth):
        w_small, w_gate = _split_in_proj(w_in[l])
        zs = _in_proj(xt, row(norm_mix_g[l]), w_small)

        w_bd = _block_diag(w_pool[l])
        m_pool = _pool(zs, w_bd.astype(BF16), row(pool_scale[l]))
        m_conv = _conv(zs, conv_w[l], row(conv_b[l]), row(conv_ln_g[l]), row(conv_ln_b[l]))
        bias_full = jnp.repeat(sgu_b[l].T, SGU_WIDTH // SGU_GROUPS, axis=1)
        m_sgu = _sgu(zs, row(sgu_ln_g[l]), row(sgu_ln_b[l]),
                     sgu_w[l].reshape(SGU_GROUPS * SGU_LEN, SGU_LEN), bias_full)
        wqa, wqb, wk, wv = _mla_weights(w_uq[l], w_ukv[l])
        q, k, v = _mla_prep(zs, row(q_norm_g[l]), row(kv_norm_g[l]), wqa, wqb, wk, wv, place, tabs)
        m_mla = _attention(q, k, v, batch)

        xt = _merge(xt, row(norm_mix_g[l]), m_pool, m_mla, m_conv, m_sgu, w_gate, row(b_gate[l]),
                    w_pool_out[l].astype(BF16), w_mla_out[l].astype(BF16),
                    w_conv_out[l].astype(BF16), w_sgu_out[l].astype(BF16), w_o[l].astype(BF16))

        j = l // 2
        if l % 2 == 0:
            xt = _ffn(xt, row(norm_ffn_g[l]), w_ffn_gate[j].astype(BF16),
                      w_ffn_up[j].astype(BF16), w_ffn_down[j].astype(BF16))
        else:
            wr = jnp.pad(w_router[j], ((0, 0), (0, LANES - N_EXPERTS)))
            br = jnp.pad(b_router[j], (0, LANES - N_EXPERTS)).reshape(1, LANES)
            t = xt.shape[0]
            h, info, info_t, cnt = _router(xt, row(norm_ffn_g[l]), wr, br)
            slots, clear, tile_expert, n_tiles = _route_plan(info_t, cnt, t)
            xs = _dispatch(slots, clear, h, _num_row_tiles(t) * TMG)
            y = _experts(tile_expert, n_tiles, xs, w_moe_gate[j].astype(BF16),
                         w_moe_up[j].astype(BF16), w_moe_down[j])
            assert l == depth - 1
            xt = _combine(slots, xt, info, row(final_norm_g), y)
    return xt.reshape(batch, seq, d)
```

```python
import jax
import jax.numpy as jnp
import numpy as np
from jax import lax
from jax.experimental import pallas as pl
from jax.experimental.pallas import tpu as pltpu

F32 = jnp.float32
BF16 = jnp.bfloat16

D_MODEL = 1024
SEQ = 8192
CHUNK = 64
POOL_WIDTH = 256
POOL_WINDOWS = (2, 4, 8, 16)
MLA_HEADS = 8
Q_LORA = 256
KV_LORA = 128
QK_NOPE = 64
QK_ROPE = 32
V_HEAD = 64
V_ROWS = 80
ROPE_THETA = 10000.0
CONV_WIDTH = 256
CONV_K = 31
SGU_WIDTH = 256
SGU_GROUPS = 4
SGU_LEN = 128
D_FF = 2816
N_EXPERTS = 8
D_FF_EXPERT = 3584
EPS = 1e-6
NEG = -1e30
LOG2E = 1.4426950408889634

LANES = 128
SUBLANES = 8
MXU_TILE = 256
HEAD_PAD = 128
ZC_COL, ZS_COL, ZP_COL, CQ_COL, CKV_COL, KR_COL = 0, 512, 1024, 1280, 1536, 1664
Z_SMALL = 1792
POOL_HALO = 32
CONV_HALO = 32

TM = 512
TQ = 512
TK = 512
TR = 512
BLK = 1024
ROW_UNROLL = 16
TMG = 512
TF_G = 1792
INFO_W = 8
VMEM_LIMIT = 56 * 1024 * 1024


def _params(sem):
    return pltpu.CompilerParams(dimension_semantics=sem, vmem_limit_bytes=VMEM_LIMIT)


def _rms(x, g):
    return x * lax.rsqrt(jnp.mean(x * x, axis=-1, keepdims=True) + EPS) * g


def _layernorm(x, g, b):
    mu = jnp.mean(x, axis=-1, keepdims=True)
    xc = x - mu
    var = jnp.mean(xc * xc, axis=-1, keepdims=True)
    return xc * lax.rsqrt(var + EPS) * g + b


def _sigmoid(x):
    return 0.5 * jnp.tanh(0.5 * x) + 0.5


def _dot(a, b):
    return jnp.dot(a, b, preferred_element_type=F32)


def _mxu_chunks(total, max_chunk):
    assert total % MXU_TILE == 0 and max_chunk % MXU_TILE == 0
    return [slice(s, min(s + max_chunk, total)) for s in range(0, total, max_chunk)]


def _in_proj_kernel(x_ref, g_ref, ws_ref, zs_ref):
    h = _rms(x_ref[...], g_ref[...]).astype(BF16)
    for cs in _mxu_chunks(Z_SMALL, 1024):
        zs_ref[:, cs] = _dot(h, ws_ref[:, cs])


def _in_proj(x, g, w_small):
    t = x.shape[0]
    return pl.pallas_call(
        _in_proj_kernel,
        grid=(t // TM,),
        in_specs=[
            pl.BlockSpec((TM, D_MODEL), lambda i: (i, 0)),
            pl.BlockSpec((1, D_MODEL), lambda i: (0, 0)),
            pl.BlockSpec((D_MODEL, Z_SMALL), lambda i: (0, 0)),
        ],
        out_specs=pl.BlockSpec((TM, Z_SMALL), lambda i: (i, 0)),
        out_shape=jax.ShapeDtypeStruct((t, Z_SMALL), F32),
        compiler_params=_params(("parallel",)),
        name="in_proj",
    )(x, g, w_small)


def _pool_kernel(z_ref, halo_ref, w_ref, scale_ref, o_ref, s1_ref, s2_ref, s4_ref, s8_ref):
    i = pl.program_id(0)
    pos0 = (i * TM) % SEQ
    z = z_ref[...]
    rows = TM + POOL_HALO
    s1_ref[0:POOL_HALO, :] = jnp.where(pos0 == 0, 0.0, halo_ref[...])
    s1_ref[POOL_HALO:, :] = z

    def double(src_ref, dst_ref, shift):
        dst_ref[0:SUBLANES, :] = jnp.zeros((SUBLANES, POOL_WIDTH), F32)
        dst_ref[SUBLANES:, :] = src_ref[SUBLANES:, :] + src_ref[SUBLANES - shift:rows - shift, :]

    double(s1_ref, s2_ref, 1)
    double(s2_ref, s4_ref, 2)
    double(s4_ref, s8_ref, 4)
    s8 = s8_ref[POOL_HALO:, :]
    sums = (s2_ref[POOL_HALO:, :], s4_ref[POOL_HALO:, :], s8, s8 + s8_ref[POOL_HALO - 8:rows - 8, :])
    lane = lax.broadcasted_iota(jnp.int32, (TM, POOL_WIDTH), 1)
    pos = lax.broadcasted_iota(jnp.int32, (TM, POOL_WIDTH), 0) + pos0
    group = lane // (POOL_WIDTH // len(POOL_WINDOWS))
    assert POOL_WINDOWS == (2, 4, 8, 16)
    total = jnp.zeros_like(z)
    win = jnp.zeros_like(lane)
    for gi, w in enumerate(POOL_WINDOWS):
        total = jnp.where(group == gi, sums[gi], total)
        win = jnp.where(group == gi, w, win)
    cnt = jnp.minimum(pos + 1, win).astype(F32)
    pooled = total / cnt - z
    y = _dot(pooled.astype(BF16), w_ref[...]) * scale_ref[...]
    o_ref[...] = y.astype(BF16)


def _pool(zs, w_bd, scale):
    t = zs.shape[0]
    cb = ZP_COL // POOL_WIDTH
    r = TM // POOL_HALO
    return pl.pallas_call(
        _pool_kernel,
        grid=(t // TM,),
        in_specs=[
            pl.BlockSpec((TM, POOL_WIDTH), lambda i: (i, cb)),
            pl.BlockSpec((POOL_HALO, POOL_WIDTH), lambda i: (jnp.maximum(i * r - 1, 0), cb)),
            pl.BlockSpec((POOL_WIDTH, POOL_WIDTH), lambda i: (0, 0)),
            pl.BlockSpec((1, POOL_WIDTH), lambda i: (0, 0)),
        ],
        out_specs=pl.BlockSpec((TM, POOL_WIDTH), lambda i: (i, 0)),
        out_shape=jax.ShapeDtypeStruct((t, POOL_WIDTH), BF16),
        scratch_shapes=[pltpu.VMEM((TM + POOL_HALO, POOL_WIDTH), F32)] * 4,
        compiler_params=_params(("parallel",)),
        name="pool_mixer",
    )(zs, zs, w_bd, scale)


def _glu(z2):
    return z2[:, :CONV_WIDTH] * _sigmoid(z2[:, CONV_WIDTH:])


def _conv_kernel(z_ref, halo_ref, w_ref, b_ref, lg_ref, lb_ref, o_ref, buf_ref, sh_ref):
    i = pl.program_id(0)
    pos0 = (i * TM) % SEQ
    buf_ref[0:CONV_HALO, :] = jnp.where(pos0 == 0, 0.0, _glu(halo_ref[...]))
    buf_ref[CONV_HALO:, :] = _glu(z_ref[...])
    span = TM + CONV_HALO - SUBLANES
    for ph in range(1, SUBLANES):
        sh_ref[ph - 1, 0:span, :] = buf_ref[ph:ph + span, :]
    off = CONV_HALO - (CONV_K - 1)
    y = jnp.zeros((TM, CONV_WIDTH), F32) + b_ref[...]
    for k in range(CONV_K):
        ph = (off + k) % SUBLANES
        r0 = off + k - ph
        tap = buf_ref[r0:r0 + TM, :] if ph == 0 else sh_ref[ph - 1, r0:r0 + TM, :]
        y = y + w_ref[k:k + 1, :] * tap
    yn = _layernorm(y, lg_ref[...], lb_ref[...])
    o_ref[...] = (yn * _sigmoid(yn)).astype(BF16)


def _conv(zs, conv_w, conv_b, ln_g, ln_b):
    t = zs.shape[0]
    cb = ZC_COL // (2 * CONV_WIDTH)
    r = TM // CONV_HALO
    vec = pl.BlockSpec((1, CONV_WIDTH), lambda i: (0, 0))
    return pl.pallas_call(
        _conv_kernel,
        grid=(t // TM,),
        in_specs=[
            pl.BlockSpec((TM, 2 * CONV_WIDTH), lambda i: (i, cb)),
            pl.BlockSpec((CONV_HALO, 2 * CONV_WIDTH), lambda i: (jnp.maximum(i * r - 1, 0), cb)),
            pl.BlockSpec((CONV_K, CONV_WIDTH), lambda i: (0, 0)),
            vec, vec, vec,
        ],
        out_specs=pl.BlockSpec((TM, CONV_WIDTH), lambda i: (i, 0)),
        out_shape=jax.ShapeDtypeStruct((t, CONV_WIDTH), BF16),
        scratch_shapes=[pltpu.VMEM((TM + CONV_HALO, CONV_WIDTH), F32),
                        pltpu.VMEM((SUBLANES - 1, TM + CONV_HALO, CONV_WIDTH), F32)],
        compiler_params=_params(("parallel",)),
        name="conv_mixer",
    )(zs, zs, conv_w, conv_b, ln_g, ln_b)


def _sgu_kernel(z_ref, lg_ref, lb_ref, ws_ref, bias_ref, o_ref):
    z = z_ref[...]
    z = 0.5 * z * (1.0 + lax.erf(z * (2.0 ** -0.5)))
    u = z[:, :SGU_WIDTH]
    v = _layernorm(z[:, SGU_WIDTH:], lg_ref[...], lb_ref[...]).astype(BF16)
    rows = SGU_GROUPS * SGU_LEN
    row = lax.broadcasted_iota(jnp.int32, (rows, SGU_LEN), 0) % SGU_LEN
    col = lax.broadcasted_iota(jnp.int32, (rows, SGU_LEN), 1)
    ws = jnp.where(col <= row, ws_ref[...], 0.0).astype(BF16)
    lane_group = lax.broadcasted_iota(jnp.int32, (SGU_LEN, SGU_WIDTH), 1) // (SGU_WIDTH // SGU_GROUPS)
    for blk in range(TM // SGU_LEN):
        r0 = blk * SGU_LEN
        full = _dot(ws, v[r0:r0 + SGU_LEN, :])
        mixed = full[0:SGU_LEN, :]
        for g in range(1, SGU_GROUPS):
            mixed = jnp.where(lane_group == g, full[g * SGU_LEN:(g + 1) * SGU_LEN, :], mixed)
        mixed = mixed + bias_ref[...]
        o_ref[r0:r0 + SGU_LEN, :] = (u[r0:r0 + SGU_LEN, :] * mixed).astype(BF16)


def _sgu(zs, ln_g, ln_b, ws_stack, bias_full):
    t = zs.shape[0]
    cb = ZS_COL // (2 * SGU_WIDTH)
    vec = pl.BlockSpec((1, SGU_WIDTH), lambda i: (0, 0))
    return pl.pallas_call(
        _sgu_kernel,
        grid=(t // TM,),
        in_specs=[
            pl.BlockSpec((TM, 2 * SGU_WIDTH), lambda i: (i, cb)),
            vec, vec,
            pl.BlockSpec((SGU_GROUPS * SGU_LEN, SGU_LEN), lambda i: (0, 0)),
            pl.BlockSpec((SGU_LEN, SGU_WIDTH), lambda i: (0, 0)),
        ],
        out_specs=pl.BlockSpec((TM, SGU_WIDTH), lambda i: (i, 0)),
        out_shape=jax.ShapeDtypeStruct((t, SGU_WIDTH), BF16),
        compiler_params=_params(("parallel",)),
        name="sgu_mixer",
    )(zs, ln_g, ln_b, ws_stack, bias_full)


def _tile_heads(tab):
    return jnp.concatenate([tab] * MLA_HEADS, axis=1)


def _mla_prep_kernel(cq_ref, ckv_ref, kr_ref, qg_ref, kvg_ref, wqa_ref, wqb_ref, wk_ref, wv_ref,
                     place_ref, cq_tab_ref, sq_tab_ref, ck_tab_ref, sk_tab_ref,
                     qt_ref, k_ref, vt_ref):
    cqn = _rms(cq_ref[...], qg_ref[...]).astype(BF16)
    qa = _dot(cqn, wqa_ref[...])
    qb = _dot(cqn, wqb_ref[...])
    scale = (QK_NOPE + QK_ROPE) ** -0.5 * LOG2E
    q = (qa * _tile_heads(cq_tab_ref[...]) + qb * _tile_heads(sq_tab_ref[...])) * scale
    qt_ref[...] = q.T.astype(BF16)
    kvn = _rms(ckv_ref[...], kvg_ref[...]).astype(BF16)
    vt = _dot(kvn, wv_ref[...]).T
    ones_row = lax.broadcasted_iota(jnp.int32, vt.shape, 0) % V_ROWS == V_HEAD
    vt = jnp.where(ones_row, 1.0, vt)
    for c in range(TM // TK):
        vt_ref[c] = vt[:, c * TK:(c + 1) * TK].astype(BF16)
    kr = kr_ref[...]
    kro = kr * ck_tab_ref[...] + pltpu.roll(kr, LANES - QK_ROPE, 1) * sk_tab_ref[...]
    k = _dot(kvn, wk_ref[...]) + _dot(kro.astype(BF16), place_ref[...])
    k_ref[...] = k.astype(BF16)


def _mla_prep(zs, q_g, kv_g, wqa, wqb, wk, wv, place, tabs):
    t = zs.shape[0]
    hw = MLA_HEADS * HEAD_PAD
    hv = MLA_HEADS * V_ROWS
    ns = SEQ // TM
    tab = pl.BlockSpec((TM, LANES), lambda i: (i % ns, 0))
    full = lambda a: pl.BlockSpec(a.shape, lambda i: (0, 0))
    return pl.pallas_call(
        _mla_prep_kernel,
        grid=(t // TM,),
        in_specs=[
            pl.BlockSpec((TM, Q_LORA), lambda i: (i, CQ_COL // Q_LORA)),
            pl.BlockSpec((TM, KV_LORA), lambda i: (i, CKV_COL // KV_LORA)),
            pl.BlockSpec((TM, LANES), lambda i: (i, KR_COL // LANES)),
            full(q_g), full(kv_g), full(wqa), full(wqb), full(wk), full(wv), full(place),
            tab, tab, tab, tab,
        ],
        out_specs=[
            pl.BlockSpec((hw, TM), lambda i: (0, i)),
            pl.BlockSpec((TM, hw), lambda i: (i, 0)),
            pl.BlockSpec((TM // TK, hv, TK), lambda i: (i, 0, 0)),
        ],
        out_shape=[
            jax.ShapeDtypeStruct((hw, t), BF16),
            jax.ShapeDtypeStruct((t, hw), BF16),
            jax.ShapeDtypeStruct((t // TK, hv, TK), BF16),
        ],
        compiler_params=_params(("parallel",)),
        name="mla_prep",
    )(zs, zs, zs, q_g, kv_g, wqa, wqb, wk, wv, place, *tabs)


def _attn_kernel(qt_ref, k_ref, vt_ref, o_ref, m_ref, acc_ref, st0_ref, st1_ref, mx_ref):
    qi = pl.program_id(1)
    m_ref[...] = jnp.full(m_ref.shape, NEG, F32)
    acc_ref[...] = jnp.zeros(acc_ref.shape, F32)
    bufs = (st0_ref, st1_ref)

    def scores_head(h, j, slot, masked):
        ks = pl.multiple_of(j * TK, TK)
        hs = slice(h * HEAD_PAD, (h + 1) * HEAD_PAD)
        st = _dot(k_ref[pl.ds(ks, TK), hs], qt_ref[hs, :])
        if masked:
            key_chunk = lax.broadcasted_iota(jnp.int32, (TK, TQ), 0) // CHUNK
            qry_chunk = lax.broadcasted_iota(jnp.int32, (TK, TQ), 1) // CHUNK
            st = jnp.where(key_chunk <= qry_chunk, st, NEG)
        bufs[slot][h] = st
        mx_ref[slot, h] = jnp.max(st, axis=0, keepdims=True)

    def update_head(h, j, slot):
        m_prev = m_ref[h]
        m_new = jnp.maximum(m_prev, mx_ref[slot, h])
        alpha = jnp.exp2(m_prev - m_new)
        p = jnp.exp2(bufs[slot][h] - m_new).astype(BF16)
        m_ref[h] = m_new
        rows = slice(h * V_ROWS, (h + 1) * V_ROWS)
        acc_ref[rows, :] = alpha * acc_ref[rows, :] + _dot(vt_ref[j, rows, :], p)

    def scores(j, slot, masked):
        for h in range(MLA_HEADS):
            scores_head(h, j, slot, masked)

    def update(j, slot):
        for h in range(MLA_HEADS):
            update_head(h, j, slot)

    def scores_and_update(j_next, slot_next, masked, j, slot):
        for h in range(MLA_HEADS):
            scores_head(h, j_next, slot_next, masked)
            update_head(h, j, slot)

    @pl.when(qi > 0)
    def _():
        scores(0, 0, False)

    def pair(i, carry):
        scores_and_update(2 * i + 1, 1, False, 2 * i, 0)
        scores_and_update(2 * i + 2, 0, False, 2 * i + 1, 1)
        return carry

    pairs = jnp.maximum(qi - 1, 0) // 2
    lax.fori_loop(0, pairs, pair, 0)
    done = 2 * pairs

    @pl.when(qi == 0)
    def _():
        scores(0, 0, True)
        update(0, 0)

    @pl.when(qi % 2 == 1)
    def _():
        scores_and_update(qi, 1, True, done, 0)
        update(qi, 1)

    @pl.when((qi > 0) & (qi % 2 == 0))
    def _():
        scores_and_update(qi - 1, 1, False, done, 0)
        scores_and_update(qi, 0, True, qi - 1, 1)
        update(qi, 0)

    outs = [acc_ref[h * V_ROWS:h * V_ROWS + V_HEAD, :] / acc_ref[h * V_ROWS + V_HEAD:h * V_ROWS + V_HEAD + 1, :]
            for h in range(MLA_HEADS)]
    o_ref[...] = jnp.concatenate(outs, axis=0).T.astype(BF16)


def _attention(qt, k, vt, batch):
    t = k.shape[0]
    nq = SEQ // TQ
    nk = SEQ // TK
    hw = MLA_HEADS * HEAD_PAD
    hv = MLA_HEADS * V_HEAD
    hr = MLA_HEADS * V_ROWS
    once = pl.Buffered(1)
    return pl.pallas_call(
        _attn_kernel,
        grid=(batch, nq),
        in_specs=[
            pl.BlockSpec((hw, TQ), lambda b, i: (0, b * nq + i)),
            pl.BlockSpec((SEQ, hw), lambda b, i: (b, 0), pipeline_mode=once),
            pl.BlockSpec((nk, hr, TK), lambda b, i: (b, 0, 0), pipeline_mode=once),
        ],
        out_specs=pl.BlockSpec((TQ, hv), lambda b, i: (b * nq + i, 0)),
        out_shape=jax.ShapeDtypeStruct((t, hv), BF16),
        scratch_shapes=[
            pltpu.VMEM((MLA_HEADS, 1, TQ), F32),
            pltpu.VMEM((hr, TQ), F32),
            pltpu.VMEM((MLA_HEADS, TK, TQ), F32),
            pltpu.VMEM((MLA_HEADS, TK, TQ), F32),
            pltpu.VMEM((2, MLA_HEADS, 1, TQ), F32),
        ],
        compiler_params=_params(("parallel", "arbitrary")),
        name="mla_attention",
    )(qt, k, vt)


def _merge_kernel(x_ref, ng_ref, mp_ref, ma_ref, mc_ref, ms_ref, wgate_ref, bg_ref,
                  wp_ref, wa_ref, wc_ref, ws_ref, wo_ref, o_ref, mg_ref):
    x = x_ref[...]
    h = _rms(x, ng_ref[...]).astype(BF16)
    branches = ((mp_ref, wp_ref), (ma_ref, wa_ref), (mc_ref, wc_ref), (ms_ref, ws_ref))
    half = D_MODEL // 2
    for n0 in range(0, D_MODEL, half):
        merged = jnp.zeros((TM, half), F32)
        for b, (m_ref, w_ref) in enumerate(branches):
            c0 = b * D_MODEL + n0
            gate = _sigmoid(_dot(h, wgate_ref[:, c0:c0 + half]) + bg_ref[:, c0:c0 + half])
            merged = merged + gate * _dot(m_ref[...], w_ref[:, n0:n0 + half])
        mg_ref[:, n0:n0 + half] = merged.astype(BF16)
    o_ref[...] = x + _dot(mg_ref[...], wo_ref[...])


def _merge(x, norm_g, mp, ma, mc, ms, w_gate, b_gate, wp, wa, wc, ws, wo):
    t = x.shape[0]
    row = lambda w: pl.BlockSpec((TM, w), lambda i: (i, 0))
    full = lambda a: pl.BlockSpec(a.shape, lambda i: (0, 0))
    return pl.pallas_call(
        _merge_kernel,
        grid=(t // TM,),
        in_specs=[row(D_MODEL), full(norm_g), row(POOL_WIDTH), row(MLA_HEADS * V_HEAD), row(CONV_WIDTH),
                  row(SGU_WIDTH), full(w_gate), full(b_gate), full(wp), full(wa), full(wc), full(ws), full(wo)],
        out_specs=row(D_MODEL),
        out_shape=jax.ShapeDtypeStruct((t, D_MODEL), F32),
        scratch_shapes=[pltpu.VMEM((TM, D_MODEL), BF16)],
        compiler_params=_params(("parallel",)),
        name="merge_out",
    )(x, norm_g, mp, ma, mc, ms, w_gate, b_gate, wp, wa, wc, ws, wo)


def _ffn_kernel(x_ref, g_ref, wg_ref, wu_ref, wd_ref, o_ref):
    x = x_ref[...]
    h = _rms(x, g_ref[...]).astype(BF16)
    acc = x
    for cs in _mxu_chunks(D_FF, 1536):
        gate = _dot(h, wg_ref[:, cs])
        up = _dot(h, wu_ref[:, cs])
        act = (gate * _sigmoid(gate) * up).astype(BF16)
        acc = acc + _dot(act, wd_ref[cs, :].astype(BF16))
    o_ref[...] = acc


def _ffn(x, g, wg, wu, wd):
    t = x.shape[0]
    full = lambda a: pl.BlockSpec(a.shape, lambda i: (0, 0), pipeline_mode=pl.Buffered(1))
    return pl.pallas_call(
        _ffn_kernel,
        grid=(t // TM,),
        in_specs=[pl.BlockSpec((TM, D_MODEL), lambda i: (i, 0)), full(g), full(wg), full(wu), full(wd)],
        out_specs=pl.BlockSpec((TM, D_MODEL), lambda i: (i, 0)),
        out_shape=jax.ShapeDtypeStruct((t, D_MODEL), F32),
        compiler_params=_params(("parallel",)),
        name="ffn_dense",
    )(x, g, wg, wu, wd)


def _router_kernel(x_ref, g_ref, wr_ref, br_ref, h_ref, info_ref, info_t_ref, cnt_ref, carry_ref):
    @pl.when(pl.program_id(0) == 0)
    def _():
        carry_ref[...] = jnp.zeros(carry_ref.shape, F32)

    h = _rms(x_ref[...], g_ref[...])
    h_ref[...] = h
    w = wr_ref[...]
    h_hi = h.astype(BF16)
    w_hi = w.astype(BF16)
    h_lo = (h - h_hi.astype(F32)).astype(BF16)
    w_lo = (w - w_hi.astype(F32)).astype(BF16)
    logits = _dot(h_hi, w_hi) + _dot(h_lo, w_hi) + _dot(h_hi, w_lo) + br_ref[...]
    lane = lax.broadcasted_iota(jnp.int32, logits.shape, 1).astype(F32)
    logits = jnp.where(lane < N_EXPERTS, logits, NEG)
    v1 = jnp.max(logits, axis=1, keepdims=True)
    i1 = jnp.min(jnp.where(logits == v1, lane, float(LANES)), axis=1, keepdims=True)
    rest = jnp.where(lane == i1, NEG, logits)
    v2 = jnp.max(rest, axis=1, keepdims=True)
    i2 = jnp.min(jnp.where(rest == v2, lane, float(LANES)), axis=1, keepdims=True)
    e = jnp.exp(v2 - v1)
    w1 = 1.0 / (1.0 + e)
    w2 = e / (1.0 + e)
    chosen = jnp.where((lane == i1) | (lane == i2), 1.0, 0.0)
    row = lax.broadcasted_iota(jnp.int32, (TR, TR), 0)
    col = lax.broadcasted_iota(jnp.int32, (TR, TR), 1)
    before = jnp.where(col < row, 1.0, 0.0)
    pos = _dot(before, chosen) + carry_ref[0:1, :]
    r1 = jnp.sum(jnp.where(lane == i1, pos, 0.0), axis=1, keepdims=True)
    r2 = jnp.sum(jnp.where(lane == i2, pos, 0.0), axis=1, keepdims=True)
    carry_ref[0:1, :] = carry_ref[0:1, :] + jnp.sum(chosen, axis=0, keepdims=True)
    cnt_ref[...] = carry_ref[...]
    fields = (i1, i2, r1, r2, w1, w2)
    info = jnp.zeros(logits.shape, F32)
    for n, val in enumerate(fields):
        info = jnp.where(lane == n, val, info)
    info_ref[...] = info[:, :INFO_W]
    info_t_ref[...] = info.T[:INFO_W, :]


def _router(x, g, wr, br):
    t = x.shape[0]
    full = lambda a: pl.BlockSpec(a.shape, lambda i: (0, 0))
    return pl.pallas_call(
        _router_kernel,
        grid=(t // TR,),
        in_specs=[pl.BlockSpec((TR, D_MODEL), lambda i: (i, 0)), full(g), full(wr), full(br)],
        out_specs=[
            pl.BlockSpec((TR, D_MODEL), lambda i: (i, 0)),
            pl.BlockSpec((TR, INFO_W), lambda i: (i, 0)),
            pl.BlockSpec((INFO_W, TR), lambda i: (0, i)),
            pl.BlockSpec((8, LANES), lambda i: (0, 0)),
        ],
        out_shape=[
            jax.ShapeDtypeStruct((t, D_MODEL), F32),
            jax.ShapeDtypeStruct((t, INFO_W), F32),
            jax.ShapeDtypeStruct((INFO_W, t), F32),
            jax.ShapeDtypeStruct((8, LANES), F32),
        ],
        scratch_shapes=[pltpu.VMEM((8, LANES), F32)],
        compiler_params=_params(("arbitrary",)),
        name="moe_router",
    )(x, g, wr, br)


def _num_row_tiles(t):
    return (2 * t) // TMG + N_EXPERTS


def _route_plan(info_t, cnt, t):
    counts = cnt[0, :N_EXPERTS].astype(jnp.int32)
    padded = ((counts + TMG - 1) // TMG) * TMG
    ends = jnp.cumsum(padded)
    off = ends - padded
    expert = info_t[0:2].astype(jnp.int32)
    slot = info_t[2:4].astype(jnp.int32)
    for e in range(N_EXPERTS):
        slot = slot + jnp.where(expert == e, off[e], 0)
    slots = slot.reshape(2, t // BLK, BLK).transpose(1, 0, 2).reshape(-1)
    starts = jnp.arange(_num_row_tiles(t), dtype=jnp.int32) * TMG
    first_row = jnp.minimum(starts, ends[-1] - 1)
    tile_expert = jnp.sum(ends[None, :] <= first_row[:, None], axis=1)
    n_tiles = (ends[-1] // TMG).reshape(1)
    last_tile = jnp.where(padded > 0, ends - TMG, -1)
    tail = starts[-N_EXPERTS:]
    clear = jnp.concatenate([last_tile, jnp.where(tail >= ends[-1], tail, -1)])
    return (slots.astype(jnp.int32), clear.astype(jnp.int32),
            tile_expert.astype(jnp.int32), n_tiles.astype(jnp.int32))


def _row_copy(src_ref, src_row, dst_ref, dst_row, sem):
    return pltpu.make_async_copy(src_ref.at[pl.ds(src_row, 1), :], dst_ref.at[pl.ds(dst_row, 1), :], sem)


def _dispatch_kernel(slots_ref, clear_ref, h_ref, xs_ref, zero_ref, sem):
    @pl.when(pl.program_id(0) == 0)
    def _():
        zero_ref[...] = jnp.zeros(zero_ref.shape, F32)
        for n in range(2 * N_EXPERTS):
            @pl.when(clear_ref[n] >= 0)
            def _():
                start = pl.multiple_of(clear_ref[n], TMG)
                clear = pltpu.make_async_copy(zero_ref, xs_ref.at[pl.ds(start, TMG), :], sem)
                clear.start()
                clear.wait()

    base = pl.program_id(0) * (2 * BLK)

    def body(r, carry):
        for j in range(2):
            _row_copy(h_ref, r, xs_ref, slots_ref[base + j * BLK + r], sem).start()
        return carry

    lax.fori_loop(0, BLK, body, 0, unroll=ROW_UNROLL)
    for j in range(2):
        pltpu.make_async_copy(h_ref, xs_ref.at[pl.ds(0, BLK), :], sem).wait()


def _dispatch(slots, clear, h, n_rows):
    t = h.shape[0]
    return pl.pallas_call(
        _dispatch_kernel,
        grid_spec=pltpu.PrefetchScalarGridSpec(
            num_scalar_prefetch=2,
            grid=(t // BLK,),
            in_specs=[pl.BlockSpec((BLK, D_MODEL), lambda i, s, c: (i, 0))],
            out_specs=pl.BlockSpec(memory_space=pl.ANY),
            scratch_shapes=[pltpu.VMEM((TMG, D_MODEL), F32), pltpu.SemaphoreType.DMA(())],
        ),
        out_shape=jax.ShapeDtypeStruct((n_rows, D_MODEL), F32),
        compiler_params=_params(("arbitrary",)),
        name="moe_dispatch",
    )(slots, clear, h)


def _experts_kernel(te_ref, nt_ref, xs_ref, wg_ref, wu_ref, wd_ref, y_ref, xb_ref, acc_ref):
    del te_ref
    f = pl.program_id(1)
    active = pl.program_id(0) < nt_ref[0]

    @pl.when(f == 0)
    def _():
        xb_ref[...] = xs_ref[...].astype(BF16)
        acc_ref[...] = jnp.zeros(acc_ref.shape, F32)

    @pl.when(active)
    def _():
        h = xb_ref[...]
        part = jnp.zeros((TMG, D_MODEL), F32)
        for cs in _mxu_chunks(TF_G, 1024):
            gate = _dot(h, wg_ref[:, cs])
            up = _dot(h, wu_ref[:, cs])
            part = part + _dot((gate * _sigmoid(gate) * up).astype(BF16), wd_ref[cs, :].astype(BF16))
        acc_ref[...] += part

    @pl.when(f == pl.num_programs(1) - 1)
    def _():
        y_ref[...] = acc_ref[...]


def _experts(tile_expert, n_tiles, xs, wg, wu, wd):
    n_rows = xs.shape[0]
    nf = D_FF_EXPERT // TF_G

    def f_eff(g, f, nt):
        return jnp.where(g < nt[0], f, nf - 1)

    return pl.pallas_call(
        _experts_kernel,
        grid_spec=pltpu.PrefetchScalarGridSpec(
            num_scalar_prefetch=2,
            grid=(n_rows // TMG, nf),
            in_specs=[
                pl.BlockSpec((TMG, D_MODEL), lambda g, f, te, nt: (g, 0)),
                pl.BlockSpec((None, D_MODEL, TF_G), lambda g, f, te, nt: (te[g], 0, f_eff(g, f, nt))),
                pl.BlockSpec((None, D_MODEL, TF_G), lambda g, f, te, nt: (te[g], 0, f_eff(g, f, nt))),
                pl.BlockSpec((None, TF_G, D_MODEL), lambda g, f, te, nt: (te[g], f_eff(g, f, nt), 0)),
            ],
            out_specs=pl.BlockSpec((TMG, D_MODEL), lambda g, f, te, nt: (g, 0)),
            scratch_shapes=[pltpu.VMEM((TMG, D_MODEL), BF16), pltpu.VMEM((TMG, D_MODEL), F32)],
        ),
        out_shape=jax.ShapeDtypeStruct((n_rows, D_MODEL), F32),
        compiler_params=_params(("arbitrary", "arbitrary")),
        name="moe_experts",
    )(tile_expert, n_tiles, xs, wg, wu, wd)


def _combine_kernel(slots_ref, x_ref, info_ref, fg_ref, y_ref, o_ref, rows_ref, sem):
    base = pl.program_id(0) * (2 * BLK)

    def body(r, carry):
        for j in range(2):
            _row_copy(y_ref, slots_ref[base + j * BLK + r], rows_ref.at[j], r, sem).start()
        return carry

    lax.fori_loop(0, BLK, body, 0, unroll=ROW_UNROLL)
    for j in range(2):
        pltpu.make_async_copy(y_ref.at[pl.ds(0, BLK), :], rows_ref.at[j], sem).wait()
    w1 = info_ref[:, 4:5]
    w2 = info_ref[:, 5:6]
    o_ref[...] = _rms(x_ref[...] + w1 * rows_ref[0] + w2 * rows_ref[1], fg_ref[...])


def _combine(slots, x, info, final_g, y):
    t = x.shape[0]
    return pl.pallas_call(
        _combine_kernel,
        grid_spec=pltpu.PrefetchScalarGridSpec(
            num_scalar_prefetch=1,
            grid=(t // BLK,),
            in_specs=[pl.BlockSpec((BLK, D_MODEL), lambda i, s: (i, 0)),
                      pl.BlockSpec((BLK, INFO_W), lambda i, s: (i, 0)),
                      pl.BlockSpec((1, D_MODEL), lambda i, s: (0, 0)),
                      pl.BlockSpec(memory_space=pl.ANY)],
            out_specs=pl.BlockSpec((BLK, D_MODEL), lambda i, s: (i, 0)),
            scratch_shapes=[pltpu.VMEM((2, BLK, D_MODEL), F32), pltpu.SemaphoreType.DMA(())],
        ),
        out_shape=jax.ShapeDtypeStruct((t, D_MODEL), F32),
        compiler_params=_params(("arbitrary",)),
        name="moe_combine",
    )(slots, x, info, final_g, y)


def _block_diag(w):
    g, c, _ = w.shape
    eye = jnp.eye(g, dtype=w.dtype)
    return (w[:, :, None, :] * eye[:, None, :, None]).reshape(g * c, g * c)


def _rotate_half_cols(w):
    half = w.shape[-1] // 2
    return jnp.concatenate([-w[..., half:], w[..., :half]], axis=-1)


def _rope_tables():
    pos = np.arange(SEQ, dtype=np.float32)
    inv = np.float32(ROPE_THETA) ** (-np.arange(0, QK_ROPE, 2, dtype=np.float32) / np.float32(QK_ROPE))
    ang = pos[:, None] * inv[None, :]
    cos, sin = jnp.asarray(np.cos(ang)), jnp.asarray(np.sin(ang))
    one = jnp.ones((SEQ, QK_NOPE), F32)
    z_nope = jnp.zeros((SEQ, QK_NOPE), F32)
    z_pad = jnp.zeros((SEQ, HEAD_PAD - QK_NOPE - QK_ROPE), F32)
    z_rest = jnp.zeros((SEQ, LANES - QK_ROPE), F32)
    cq_tab = jnp.concatenate([one, cos, cos, z_pad], axis=1)
    sq_tab = jnp.concatenate([z_nope, sin, sin, z_pad], axis=1)
    ck_tab = jnp.concatenate([cos, cos, z_rest], axis=1)
    sk_tab = jnp.concatenate([sin, sin, z_rest], axis=1)
    return cq_tab, sq_tab, ck_tab, sk_tab


def _rope_placement():
    r = jnp.arange(LANES)[:, None]
    c = jnp.arange(MLA_HEADS * HEAD_PAD)[None, :]
    return ((r < QK_ROPE) & (c % HEAD_PAD == QK_NOPE + r)).astype(BF16)


def _split_in_proj(w):
    bounds = (0, 256, 512, 640, 672, 1184, 1696, w.shape[1])
    zp, cq, ckv, kr, zc, zs, zg = [w[:, a:b] for a, b in zip(bounds[:-1], bounds[1:])]
    pad = jnp.zeros((D_MODEL, LANES - 2 * QK_ROPE), w.dtype)
    small = jnp.concatenate([zc, zs, zp, cq, ckv, kr, _rotate_half_cols(kr), pad], axis=1)
    return small.astype(BF16), zg.astype(BF16)


def _mla_weights(w_uq, w_ukv):
    wq = w_uq.reshape(Q_LORA, MLA_HEADS, QK_NOPE + QK_ROPE)
    nope, rope = wq[..., :QK_NOPE], wq[..., QK_NOPE:]
    zq = jnp.zeros((Q_LORA, MLA_HEADS, HEAD_PAD - QK_NOPE - QK_ROPE), w_uq.dtype)
    wqa = jnp.concatenate([nope, rope, zq], axis=-1).reshape(Q_LORA, MLA_HEADS * HEAD_PAD)
    wqb = jnp.concatenate([jnp.zeros_like(nope), _rotate_half_cols(rope), zq], axis=-1)
    wqb = wqb.reshape(Q_LORA, MLA_HEADS * HEAD_PAD)
    wkv = w_ukv.reshape(KV_LORA, MLA_HEADS, QK_NOPE + V_HEAD)
    k_nope, v = wkv[..., :QK_NOPE], wkv[..., QK_NOPE:]
    zk = jnp.zeros((KV_LORA, MLA_HEADS, HEAD_PAD - QK_NOPE), w_ukv.dtype)
    wk = jnp.concatenate([k_nope, zk], axis=-1).reshape(KV_LORA, MLA_HEADS * HEAD_PAD)
    zv = jnp.zeros((KV_LORA, MLA_HEADS, V_ROWS - V_HEAD), w_ukv.dtype)
    wv = jnp.concatenate([v, zv], axis=-1).reshape(KV_LORA, MLA_HEADS * V_ROWS)
    return wqa.astype(BF16), wqb.astype(BF16), wk.astype(BF16), wv.astype(BF16)


def kernel(x, norm_mix_g, w_in, b_gate, w_pool, pool_scale, w_pool_out, q_norm_g, w_uq, kv_norm_g, w_ukv, w_mla_out, conv_w, conv_b, conv_ln_g, conv_ln_b, w_conv_out, sgu_ln_g, sgu_ln_b, sgu_w, sgu_b, w_sgu_out, w_o, norm_ffn_g, w_ffn_gate, w_ffn_up, w_ffn_down, w_router, b_router, w_moe_gate, w_moe_up, w_moe_down, final_norm_g):
    batch, seq, d = x.shape
    assert (seq, d) == (SEQ, D_MODEL)
    depth = w_in.shape[0]
    xt = x.reshape(batch * seq, d)
    tabs = _rope_tables()
    place = _rope_placement()
    row = lambda a: a.reshape(1, -1)

    for l in range(depth):
        w_small, w_gate = _split_in_proj(w_in[l])
        zs = _in_proj(xt, row(norm_mix_g[l]), w_small)

        w_bd = _block_diag(w_pool[l])
        m_pool = _pool(zs, w_bd.astype(BF16), row(pool_scale[l]))
        m_conv = _conv(zs, conv_w[l], row(conv_b[l]), row(conv_ln_g[l]), row(conv_ln_b[l]))
        bias_full = jnp.repeat(sgu_b[l].T, SGU_WIDTH // SGU_GROUPS, axis=1)
        m_sgu = _sgu(zs, row(sgu_ln_g[l]), row(sgu_ln_b[l]),
                     sgu_w[l].reshape(SGU_GROUPS * SGU_LEN, SGU_LEN), bias_full)
        wqa, wqb, wk, wv = _mla_weights(w_uq[l], w_ukv[l])
        q, k, v = _mla_prep(zs, row(q_norm_g[l]), row(kv_norm_g[l]), wqa, wqb, wk, wv, place, tabs)
        m_mla = _attention(q, k, v, batch)

        xt = _merge(xt, row(norm_mix_g[l]), m_pool, m_mla, m_conv, m_sgu, w_gate, row(b_gate[l]),
                    w_pool_out[l].astype(BF16), w_mla_out[l].astype(BF16),
                    w_conv_out[l].astype(BF16), w_sgu_out[l].astype(BF16), w_o[l].astype(BF16))

        j = l // 2
        if l % 2 == 0:
            xt = _ffn(xt, row(norm_ffn_g[l]), w_ffn_gate[j].astype(BF16),
                      w_ffn_up[j].astype(BF16), w_ffn_down[j])
        else:
            wr = jnp.pad(w_router[j], ((0, 0), (0, LANES - N_EXPERTS)))
            br = jnp.pad(b_router[j], (0, LANES - N_EXPERTS)).reshape(1, LANES)
            t = xt.shape[0]
            h, info, info_t, cnt = _router(xt, row(norm_ffn_g[l]), wr, br)
            slots, clear, tile_expert, n_tiles = _route_plan(info_t, cnt, t)
            xs = _dispatch(slots, clear, h, _num_row_tiles(t) * TMG)
            y = _experts(tile_expert, n_tiles, xs, w_moe_gate[j].astype(BF16),
                         w_moe_up[j].astype(BF16), w_moe_down[j])
            assert l == depth - 1
            xt = _combine(slots, xt, info, row(final_norm_g), y)
    return xt.reshape(batch, seq, d)
```

```python
import jax
import jax.numpy as jnp
import numpy as np
from jax import lax
from jax.experimental import pallas as pl
from jax.experimental.pallas import tpu as pltpu

F32 = jnp.float32
BF16 = jnp.bfloat16

D_MODEL = 1024
SEQ = 8192
CHUNK = 64
POOL_WIDTH = 256
POOL_WINDOWS = (2, 4, 8, 16)
MLA_HEADS = 8
Q_LORA = 256
KV_LORA = 128
QK_NOPE = 64
QK_ROPE = 32
V_HEAD = 64
V_ROWS = 80
ROPE_THETA = 10000.0
CONV_WIDTH = 256
CONV_K = 31
SGU_WIDTH = 256
SGU_GROUPS = 4
SGU_LEN = 128
D_FF = 2816
N_EXPERTS = 8
D_FF_EXPERT = 3584
EPS = 1e-6
NEG = -1e30
LOG2E = 1.4426950408889634

LANES = 128
SUBLANES = 8
MXU_TILE = 256
HEAD_PAD = 128
ZC_COL, ZS_COL, ZP_COL, CQ_COL, CKV_COL, KR_COL = 0, 512, 1024, 1280, 1536, 1664
Z_SMALL = 1792
POOL_HALO = 32
CONV_HALO = 32

TM = 512
TQ = 512
TK = 512
TR = 512
BLK = 1024
ROW_UNROLL = 16
TMG = 512
TF_G = 1792
INFO_W = 8
VMEM_LIMIT = 56 * 1024 * 1024


def _params(sem):
    return pltpu.CompilerParams(dimension_semantics=sem, vmem_limit_bytes=VMEM_LIMIT)


def _rms(x, g):
    return x * lax.rsqrt(jnp.mean(x * x, axis=-1, keepdims=True) + EPS) * g


def _layernorm(x, g, b):
    mu = jnp.mean(x, axis=-1, keepdims=True)
    xc = x - mu
    var = jnp.mean(xc * xc, axis=-1, keepdims=True)
    return xc * lax.rsqrt(var + EPS) * g + b


def _sigmoid(x):
    return 0.5 * jnp.tanh(0.5 * x) + 0.5


def _dot(a, b):
    return jnp.dot(a, b, preferred_element_type=F32)


def _mxu_chunks(total, max_chunk):
    assert total % MXU_TILE == 0 and max_chunk % MXU_TILE == 0
    return [slice(s, min(s + max_chunk, total)) for s in range(0, total, max_chunk)]


def _in_proj_kernel(x_ref, g_ref, ws_ref, zs_ref):
    h = _rms(x_ref[...], g_ref[...]).astype(BF16)
    for cs in _mxu_chunks(Z_SMALL, 1024):
        zs_ref[:, cs] = _dot(h, ws_ref[:, cs])


def _in_proj(x, g, w_small):
    t = x.shape[0]
    return pl.pallas_call(
        _in_proj_kernel,
        grid=(t // TM,),
        in_specs=[
            pl.BlockSpec((TM, D_MODEL), lambda i: (i, 0)),
            pl.BlockSpec((1, D_MODEL), lambda i: (0, 0)),
            pl.BlockSpec((D_MODEL, Z_SMALL), lambda i: (0, 0)),
        ],
        out_specs=pl.BlockSpec((TM, Z_SMALL), lambda i: (i, 0)),
        out_shape=jax.ShapeDtypeStruct((t, Z_SMALL), F32),
        compiler_params=_params(("parallel",)),
        name="in_proj",
    )(x, g, w_small)


def _pool_kernel(z_ref, halo_ref, w_ref, scale_ref, o_ref, s1_ref, s2_ref, s4_ref, s8_ref):
    i = pl.program_id(0)
    pos0 = (i * TM) % SEQ
    z = z_ref[...]
    rows = TM + POOL_HALO
    s1_ref[0:POOL_HALO, :] = jnp.where(pos0 == 0, 0.0, halo_ref[...])
    s1_ref[POOL_HALO:, :] = z

    def double(src_ref, dst_ref, shift):
        dst_ref[0:SUBLANES, :] = jnp.zeros((SUBLANES, POOL_WIDTH), F32)
        dst_ref[SUBLANES:, :] = src_ref[SUBLANES:, :] + src_ref[SUBLANES - shift:rows - shift, :]

    double(s1_ref, s2_ref, 1)
    double(s2_ref, s4_ref, 2)
    double(s4_ref, s8_ref, 4)
    s8 = s8_ref[POOL_HALO:, :]
    sums = (s2_ref[POOL_HALO:, :], s4_ref[POOL_HALO:, :], s8, s8 + s8_ref[POOL_HALO - 8:rows - 8, :])
    lane = lax.broadcasted_iota(jnp.int32, (TM, POOL_WIDTH), 1)
    pos = lax.broadcasted_iota(jnp.int32, (TM, POOL_WIDTH), 0) + pos0
    group = lane // (POOL_WIDTH // len(POOL_WINDOWS))
    assert POOL_WINDOWS == (2, 4, 8, 16)
    total = jnp.zeros_like(z)
    win = jnp.zeros_like(lane)
    for gi, w in enumerate(POOL_WINDOWS):
        total = jnp.where(group == gi, sums[gi], total)
        win = jnp.where(group == gi, w, win)
    cnt = jnp.minimum(pos + 1, win).astype(F32)
    pooled = total / cnt - z
    y = _dot(pooled.astype(BF16), w_ref[...]) * scale_ref[...]
    o_ref[...] = y.astype(BF16)


def _pool(zs, w_bd, scale):
    t = zs.shape[0]
    cb = ZP_COL // POOL_WIDTH
    r = TM // POOL_HALO
    return pl.pallas_call(
        _pool_kernel,
        grid=(t // TM,),
        in_specs=[
            pl.BlockSpec((TM, POOL_WIDTH), lambda i: (i, cb)),
            pl.BlockSpec((POOL_HALO, POOL_WIDTH), lambda i: (jnp.maximum(i * r - 1, 0), cb)),
            pl.BlockSpec((POOL_WIDTH, POOL_WIDTH), lambda i: (0, 0)),
            pl.BlockSpec((1, POOL_WIDTH), lambda i: (0, 0)),
        ],
        out_specs=pl.BlockSpec((TM, POOL_WIDTH), lambda i: (i, 0)),
        out_shape=jax.ShapeDtypeStruct((t, POOL_WIDTH), BF16),
        scratch_shapes=[pltpu.VMEM((TM + POOL_HALO, POOL_WIDTH), F32)] * 4,
        compiler_params=_params(("parallel",)),
        name="pool_mixer",
    )(zs, zs, w_bd, scale)


def _glu(z2):
    return z2[:, :CONV_WIDTH] * _sigmoid(z2[:, CONV_WIDTH:])


def _conv_kernel(z_ref, halo_ref, w_ref, b_ref, lg_ref, lb_ref, o_ref, buf_ref, sh_ref):
    i = pl.program_id(0)
    pos0 = (i * TM) % SEQ
    buf_ref[0:CONV_HALO, :] = jnp.where(pos0 == 0, 0.0, _glu(halo_ref[...]))
    buf_ref[CONV_HALO:, :] = _glu(z_ref[...])
    span = TM + CONV_HALO - SUBLANES
    for ph in range(1, SUBLANES):
        sh_ref[ph - 1, 0:span, :] = buf_ref[ph:ph + span, :]
    off = CONV_HALO - (CONV_K - 1)
    y = jnp.zeros((TM, CONV_WIDTH), F32) + b_ref[...]
    for k in range(CONV_K):
        ph = (off + k) % SUBLANES
        r0 = off + k - ph
        tap = buf_ref[r0:r0 + TM, :] if ph == 0 else sh_ref[ph - 1, r0:r0 + TM, :]
        y = y + w_ref[k:k + 1, :] * tap
    yn = _layernorm(y, lg_ref[...], lb_ref[...])
    o_ref[...] = (yn * _sigmoid(yn)).astype(BF16)


def _conv(zs, conv_w, conv_b, ln_g, ln_b):
    t = zs.shape[0]
    cb = ZC_COL // (2 * CONV_WIDTH)
    r = TM // CONV_HALO
    vec = pl.BlockSpec((1, CONV_WIDTH), lambda i: (0, 0))
    return pl.pallas_call(
        _conv_kernel,
        grid=(t // TM,),
        in_specs=[
            pl.BlockSpec((TM, 2 * CONV_WIDTH), lambda i: (i, cb)),
            pl.BlockSpec((CONV_HALO, 2 * CONV_WIDTH), lambda i: (jnp.maximum(i * r - 1, 0), cb)),
            pl.BlockSpec((CONV_K, CONV_WIDTH), lambda i: (0, 0)),
            vec, vec, vec,
        ],
        out_specs=pl.BlockSpec((TM, CONV_WIDTH), lambda i: (i, 0)),
        out_shape=jax.ShapeDtypeStruct((t, CONV_WIDTH), BF16),
        scratch_shapes=[pltpu.VMEM((TM + CONV_HALO, CONV_WIDTH), F32),
                        pltpu.VMEM((SUBLANES - 1, TM + CONV_HALO, CONV_WIDTH), F32)],
        compiler_params=_params(("parallel",)),
        name="conv_mixer",
    )(zs, zs, conv_w, conv_b, ln_g, ln_b)


def _sgu_kernel(z_ref, lg_ref, lb_ref, ws_ref, bias_ref, o_ref):
    z = z_ref[...]
    z = 0.5 * z * (1.0 + lax.erf(z * (2.0 ** -0.5)))
    u = z[:, :SGU_WIDTH]
    v = _layernorm(z[:, SGU_WIDTH:], lg_ref[...], lb_ref[...]).astype(BF16)
    rows = SGU_GROUPS * SGU_LEN
    row = lax.broadcasted_iota(jnp.int32, (rows, SGU_LEN), 0) % SGU_LEN
    col = lax.broadcasted_iota(jnp.int32, (rows, SGU_LEN), 1)
    ws = jnp.where(col <= row, ws_ref[...], 0.0).astype(BF16)
    lane_group = lax.broadcasted_iota(jnp.int32, (SGU_LEN, SGU_WIDTH), 1) // (SGU_WIDTH // SGU_GROUPS)
    for blk in range(TM // SGU_LEN):
        r0 = blk * SGU_LEN
        full = _dot(ws, v[r0:r0 + SGU_LEN, :])
        mixed = full[0:SGU_LEN, :]
        for g in range(1, SGU_GROUPS):
            mixed = jnp.where(lane_group == g, full[g * SGU_LEN:(g + 1) * SGU_LEN, :], mixed)
        mixed = mixed + bias_ref[...]
        o_ref[r0:r0 + SGU_LEN, :] = (u[r0:r0 + SGU_LEN, :] * mixed).astype(BF16)


def _sgu(zs, ln_g, ln_b, ws_stack, bias_full):
    t = zs.shape[0]
    cb = ZS_COL // (2 * SGU_WIDTH)
    vec = pl.BlockSpec((1, SGU_WIDTH), lambda i: (0, 0))
    return pl.pallas_call(
        _sgu_kernel,
        grid=(t // TM,),
        in_specs=[
            pl.BlockSpec((TM, 2 * SGU_WIDTH), lambda i: (i, cb)),
            vec, vec,
            pl.BlockSpec((SGU_GROUPS * SGU_LEN, SGU_LEN), lambda i: (0, 0)),
            pl.BlockSpec((SGU_LEN, SGU_WIDTH), lambda i: (0, 0)),
        ],
        out_specs=pl.BlockSpec((TM, SGU_WIDTH), lambda i: (i, 0)),
        out_shape=jax.ShapeDtypeStruct((t, SGU_WIDTH), BF16),
        compiler_params=_params(("parallel",)),
        name="sgu_mixer",
    )(zs, ln_g, ln_b, ws_stack, bias_full)


def _tile_heads(tab):
    return jnp.concatenate([tab] * MLA_HEADS, axis=1)


def _mla_prep_kernel(cq_ref, ckv_ref, kr_ref, qg_ref, kvg_ref, wqa_ref, wqb_ref, wk_ref, wv_ref,
                     place_ref, cq_tab_ref, sq_tab_ref, ck_tab_ref, sk_tab_ref,
                     qt_ref, k_ref, vt_ref):
    cqn = _rms(cq_ref[...], qg_ref[...]).astype(BF16)
    qa = _dot(cqn, wqa_ref[...])
    qb = _dot(cqn, wqb_ref[...])
    scale = (QK_NOPE + QK_ROPE) ** -0.5 * LOG2E
    q = (qa * _tile_heads(cq_tab_ref[...]) + qb * _tile_heads(sq_tab_ref[...])) * scale
    qt_ref[...] = q.T.astype(BF16)
    kvn = _rms(ckv_ref[...], kvg_ref[...]).astype(BF16)
    vt = _dot(kvn, wv_ref[...]).T
    ones_row = lax.broadcasted_iota(jnp.int32, vt.shape, 0) % V_ROWS == V_HEAD
    vt = jnp.where(ones_row, 1.0, vt)
    for c in range(TM // TK):
        vt_ref[c] = vt[:, c * TK:(c + 1) * TK].astype(BF16)
    kr = kr_ref[...]
    kro = kr * ck_tab_ref[...] + pltpu.roll(kr, LANES - QK_ROPE, 1) * sk_tab_ref[...]
    k = _dot(kvn, wk_ref[...]) + _dot(kro.astype(BF16), place_ref[...])
    k_ref[...] = k.astype(BF16)


def _mla_prep(zs, q_g, kv_g, wqa, wqb, wk, wv, place, tabs):
    t = zs.shape[0]
    hw = MLA_HEADS * HEAD_PAD
    hv = MLA_HEADS * V_ROWS
    ns = SEQ // TM
    tab = pl.BlockSpec((TM, LANES), lambda i: (i % ns, 0))
    full = lambda a: pl.BlockSpec(a.shape, lambda i: (0, 0))
    return pl.pallas_call(
        _mla_prep_kernel,
        grid=(t // TM,),
        in_specs=[
            pl.BlockSpec((TM, Q_LORA), lambda i: (i, CQ_COL // Q_LORA)),
            pl.BlockSpec((TM, KV_LORA), lambda i: (i, CKV_COL // KV_LORA)),
            pl.BlockSpec((TM, LANES), lambda i: (i, KR_COL // LANES)),
            full(q_g), full(kv_g), full(wqa), full(wqb), full(wk), full(wv), full(place),
            tab, tab, tab, tab,
        ],
        out_specs=[
            pl.BlockSpec((hw, TM), lambda i: (0, i)),
            pl.BlockSpec((TM, hw), lambda i: (i, 0)),
            pl.BlockSpec((TM // TK, hv, TK), lambda i: (i, 0, 0)),
        ],
        out_shape=[
            jax.ShapeDtypeStruct((hw, t), BF16),
            jax.ShapeDtypeStruct((t, hw), BF16),
            jax.ShapeDtypeStruct((t // TK, hv, TK), BF16),
        ],
        compiler_params=_params(("parallel",)),
        name="mla_prep",
    )(zs, zs, zs, q_g, kv_g, wqa, wqb, wk, wv, place, *tabs)


def _attn_kernel(qt_ref, k_ref, vt_ref, o_ref, m_ref, acc_ref, st0_ref, st1_ref, mx_ref):
    qi = pl.program_id(1)
    m_ref[...] = jnp.full(m_ref.shape, NEG, F32)
    acc_ref[...] = jnp.zeros(acc_ref.shape, F32)
    bufs = (st0_ref, st1_ref)

    def scores_head(h, j, slot, masked):
        ks = pl.multiple_of(j * TK, TK)
        hs = slice(h * HEAD_PAD, (h + 1) * HEAD_PAD)
        st = _dot(k_ref[pl.ds(ks, TK), hs], qt_ref[hs, :])
        if masked:
            key_chunk = lax.broadcasted_iota(jnp.int32, (TK, TQ), 0) // CHUNK
            qry_chunk = lax.broadcasted_iota(jnp.int32, (TK, TQ), 1) // CHUNK
            st = jnp.where(key_chunk <= qry_chunk, st, NEG)
        bufs[slot][h] = st
        mx_ref[slot, h] = jnp.max(st, axis=0, keepdims=True)

    def update_head(h, j, slot):
        m_prev = m_ref[h]
        m_new = jnp.maximum(m_prev, mx_ref[slot, h])
        alpha = jnp.exp2(m_prev - m_new)
        p = jnp.exp2(bufs[slot][h] - m_new).astype(BF16)
        m_ref[h] = m_new
        rows = slice(h * V_ROWS, (h + 1) * V_ROWS)
        acc_ref[rows, :] = alpha * acc_ref[rows, :] + _dot(vt_ref[j, rows, :], p)

    def scores(j, slot, masked):
        for h in range(MLA_HEADS):
            scores_head(h, j, slot, masked)

    def update(j, slot):
        for h in range(MLA_HEADS):
            update_head(h, j, slot)

    def scores_and_update(j_next, slot_next, masked, j, slot):
        for h in range(MLA_HEADS):
            scores_head(h, j_next, slot_next, masked)
            update_head(h, j, slot)

    @pl.when(qi > 0)
    def _():
        scores(0, 0, False)

    def pair(i, carry):
        scores_and_update(2 * i + 1, 1, False, 2 * i, 0)
        scores_and_update(2 * i + 2, 0, False, 2 * i + 1, 1)
        return carry

    pairs = jnp.maximum(qi - 1, 0) // 2
    lax.fori_loop(0, pairs, pair, 0)
    done = 2 * pairs

    @pl.when(qi == 0)
    def _():
        scores(0, 0, True)
        update(0, 0)

    @pl.when(qi % 2 == 1)
    def _():
        scores_and_update(qi, 1, True, done, 0)
        update(qi, 1)

    @pl.when((qi > 0) & (qi % 2 == 0))
    def _():
        scores_and_update(qi - 1, 1, False, done, 0)
        scores_and_update(qi, 0, True, qi - 1, 1)
        update(qi, 0)

    outs = [acc_ref[h * V_ROWS:h * V_ROWS + V_HEAD, :] / acc_ref[h * V_ROWS + V_HEAD:h * V_ROWS + V_HEAD + 1, :]
            for h in range(MLA_HEADS)]
    o_ref[...] = jnp.concatenate(outs, axis=0).T.astype(BF16)


def _attention(qt, k, vt, batch):
    t = k.shape[0]
    nq = SEQ // TQ
    nk = SEQ // TK
    hw = MLA_HEADS * HEAD_PAD
    hv = MLA_HEADS * V_HEAD
    hr = MLA_HEADS * V_ROWS
    once = pl.Buffered(1)
    return pl.pallas_call(
        _attn_kernel,
        grid=(batch, nq),
        in_specs=[
            pl.BlockSpec((hw, TQ), lambda b, i: (0, b * nq + i)),
            pl.BlockSpec((SEQ, hw), lambda b, i: (b, 0), pipeline_mode=once),
            pl.BlockSpec((nk, hr, TK), lambda b, i: (b, 0, 0), pipeline_mode=once),
        ],
        out_specs=pl.BlockSpec((TQ, hv), lambda b, i: (b * nq + i, 0)),
        out_shape=jax.ShapeDtypeStruct((t, hv), BF16),
        scratch_shapes=[
            pltpu.VMEM((MLA_HEADS, 1, TQ), F32),
            pltpu.VMEM((hr, TQ), F32),
            pltpu.VMEM((MLA_HEADS, TK, TQ), F32),
            pltpu.VMEM((MLA_HEADS, TK, TQ), F32),
            pltpu.VMEM((2, MLA_HEADS, 1, TQ), F32),
        ],
        compiler_params=_params(("parallel", "arbitrary")),
        name="mla_attention",
    )(qt, k, vt)


def _merge_kernel(x_ref, ng_ref, mp_ref, ma_ref, mc_ref, ms_ref, wgate_ref, bg_ref,
                  wp_ref, wa_ref, wc_ref, ws_ref, wo_ref, o_ref, mg_ref):
    x = x_ref[...]
    h = _rms(x, ng_ref[...]).astype(BF16)
    branches = ((mp_ref, wp_ref), (ma_ref, wa_ref), (mc_ref, wc_ref), (ms_ref, ws_ref))
    half = D_MODEL // 2
    for n0 in range(0, D_MODEL, half):
        merged = jnp.zeros((TM, half), F32)
        for b, (m_ref, w_ref) in enumerate(branches):
            c0 = b * D_MODEL + n0
            gate = _sigmoid(_dot(h, wgate_ref[:, c0:c0 + half]) + bg_ref[:, c0:c0 + half])
            merged = merged + gate * _dot(m_ref[...], w_ref[:, n0:n0 + half])
        mg_ref[:, n0:n0 + half] = merged.astype(BF16)
    o_ref[...] = x + _dot(mg_ref[...], wo_ref[...])


def _merge(x, norm_g, mp, ma, mc, ms, w_gate, b_gate, wp, wa, wc, ws, wo):
    t = x.shape[0]
    row = lambda w: pl.BlockSpec((TM, w), lambda i: (i, 0))
    full = lambda a: pl.BlockSpec(a.shape, lambda i: (0, 0))
    return pl.pallas_call(
        _merge_kernel,
        grid=(t // TM,),
        in_specs=[row(D_MODEL), full(norm_g), row(POOL_WIDTH), row(MLA_HEADS * V_HEAD), row(CONV_WIDTH),
                  row(SGU_WIDTH), full(w_gate), full(b_gate), full(wp), full(wa), full(wc), full(ws), full(wo)],
        out_specs=row(D_MODEL),
        out_shape=jax.ShapeDtypeStruct((t, D_MODEL), F32),
        scratch_shapes=[pltpu.VMEM((TM, D_MODEL), BF16)],
        compiler_params=_params(("parallel",)),
        name="merge_out",
    )(x, norm_g, mp, ma, mc, ms, w_gate, b_gate, wp, wa, wc, ws, wo)


def _ffn_kernel(x_ref, g_ref, wg_ref, wu_ref, wd_ref, o_ref):
    x = x_ref[...]
    h = _rms(x, g_ref[...]).astype(BF16)
    acc = x
    for cs in _mxu_chunks(D_FF, 1536):
        gate = _dot(h, wg_ref[:, cs])
        up = _dot(h, wu_ref[:, cs])
        act = (gate * _sigmoid(gate) * up).astype(BF16)
        acc = acc + _dot(act, wd_ref[cs, :].astype(BF16))
    o_ref[...] = acc


def _ffn(x, g, wg, wu, wd):
    t = x.shape[0]
    full = lambda a: pl.BlockSpec(a.shape, lambda i: (0, 0), pipeline_mode=pl.Buffered(1))
    return pl.pallas_call(
        _ffn_kernel,
        grid=(t // TM,),
        in_specs=[pl.BlockSpec((TM, D_MODEL), lambda i: (i, 0)), full(g), full(wg), full(wu), full(wd)],
        out_specs=pl.BlockSpec((TM, D_MODEL), lambda i: (i, 0)),
        out_shape=jax.ShapeDtypeStruct((t, D_MODEL), F32),
        compiler_params=_params(("parallel",)),
        name="ffn_dense",
    )(x, g, wg, wu, wd)


def _router_kernel(x_ref, g_ref, wr_ref, br_ref, h_ref, info_ref, info_t_ref, cnt_ref, carry_ref):
    @pl.when(pl.program_id(0) == 0)
    def _():
        carry_ref[...] = jnp.zeros(carry_ref.shape, F32)

    h = _rms(x_ref[...], g_ref[...])
    h_ref[...] = h
    w = wr_ref[...]
    h_hi = h.astype(BF16)
    w_hi = w.astype(BF16)
    h_lo = (h - h_hi.astype(F32)).astype(BF16)
    w_lo = (w - w_hi.astype(F32)).astype(BF16)
    logits = _dot(h_hi, w_hi) + _dot(h_lo, w_hi) + _dot(h_hi, w_lo) + br_ref[...]
    lane = lax.broadcasted_iota(jnp.int32, logits.shape, 1).astype(F32)
    logits = jnp.where(lane < N_EXPERTS, logits, NEG)
    v1 = jnp.max(logits, axis=1, keepdims=True)
    i1 = jnp.min(jnp.where(logits == v1, lane, float(LANES)), axis=1, keepdims=True)
    rest = jnp.where(lane == i1, NEG, logits)
    v2 = jnp.max(rest, axis=1, keepdims=True)
    i2 = jnp.min(jnp.where(rest == v2, lane, float(LANES)), axis=1, keepdims=True)
    e = jnp.exp(v2 - v1)
    w1 = 1.0 / (1.0 + e)
    w2 = e / (1.0 + e)
    chosen = jnp.where((lane == i1) | (lane == i2), 1.0, 0.0)
    row = lax.broadcasted_iota(jnp.int32, (TR, TR), 0)
    col = lax.broadcasted_iota(jnp.int32, (TR, TR), 1)
    before = jnp.where(col < row, 1.0, 0.0)
    pos = _dot(before, chosen) + carry_ref[0:1, :]
    r1 = jnp.sum(jnp.where(lane == i1, pos, 0.0), axis=1, keepdims=True)
    r2 = jnp.sum(jnp.where(lane == i2, pos, 0.0), axis=1, keepdims=True)
    carry_ref[0:1, :] = carry_ref[0:1, :] + jnp.sum(chosen, axis=0, keepdims=True)
    cnt_ref[...] = carry_ref[...]
    fields = (i1, i2, r1, r2, w1, w2)
    info = jnp.zeros(logits.shape, F32)
    for n, val in enumerate(fields):
        info = jnp.where(lane == n, val, info)
    info_ref[...] = info[:, :INFO_W]
    info_t_ref[...] = info.T[:INFO_W, :]


def _router(x, g, wr, br):
    t = x.shape[0]
    full = lambda a: pl.BlockSpec(a.shape, lambda i: (0, 0))
    return pl.pallas_call(
        _router_kernel,
        grid=(t // TR,),
        in_specs=[pl.BlockSpec((TR, D_MODEL), lambda i: (i, 0)), full(g), full(wr), full(br)],
        out_specs=[
            pl.BlockSpec((TR, D_MODEL), lambda i: (i, 0)),
            pl.BlockSpec((TR, INFO_W), lambda i: (i, 0)),
            pl.BlockSpec((INFO_W, TR), lambda i: (0, i)),
            pl.BlockSpec((8, LANES), lambda i: (0, 0)),
        ],
        out_shape=[
            jax.ShapeDtypeStruct((t, D_MODEL), F32),
            jax.ShapeDtypeStruct((t, INFO_W), F32),
            jax.ShapeDtypeStruct((INFO_W, t), F32),
            jax.ShapeDtypeStruct((8, LANES), F32),
        ],
        scratch_shapes=[pltpu.VMEM((8, LANES), F32)],
        compiler_params=_params(("arbitrary",)),
        name="moe_router",
    )(x, g, wr, br)


def _num_row_tiles(t):
    return (2 * t) // TMG + N_EXPERTS


def _route_plan(info_t, cnt, t):
    counts = cnt[0, :N_EXPERTS].astype(jnp.int32)
    padded = ((counts + TMG - 1) // TMG) * TMG
    ends = jnp.cumsum(padded)
    off = ends - padded
    expert = info_t[0:2].astype(jnp.int32)
    slot = info_t[2:4].astype(jnp.int32)
    for e in range(N_EXPERTS):
        slot = slot + jnp.where(expert == e, off[e], 0)
    slots = slot.reshape(2, t // BLK, BLK).transpose(1, 0, 2).reshape(-1)
    starts = jnp.arange(_num_row_tiles(t), dtype=jnp.int32) * TMG
    first_row = jnp.minimum(starts, ends[-1] - 1)
    tile_expert = jnp.sum(ends[None, :] <= first_row[:, None], axis=1)
    n_tiles = (ends[-1] // TMG).reshape(1)
    last_tile = jnp.where(padded > 0, ends - TMG, -1)
    tail = starts[-N_EXPERTS:]
    clear = jnp.concatenate([last_tile, jnp.where(tail >= ends[-1], tail, -1)])
    return (slots.astype(jnp.int32), clear.astype(jnp.int32),
            tile_expert.astype(jnp.int32), n_tiles.astype(jnp.int32))


def _row_copy(src_ref, src_row, dst_ref, dst_row, sem):
    return pltpu.make_async_copy(src_ref.at[pl.ds(src_row, 1), :], dst_ref.at[pl.ds(dst_row, 1), :], sem)


def _dispatch_kernel(slots_ref, clear_ref, h_ref, xs_ref, zero_ref, sem):
    @pl.when(pl.program_id(0) == 0)
    def _():
        zero_ref[...] = jnp.zeros(zero_ref.shape, F32)
        for n in range(2 * N_EXPERTS):
            @pl.when(clear_ref[n] >= 0)
            def _():
                start = pl.multiple_of(clear_ref[n], TMG)
                clear = pltpu.make_async_copy(zero_ref, xs_ref.at[pl.ds(start, TMG), :], sem)
                clear.start()
                clear.wait()

    base = pl.program_id(0) * (2 * BLK)

    def body(r, carry):
        for j in range(2):
            _row_copy(h_ref, r, xs_ref, slots_ref[base + j * BLK + r], sem).start(priority=j)
        return carry

    lax.fori_loop(0, BLK, body, 0, unroll=ROW_UNROLL)
    for j in range(2):
        pltpu.make_async_copy(h_ref, xs_ref.at[pl.ds(0, BLK), :], sem).wait()


def _dispatch(slots, clear, h, n_rows):
    t = h.shape[0]
    return pl.pallas_call(
        _dispatch_kernel,
        grid_spec=pltpu.PrefetchScalarGridSpec(
            num_scalar_prefetch=2,
            grid=(t // BLK,),
            in_specs=[pl.BlockSpec((BLK, D_MODEL), lambda i, s, c: (i, 0))],
            out_specs=pl.BlockSpec(memory_space=pl.ANY),
            scratch_shapes=[pltpu.VMEM((TMG, D_MODEL), F32), pltpu.SemaphoreType.DMA(())],
        ),
        out_shape=jax.ShapeDtypeStruct((n_rows, D_MODEL), F32),
        compiler_params=_params(("arbitrary",)),
        name="moe_dispatch",
    )(slots, clear, h)


def _experts_kernel(te_ref, nt_ref, xs_ref, wg_ref, wu_ref, wd_ref, y_ref, xb_ref, acc_ref):
    del te_ref
    f = pl.program_id(1)
    active = pl.program_id(0) < nt_ref[0]

    @pl.when(f == 0)
    def _():
        xb_ref[...] = xs_ref[...].astype(BF16)
        acc_ref[...] = jnp.zeros(acc_ref.shape, F32)

    @pl.when(active)
    def _():
        h = xb_ref[...]
        part = jnp.zeros((TMG, D_MODEL), F32)
        for cs in _mxu_chunks(TF_G, 1024):
            gate = _dot(h, wg_ref[:, cs])
            up = _dot(h, wu_ref[:, cs])
            part = part + _dot((gate * _sigmoid(gate) * up).astype(BF16), wd_ref[cs, :].astype(BF16))
        acc_ref[...] += part

    @pl.when(f == pl.num_programs(1) - 1)
    def _():
        y_ref[...] = acc_ref[...]


def _experts(tile_expert, n_tiles, xs, wg, wu, wd):
    n_rows = xs.shape[0]
    nf = D_FF_EXPERT // TF_G

    def f_eff(g, f, nt):
        return jnp.where(g < nt[0], f, nf - 1)

    return pl.pallas_call(
        _experts_kernel,
        grid_spec=pltpu.PrefetchScalarGridSpec(
            num_scalar_prefetch=2,
            grid=(n_rows // TMG, nf),
            in_specs=[
                pl.BlockSpec((TMG, D_MODEL), lambda g, f, te, nt: (g, 0)),
                pl.BlockSpec((None, D_MODEL, TF_G), lambda g, f, te, nt: (te[g], 0, f_eff(g, f, nt))),
                pl.BlockSpec((None, D_MODEL, TF_G), lambda g, f, te, nt: (te[g], 0, f_eff(g, f, nt))),
                pl.BlockSpec((None, TF_G, D_MODEL), lambda g, f, te, nt: (te[g], f_eff(g, f, nt), 0)),
            ],
            out_specs=pl.BlockSpec((TMG, D_MODEL), lambda g, f, te, nt: (g, 0)),
            scratch_shapes=[pltpu.VMEM((TMG, D_MODEL), BF16), pltpu.VMEM((TMG, D_MODEL), F32)],
        ),
        out_shape=jax.ShapeDtypeStruct((n_rows, D_MODEL), F32),
        compiler_params=_params(("arbitrary", "arbitrary")),
        name="moe_experts",
    )(tile_expert, n_tiles, xs, wg, wu, wd)


def _combine_kernel(slots_ref, x_ref, info_ref, fg_ref, y_ref, o_ref, rows_ref, sem):
    base = pl.program_id(0) * (2 * BLK)

    def body(r, carry):
        for j in range(2):
            _row_copy(y_ref, slots_ref[base + j * BLK + r], rows_ref.at[j], r, sem).start(priority=j)
        return carry

    lax.fori_loop(0, BLK, body, 0, unroll=ROW_UNROLL)
    for j in range(2):
        pltpu.make_async_copy(y_ref.at[pl.ds(0, BLK), :], rows_ref.at[j], sem).wait()
    w1 = info_ref[:, 4:5]
    w2 = info_ref[:, 5:6]
    o_ref[...] = _rms(x_ref[...] + w1 * rows_ref[0] + w2 * rows_ref[1], fg_ref[...])


def _combine(slots, x, info, final_g, y):
    t = x.shape[0]
    return pl.pallas_call(
        _combine_kernel,
        grid_spec=pltpu.PrefetchScalarGridSpec(
            num_scalar_prefetch=1,
            grid=(t // BLK,),
            in_specs=[pl.BlockSpec((BLK, D_MODEL), lambda i, s: (i, 0)),
                      pl.BlockSpec((BLK, INFO_W), lambda i, s: (i, 0)),
                      pl.BlockSpec((1, D_MODEL), lambda i, s: (0, 0)),
                      pl.BlockSpec(memory_space=pl.ANY)],
            out_specs=pl.BlockSpec((BLK, D_MODEL), lambda i, s: (i, 0)),
            scratch_shapes=[pltpu.VMEM((2, BLK, D_MODEL), F32), pltpu.SemaphoreType.DMA(())],
        ),
        out_shape=jax.ShapeDtypeStruct((t, D_MODEL), F32),
        compiler_params=_params(("arbitrary",)),
        name="moe_combine",
    )(slots, x, info, final_g, y)


def _block_diag(w):
    g, c, _ = w.shape
    eye = jnp.eye(g, dtype=w.dtype)
    return (w[:, :, None, :] * eye[:, None, :, None]).reshape(g * c, g * c)


def _rotate_half_cols(w):
    half = w.shape[-1] // 2
    return jnp.concatenate([-w[..., half:], w[..., :half]], axis=-1)


def _rope_tables():
    pos = np.arange(SEQ, dtype=np.float32)
    inv = np.float32(ROPE_THETA) ** (-np.arange(0, QK_ROPE, 2, dtype=np.float32) / np.float32(QK_ROPE))
    ang = pos[:, None] * inv[None, :]
    cos, sin = jnp.asarray(np.cos(ang)), jnp.asarray(np.sin(ang))
    one = jnp.ones((SEQ, QK_NOPE), F32)
    z_nope = jnp.zeros((SEQ, QK_NOPE), F32)
    z_pad = jnp.zeros((SEQ, HEAD_PAD - QK_NOPE - QK_ROPE), F32)
    z_rest = jnp.zeros((SEQ, LANES - QK_ROPE), F32)
    cq_tab = jnp.concatenate([one, cos, cos, z_pad], axis=1)
    sq_tab = jnp.concatenate([z_nope, sin, sin, z_pad], axis=1)
    ck_tab = jnp.concatenate([cos, cos, z_rest], axis=1)
    sk_tab = jnp.concatenate([sin, sin, z_rest], axis=1)
    return cq_tab, sq_tab, ck_tab, sk_tab


def _rope_placement():
    r = jnp.arange(LANES)[:, None]
    c = jnp.arange(MLA_HEADS * HEAD_PAD)[None, :]
    return ((r < QK_ROPE) & (c % HEAD_PAD == QK_NOPE + r)).astype(BF16)


def _split_in_proj(w):
    bounds = (0, 256, 512, 640, 672, 1184, 1696, w.shape[1])
    zp, cq, ckv, kr, zc, zs, zg = [w[:, a:b] for a, b in zip(bounds[:-1], bounds[1:])]
    pad = jnp.zeros((D_MODEL, LANES - 2 * QK_ROPE), w.dtype)
    small = jnp.concatenate([zc, zs, zp, cq, ckv, kr, _rotate_half_cols(kr), pad], axis=1)
    return small.astype(BF16), zg.astype(BF16)


def _mla_weights(w_uq, w_ukv):
    wq = w_uq.reshape(Q_LORA, MLA_HEADS, QK_NOPE + QK_ROPE)
    nope, rope = wq[..., :QK_NOPE], wq[..., QK_NOPE:]
    zq = jnp.zeros((Q_LORA, MLA_HEADS, HEAD_PAD - QK_NOPE - QK_ROPE), w_uq.dtype)
    wqa = jnp.concatenate([nope, rope, zq], axis=-1).reshape(Q_LORA, MLA_HEADS * HEAD_PAD)
    wqb = jnp.concatenate([jnp.zeros_like(nope), _rotate_half_cols(rope), zq], axis=-1)
    wqb = wqb.reshape(Q_LORA, MLA_HEADS * HEAD_PAD)
    wkv = w_ukv.reshape(KV_LORA, MLA_HEADS, QK_NOPE + V_HEAD)
    k_nope, v = wkv[..., :QK_NOPE], wkv[..., QK_NOPE:]
    zk = jnp.zeros((KV_LORA, MLA_HEADS, HEAD_PAD - QK_NOPE), w_ukv.dtype)
    wk = jnp.concatenate([k_nope, zk], axis=-1).reshape(KV_LORA, MLA_HEADS * HEAD_PAD)
    zv = jnp.zeros((KV_LORA, MLA_HEADS, V_ROWS - V_HEAD), w_ukv.dtype)
    wv = jnp.concatenate([v, zv], axis=-1).reshape(KV_LORA, MLA_HEADS * V_ROWS)
    return wqa.astype(BF16), wqb.astype(BF16), wk.astype(BF16), wv.astype(BF16)


def kernel(x, norm_mix_g, w_in, b_gate, w_pool, pool_scale, w_pool_out, q_norm_g, w_uq, kv_norm_g, w_ukv, w_mla_out, conv_w, conv_b, conv_ln_g, conv_ln_b, w_conv_out, sgu_ln_g, sgu_ln_b, sgu_w, sgu_b, w_sgu_out, w_o, norm_ffn_g, w_ffn_gate, w_ffn_up, w_ffn_down, w_router, b_router, w_moe_gate, w_moe_up, w_moe_down, final_norm_g):
    batch, seq, d = x.shape
    assert (seq, d) == (SEQ, D_MODEL)
    depth = w_in.shape[0]
    xt = x.reshape(batch * seq, d)
    tabs = _rope_tables()
    place = _rope_placement()
    row = lambda a: a.reshape(1, -1)

    for l in range(depth):
        w_small, w_gate = _split_in_proj(w_in[l])
        zs = _in_proj(xt, row(norm_mix_g[l]), w_small)

        w_bd = _block_diag(w_pool[l])
        m_pool = _pool(zs, w_bd.astype(BF16), row(pool_scale[l]))
        m_conv = _conv(zs, conv_w[l], row(conv_b[l]), row(conv_ln_g[l]), row(conv_ln_b[l]))
        bias_full = jnp.repeat(sgu_b[l].T, SGU_WIDTH // SGU_GROUPS, axis=1)
        m_sgu = _sgu(zs, row(sgu_ln_g[l]), row(sgu_ln_b[l]),
                     sgu_w[l].reshape(SGU_GROUPS * SGU_LEN, SGU_LEN), bias_full)
        wqa, wqb, wk, wv = _mla_weights(w_uq[l], w_ukv[l])
        q, k, v = _mla_prep(zs, row(q_norm_g[l]), row(kv_norm_g[l]), wqa, wqb, wk, wv, place, tabs)
        m_mla = _attention(q, k, v, batch)

        xt = _merge(xt, row(norm_mix_g[l]), m_pool, m_mla, m_conv, m_sgu, w_gate, row(b_gate[l]),
                    w_pool_out[l].astype(BF16), w_mla_out[l].astype(BF16),
                    w_conv_out[l].astype(BF16), w_sgu_out[l].astype(BF16), w_o[l].astype(BF16))

        j = l // 2
        if l % 2 == 0:
            xt = _ffn(xt, row(norm_ffn_g[l]), w_ffn_gate[j].astype(BF16),
                      w_ffn_up[j].astype(BF16), w_ffn_down[j])
        else:
            wr = jnp.pad(w_router[j], ((0, 0), (0, LANES - N_EXPERTS)))
            br = jnp.pad(b_router[j], (0, LANES - N_EXPERTS)).reshape(1, LANES)
            t = xt.shape[0]
            h, info, info_t, cnt = _router(xt, row(norm_ffn_g[l]), wr, br)
            slots, clear, tile_expert, n_tiles = _route_plan(info_t, cnt, t)
            xs = _dispatch(slots, clear, h, _num_row_tiles(t) * TMG)
            y = _experts(tile_expert, n_tiles, xs, w_moe_gate[j].astype(BF16),
                         w_moe_up[j].astype(BF16), w_moe_down[j])
            assert l == depth - 1
            xt = _combine(slots, xt, info, row(final_norm_g), y)
    return xt.reshape(batch, seq, d)
```
